```python
import jax, jax.numpy as jnp
from jax import lax
import numpy as np

D_MODEL = 1024
BATCH = 32
SEQ = 256
DEPTH = 2
DEC_BATCH = 4
DEC_SEQ = 2048
PAST_LEN = 512

GRID_W = 64
N_EVEN = (DEPTH + 1) // 2
N_ODD = DEPTH // 2
MLA_HEADS = 8
Q_LORA = 384
KV_LORA = 256
QK_NOPE = 64
QK_ROPE = 32
V_HEAD = 64
CONV_CH = 512
CONV_W = 31
CONV_PAD = CONV_W // 2
GQA_HEADS = 16
GQA_KV_HEADS = 4
GQA_HEAD_DIM = 64
D_FF = 2816
MACARON = 0.5
N_MOD = 9
Q_BLOCK = 128
ROPE_BASE = 10000.0
EPS = 1e-6
IN_A = Q_LORA + KV_LORA + QK_ROPE + 2 * CONV_CH
MIX_A = MLA_HEADS * V_HEAD + CONV_CH
IN_C = (GQA_HEADS + 2 * GQA_KV_HEADS) * GQA_HEAD_DIM
MIX_C = GQA_HEADS * GQA_HEAD_DIM

kernel_name = 'hybrid_mla_conformer_gqa_diffusion_step'


def rms_norm(x, g):
    xf = x.astype(jnp.float32)
    y = xf * lax.rsqrt(jnp.mean(xf * xf, axis=-1, keepdims=True) + EPS)
    return y.astype(x.dtype) * g


def layer_norm(x, g, b):
    xf = x.astype(jnp.float32)
    mu = jnp.mean(xf, axis=-1, keepdims=True)
    var = jnp.mean(jnp.square(xf - mu), axis=-1, keepdims=True)
    return ((xf - mu) * lax.rsqrt(var + EPS)).astype(x.dtype) * g + b


def swiglu(h, w_in, w_out):
    a, b = jnp.split(h @ w_in, 2, axis=-1)
    return (jax.nn.silu(a) * b) @ w_out


def axial_rope(x):
    n, d = x.shape[1], x.shape[-1]
    half = d // 2
    rows = n // GRID_W
    pos_row = jnp.repeat(jnp.arange(rows), GRID_W)
    pos_col = jnp.tile(jnp.arange(GRID_W), rows)
    inv = ROPE_BASE ** (-jnp.arange(0, half, 2, dtype=jnp.float32) / half)

    def rot(xa, pos):
        ang = pos.astype(jnp.float32)[:, None] * inv[None, :]
        cos = jnp.cos(ang)[None, :, None, :].astype(x.dtype)
        sin = jnp.sin(ang)[None, :, None, :].astype(x.dtype)
        x1, x2 = jnp.split(xa, 2, axis=-1)
        return jnp.concatenate([x1 * cos - x2 * sin, x2 * cos + x1 * sin], axis=-1)

    return jnp.concatenate([rot(x[..., :half], pos_row), rot(x[..., half:], pos_col)], axis=-1)


def block_attention(q, k, v, scale):
    B, Sq, Hk, G, dk = q.shape
    nb = Sq // Q_BLOCK
    qb = q.reshape(B, nb, Q_BLOCK, Hk, G, dk).transpose(1, 0, 2, 3, 4, 5)

    def one(qblk):
        s = jnp.einsum('bqhgd,bshd->bhgqs', qblk, k).astype(jnp.float32) * scale
        p = jax.nn.softmax(s, axis=-1).astype(v.dtype)
        return jnp.einsum('bhgqs,bshd->bqhgd', p, v)

    o = lax.map(one, qb)
    return o.transpose(1, 0, 2, 3, 4, 5).reshape(B, Sq, Hk * G, v.shape[-1])


def depthwise_conv(u, w, b):
    y = lax.conv_general_dilated(u, w[:, None, :], window_strides=(1,), padding=[(CONV_PAD, CONV_PAD)],
                                 dimension_numbers=('NWC', 'WIO', 'NWC'), feature_group_count=u.shape[-1])
    return y + b


def mla_keys_values(ckv, kr, w_kv_up):
    B, S, _ = ckv.shape
    kv = (ckv @ w_kv_up).reshape(B, S, MLA_HEADS, QK_NOPE + V_HEAD)
    k_nope, v = kv[..., :QK_NOPE], kv[..., QK_NOPE:]
    k = jnp.concatenate([k_nope, jnp.broadcast_to(kr[:, :, None, :], (B, S, MLA_HEADS, QK_ROPE))], axis=-1)
    return k, v


def mla_conv_mixer(h, w_in, g_ql, w_qu, g_kvl, w_kvu, w_dw, b_dw, g_ln, b_ln, w_out, ctx):
    B, S, _ = h.shape
    cq, ckv, kr, u = jnp.split(h @ w_in, [Q_LORA, Q_LORA + KV_LORA, Q_LORA + KV_LORA + QK_ROPE], axis=-1)
    q = (rms_norm(cq, g_ql) @ w_qu).reshape(B, S, MLA_HEADS, QK_NOPE + QK_ROPE)
    ckv = rms_norm(ckv, g_kvl)
    if ctx is None:
        k, v = mla_keys_values(ckv, kr, w_kvu)
    else:
        q = jnp.concatenate([q[..., :QK_NOPE], axial_rope(q[..., QK_NOPE:])], axis=-1)
        kr_lat = axial_rope(kr[:, :, None, :])[:, :, 0, :]
        k_lat, v_lat = mla_keys_values(ckv, kr_lat, w_kvu)
        k_ctx, v_ctx = mla_keys_values(ctx[0], ctx[1], w_kvu)
        k = jnp.concatenate([k_ctx, k_lat], axis=1)
        v = jnp.concatenate([v_ctx, v_lat], axis=1)
    o = block_attention(q[:, :, :, None, :], k, v, (QK_NOPE + QK_ROPE) ** -0.5)
    attn = o.reshape(B, S, MLA_HEADS * V_HEAD)
    a, b = jnp.split(u, 2, axis=-1)
    g = depthwise_conv(a * jax.nn.sigmoid(b), w_dw, b_dw)
    g = jax.nn.silu(layer_norm(g, g_ln, b_ln))
    out = jnp.concatenate([attn, g], axis=-1) @ w_out
    return out, (ckv, kr)


def gqa_mixer(h, w_in, g_q, g_k, w_out, ctx):
    B, S, _ = h.shape
    q, k, v = jnp.split(h @ w_in, [MIX_C, MIX_C + GQA_KV_HEADS * GQA_HEAD_DIM], axis=-1)
    q = rms_norm(q.reshape(B, S, GQA_HEADS, GQA_HEAD_DIM), g_q)
    k = rms_norm(k.reshape(B, S, GQA_KV_HEADS, GQA_HEAD_DIM), g_k)
    v = v.reshape(B, S, GQA_KV_HEADS, GQA_HEAD_DIM)
    if ctx is None:
        keys, vals = k, v
    else:
        q = axial_rope(q)
        keys = jnp.concatenate([ctx[0], axial_rope(k)], axis=1)
        vals = jnp.concatenate([ctx[1], v], axis=1)
    grp = GQA_HEADS // GQA_KV_HEADS
    o = block_attention(q.reshape(B, S, GQA_KV_HEADS, grp, GQA_HEAD_DIM), keys, vals, GQA_HEAD_DIM ** -0.5)
    return o.reshape(B, S, MIX_C) @ w_out, (k, v)


def trunk(x, cond, ctx, p):
    e = jax.nn.silu(cond)
    saved_a, saved_c = [], []
    for l in range(DEPTH):
        mod = (e @ p['w_mod'][l] + p['b_mod'][l])[:, None, :]
        sh1, sc1, g1, shm, scm, gm, sh2, sc2, g2 = jnp.split(mod, N_MOD, axis=-1)
        h = rms_norm(x, p['g_ff1'][l]) * (1 + sc1) + sh1
        x = x + MACARON * g1 * swiglu(h, p['w_ff1_in'][l], p['w_ff1_out'][l])
        h = rms_norm(x, p['g_mix'][l]) * (1 + scm) + shm
        if l % 2 == 0:
            i = l // 2
            cc = None if ctx is None else (ctx[0][:, i], ctx[1][:, i])
            out, st = mla_conv_mixer(h, p['w_in_a'][i], p['g_q_lora'][i], p['w_q_up'][i], p['g_kv_lora'][i],
                                     p['w_kv_up'][i], p['w_dw'][i], p['b_dw'][i], p['g_conv_ln'][i],
                                     p['b_conv_ln'][i], p['w_out_a'][i], cc)
            saved_a.append(st)
        else:
            i = l // 2
            cc = None if ctx is None else (ctx[2][:, i], ctx[3][:, i])
            out, st = gqa_mixer(h, p['w_in_c'][i], p['g_q_head'][i], p['g_k_head'][i], p['w_out_c'][i], cc)
            saved_c.append(st)
        x = x + gm * out
        h = rms_norm(x, p['g_ff2'][l]) * (1 + sc2) + sh2
        x = x + MACARON * g2 * swiglu(h, p['w_ff2_in'][l], p['w_ff2_out'][l])
    return rms_norm(x, p['g_final']), saved_a, saved_c


def setup_inputs(seed: int = 0) -> dict:
    key = jax.random.key(seed)
    ks = iter(jax.random.split(key, 48))
    f32 = jnp.float32

    def nrm(shape, scale=1.0):
        return jax.random.normal(next(ks), shape, f32) * scale

    def gain(shape):
        return 1.0 + 0.05 * jax.random.normal(next(ks), shape, f32)

    return {
        'x_prompt': nrm((BATCH, SEQ, D_MODEL)),
        'x_sample': nrm((DEC_BATCH, DEC_SEQ, D_MODEL)),
        'cache_mla_ckv': nrm((DEC_BATCH, N_EVEN, PAST_LEN, KV_LORA)),
        'cache_mla_krope': nrm((DEC_BATCH, N_EVEN, PAST_LEN, QK_ROPE)),
        'cache_gqa_k': nrm((DEC_BATCH, N_ODD, PAST_LEN, GQA_KV_HEADS, GQA_HEAD_DIM)),
        'cache_gqa_v': nrm((DEC_BATCH, N_ODD, PAST_LEN, GQA_KV_HEADS, GQA_HEAD_DIM)),
        'c': nrm((DEC_BATCH, D_MODEL)),
        'c_ctx': nrm((D_MODEL,)),
        'g_ff1': gain((DEPTH, D_MODEL)),
        'w_ff1_in': nrm((DEPTH, D_MODEL, 2 * D_FF), D_MODEL ** -0.5),
        'w_ff1_out': nrm((DEPTH, D_FF, D_MODEL), D_FF ** -0.5),
        'g_mix': gain((DEPTH, D_MODEL)),
        'g_ff2': gain((DEPTH, D_MODEL)),
        'w_ff2_in': nrm((DEPTH, D_MODEL, 2 * D_FF), D_MODEL ** -0.5),
        'w_ff2_out': nrm((DEPTH, D_FF, D_MODEL), D_FF ** -0.5),
        'w_mod': nrm((DEPTH, D_MODEL, N_MOD * D_MODEL), D_MODEL ** -0.5),
        'b_mod': nrm((DEPTH, N_MOD * D_MODEL), 0.01),
        'w_in_a': nrm((N_EVEN, D_MODEL, IN_A), D_MODEL ** -0.5),
        'g_q_lora': gain((N_EVEN, Q_LORA)),
        'w_q_up': nrm((N_EVEN, Q_LORA, MLA_HEADS * (QK_NOPE + QK_ROPE)), Q_LORA ** -0.5),
        'g_kv_lora': gain((N_EVEN, KV_LORA)),
        'w_kv_up': nrm((N_EVEN, KV_LORA, MLA_HEADS * (QK_NOPE + V_HEAD)), KV_LORA ** -0.5),
        'w_dw': nrm((N_EVEN, CONV_W, CONV_CH), CONV_W ** -0.5),
        'b_dw': nrm((N_EVEN, CONV_CH), 0.01),
        'g_conv_ln': gain((N_EVEN, CONV_CH)),
        'b_conv_ln': nrm((N_EVEN, CONV_CH), 0.01),
        'w_out_a': nrm((N_EVEN, MIX_A, D_MODEL), MIX_A ** -0.5),
        'w_in_c': nrm((N_ODD, D_MODEL, IN_C), D_MODEL ** -0.5),
        'g_q_head': gain((N_ODD, GQA_HEAD_DIM)),
        'g_k_head': gain((N_ODD, GQA_HEAD_DIM)),
        'w_out_c': nrm((N_ODD, MIX_C, D_MODEL), MIX_C ** -0.5),
        'g_final': gain((D_MODEL,)),
    }


def reference(x_prompt, x_sample, cache_mla_ckv, cache_mla_krope, cache_gqa_k, cache_gqa_v, c, c_ctx,
              g_ff1, w_ff1_in, w_ff1_out, g_mix, g_ff2, w_ff2_in, w_ff2_out, w_mod, b_mod,
              w_in_a, g_q_lora, w_q_up, g_kv_lora, w_kv_up, w_dw, b_dw, g_conv_ln, b_conv_ln, w_out_a,
              w_in_c, g_q_head, g_k_head, w_out_c, g_final):
    p = dict(g_ff1=g_ff1, w_ff1_in=w_ff1_in, w_ff1_out=w_ff1_out, g_mix=g_mix, g_ff2=g_ff2,
             w_ff2_in=w_ff2_in, w_ff2_out=w_ff2_out, w_mod=w_mod, b_mod=b_mod,
             w_in_a=w_in_a, g_q_lora=g_q_lora, w_q_up=w_q_up, g_kv_lora=g_kv_lora, w_kv_up=w_kv_up,
             w_dw=w_dw, b_dw=b_dw, g_conv_ln=g_conv_ln, b_conv_ln=b_conv_ln, w_out_a=w_out_a,
             w_in_c=w_in_c, g_q_head=g_q_head, g_k_head=g_k_head, w_out_c=w_out_c, g_final=g_final)
    y_prompt, saved_a, saved_c = trunk(x_prompt, c_ctx[None, :], None, p)
    new_mla_ckv = jnp.stack([s[0] for s in saved_a], axis=1)
    new_mla_krope = jnp.stack([s[1] for s in saved_a], axis=1)
    new_gqa_k = jnp.stack([s[0] for s in saved_c], axis=1)
    new_gqa_v = jnp.stack([s[1] for s in saved_c], axis=1)
    y_sample, _, _ = trunk(x_sample, c, (cache_mla_ckv, cache_mla_krope, cache_gqa_k, cache_gqa_v), p)
    return (y_prompt, y_sample, new_mla_ckv, new_mla_krope, new_gqa_k, new_gqa_v)
```

```python
import functools

import jax
import jax.numpy as jnp
from jax import lax
from jax.experimental import pallas as pl
from jax.experimental.pallas import tpu as pltpu

F32 = jnp.float32
BF16 = jnp.bfloat16

D_MODEL = 1024
DEPTH = 2
DEC_SEQ = 2048
GRID_W = 64
MLA_HEADS = 8
Q_LORA = 384
KV_LORA = 256
QK_NOPE = 64
QK_ROPE = 32
V_HEAD = 64
CONV_CH = 512
CONV_W = 31
CONV_PAD = CONV_W // 2
GQA_HEADS = 16
GQA_KV_HEADS = 4
GQA_HEAD_DIM = 64
D_FF = 2816
MACARON = 0.5
N_MOD = 9
ROPE_BASE = 10000.0
EPS = 1e-6

LANES = 128
HEAD_PAD = 128
MOD_ROWS = 8
HALO = 16
TM = 512
Q_SUB = 256
FF_CHUNK = 256
CONV_ROWS = 64
VMEM_LIMIT = 56 * 1024 * 1024


def _params(n_axes):
    return pltpu.CompilerParams(dimension_semantics=("parallel",) * n_axes, vmem_limit_bytes=VMEM_LIMIT)


def _const_spec(shape):
    return pl.BlockSpec(shape, lambda *_: (0,) * len(shape), pipeline_mode=pl.Buffered(1))


def _rms(x, g):
    return x * lax.rsqrt(jnp.mean(x * x, axis=-1, keepdims=True) + EPS) * g


def _mod_norm(x, g, mod):
    return _rms(x, g) * (1.0 + mod[1:2]) + mod[0:1]


def _silu(x):
    return x * jax.nn.sigmoid(x)


def _rope(x, cos, sa, sb, quarter):
    w = x.shape[-1]
    return x * cos + pltpu.roll(x, w - quarter, 1) * sa + pltpu.roll(x, quarter, 1) * sb


def _tile_lanes(t, reps):
    return t if reps == 1 else jnp.concatenate([t] * reps, axis=1)


def _mod_kernel(c_ref, w_ref, b_ref, o_ref):
    e = _silu(c_ref[...]).astype(BF16)
    o_ref[...] = jnp.dot(e, w_ref[...].astype(BF16), preferred_element_type=F32) + b_ref[...]


def _modulation(cond, w_mod, b_mod):
    n_out = N_MOD * D_MODEL
    tn = 1536
    out = pl.pallas_call(
        _mod_kernel,
        grid=(DEPTH, n_out // tn),
        in_specs=[
            pl.BlockSpec((MOD_ROWS, D_MODEL), lambda l, j: (0, 0)),
            pl.BlockSpec((None, D_MODEL, tn), lambda l, j: (l, 0, j)),
            pl.BlockSpec((None, 1, tn), lambda l, j: (l, 0, j)),
        ],
        out_specs=pl.BlockSpec((None, MOD_ROWS, tn), lambda l, j: (l, 0, j)),
        out_shape=jax.ShapeDtypeStruct((DEPTH, MOD_ROWS, n_out), F32),
        compiler_params=_params(2),
        name="modulation",
    )(cond, w_mod, b_mod.reshape(DEPTH, 1, n_out))
    return out.reshape(DEPTH * MOD_ROWS * 3, 3, D_MODEL)


def _mod_spec(layer, sub, row_fn):
    return pl.BlockSpec((None, 3, D_MODEL), lambda i, *_: ((layer * MOD_ROWS + row_fn(i)) * 3 + sub, 0, 0))


def _ffn_kernel(x_ref, mod_ref, g_ref, win_ref, wout_ref, *rest, final):
    if final:
        gfin_ref, o_ref = rest
    else:
        (o_ref,) = rest
    x = x_ref[...]
    mod = mod_ref[...]
    h = _mod_norm(x, g_ref[...], mod).astype(BF16)
    acc = jnp.zeros(x.shape, F32)
    for j in range(D_FF // FF_CHUNK):
        lo = j * FF_CHUNK
        a = jnp.dot(h, win_ref[:, lo:lo + FF_CHUNK], preferred_element_type=F32)
        b = jnp.dot(h, win_ref[:, D_FF + lo:D_FF + lo + FF_CHUNK], preferred_element_type=F32)
        act = (_silu(a) * b).astype(BF16)
        acc = acc + jnp.dot(act, wout_ref[lo:lo + FF_CHUNK, :], preferred_element_type=F32)
    y = x + (MACARON * mod[2:3]) * acc
    if final:
        y = _rms(y, gfin_ref[...])
    o_ref[...] = y


def _ffn(x, mod3, layer, sub, row_fn, g, w_in, w_out, g_final=None):
    n = x.shape[0]
    final = g_final is not None
    row = pl.BlockSpec((TM, D_MODEL), lambda i: (i, 0))
    in_specs = [row, _mod_spec(layer, sub, row_fn), _const_spec((1, D_MODEL)),
                _const_spec((D_MODEL, 2 * D_FF)), _const_spec((D_FF, D_MODEL))]
    args = [x, mod3, g.reshape(1, D_MODEL), w_in, w_out]
    if final:
        in_specs.append(_const_spec((1, D_MODEL)))
        args.append(g_final.reshape(1, D_MODEL))
    return pl.pallas_call(
        functools.partial(_ffn_kernel, final=final),
        grid=(n // TM,),
        in_specs=in_specs,
        out_specs=row,
        out_shape=jax.ShapeDtypeStruct((n, D_MODEL), F32),
        compiler_params=_params(1),
        name="ffn_final" if final else "ffn",
    )(*args)


MLA_W1_COLS = Q_LORA + KV_LORA + 2 * CONV_CH + LANES
MLA_QK = MLA_HEADS * HEAD_PAD
KR_LANE = QK_NOPE


def _mla_kv(ckvn, krb, wkv_ref, kt_ref, v_ref, nseq):
    kv = jnp.dot(ckvn.astype(BF16), wkv_ref[...], preferred_element_type=F32)
    k = kv[:, :MLA_QK] + _tile_lanes(krb, MLA_HEADS)
    v_ref[...] = kv[:, MLA_QK:].astype(BF16)
    rows = k.shape[0] // nseq
    for s in range(nseq):
        kt_ref[s] = k[s * rows:(s + 1) * rows, :].T.astype(BF16)


def _mla_in_kernel(x_ref, mod_ref, g_ref, w1_ref, gq_ref, wqu_ref, gkv_ref, wkv_ref, *rest, rope, nseq):
    if rope:
        cos_ref, sa_ref, sb_ref, q_ref, kt_ref, v_ref, glu_ref = rest
    else:
        q_ref, kt_ref, v_ref, glu_ref, ckv_ref, kr_ref = rest
    h = _mod_norm(x_ref[...], g_ref[...], mod_ref[...]).astype(BF16)
    proj = jnp.dot(h, w1_ref[...], preferred_element_type=F32)
    o1 = Q_LORA
    o2 = o1 + KV_LORA
    o3 = o2 + CONV_CH
    o4 = o3 + CONV_CH
    cq, ckv, ua, ub, krb = proj[:, :o1], proj[:, o1:o2], proj[:, o2:o3], proj[:, o3:o4], proj[:, o4:]
    glu_ref[...] = ua * jax.nn.sigmoid(ub)
    q = jnp.dot(_rms(cq, gq_ref[...]).astype(BF16), wqu_ref[...], preferred_element_type=F32)
    ckvn = _rms(ckv, gkv_ref[...])
    if rope:
        cos, sa, sb = cos_ref[...], sa_ref[...], sb_ref[...]
        quarter = QK_ROPE // 4
        krb = _rope(krb, cos, sa, sb, quarter)
        q = _rope(q, _tile_lanes(cos, MLA_HEADS), _tile_lanes(sa, MLA_HEADS), _tile_lanes(sb, MLA_HEADS), quarter)
    else:
        ckv_ref[...] = ckvn
        kr_ref[...] = krb[:, KR_LANE:KR_LANE + QK_ROPE]
    q_ref[...] = (q * ((QK_NOPE + QK_ROPE) ** -0.5)).astype(BF16)
    _mla_kv(ckvn, krb, wkv_ref, kt_ref, v_ref, nseq)


def _mla_ctx_kernel(ckv_ref, krb_ref, wkv_ref, kt_ref, v_ref):
    _mla_kv(ckv_ref[...], krb_ref[...], wkv_ref, kt_ref, v_ref, 1)


def _mla_in(x, mod3, layer, row_fn, seq, p, tables):
    n = x.shape[0]
    rope = tables is not None
    nseq = max(TM // seq, 1)
    rows = TM // nseq
    batch = n // seq
    tiles_per_seq = max(seq // TM, 1)
    row = lambda w: pl.BlockSpec((TM, w), lambda i: (i, 0))
    in_specs = [row(D_MODEL), _mod_spec(layer, 1, row_fn), _const_spec((1, D_MODEL)),
                _const_spec((D_MODEL, MLA_W1_COLS)), _const_spec((1, Q_LORA)), _const_spec((Q_LORA, MLA_QK)),
                _const_spec((1, KV_LORA)), _const_spec((KV_LORA, 2 * MLA_QK))]
    args = [x, mod3, p["g_mix"], p["w1"], p["g_q"], p["w_qu"], p["g_kv"], p["w_kv"]]
    kt_spec = pl.BlockSpec((nseq, MLA_QK, rows), lambda i: (i // tiles_per_seq, 0, i % tiles_per_seq))
    out_specs = [row(MLA_QK), kt_spec, row(MLA_QK), row(CONV_CH)]
    out_shape = [jax.ShapeDtypeStruct((n, MLA_QK), BF16), jax.ShapeDtypeStruct((batch, MLA_QK, seq), BF16),
                 jax.ShapeDtypeStruct((n, MLA_QK), BF16), jax.ShapeDtypeStruct((n, CONV_CH), F32)]
    if rope:
        tab = pl.BlockSpec((TM, LANES), lambda i: (i % tiles_per_seq, 0))
        in_specs += [tab, tab, tab]
        args += list(tables)
    else:
        out_specs += [row(KV_LORA), row(QK_ROPE)]
        out_shape += [jax.ShapeDtypeStruct((n, KV_LORA), F32), jax.ShapeDtypeStruct((n, QK_ROPE), F32)]
    return pl.pallas_call(
        functools.partial(_mla_in_kernel, rope=rope, nseq=nseq),
        grid=(n // TM,),
        in_specs=in_specs,
        out_specs=out_specs,
        out_shape=out_shape,
        compiler_params=_params(1),
        name="mla_in_rope" if rope else "mla_in",
    )(*args)


def _mla_ctx(ckv, krb, w_kv, seq):
    n = ckv.shape[0]
    return pl.pallas_call(
        _mla_ctx_kernel,
        grid=(n // seq,),
        in_specs=[pl.BlockSpec((seq, KV_LORA), lambda i: (i, 0)), pl.BlockSpec((seq, LANES), lambda i: (i, 0)),
                  _const_spec((KV_LORA, 2 * MLA_QK))],
        out_specs=[pl.BlockSpec((1, MLA_QK, seq), lambda i: (i, 0, 0)), pl.BlockSpec((seq, MLA_QK), lambda i: (i, 0))],
        out_shape=[jax.ShapeDtypeStruct((n // seq, MLA_QK, seq), BF16), jax.ShapeDtypeStruct((n, MLA_QK), BF16)],
        compiler_params=_params(1),
        name="mla_ctx",
    )(ckv, krb, w_kv)


GQA_Q = GQA_HEADS * GQA_HEAD_DIM
GQA_KV = GQA_KV_HEADS * GQA_HEAD_DIM
GQA_VPAD = GQA_KV_HEADS * HEAD_PAD
GQA_W_COLS = GQA_Q + 2 * GQA_KV + GQA_VPAD


def _head_rms(x, g):
    tm, w = x.shape
    lo = lax.broadcasted_iota(jnp.int32, (tm, LANES), 1) < GQA_HEAD_DIM
    outs = []
    for b in range(w // LANES):
        xb = x[:, b * LANES:(b + 1) * LANES]
        sq = xb * xb
        s_lo = jnp.sum(jnp.where(lo, sq, 0.0), axis=-1, keepdims=True)
        s_hi = jnp.sum(jnp.where(lo, 0.0, sq), axis=-1, keepdims=True)
        ms = jnp.where(lo, s_lo, s_hi) * (1.0 / GQA_HEAD_DIM)
        outs.append(xb * lax.rsqrt(ms + EPS))
    return jnp.concatenate(outs, axis=1) * g


def _gqa_in_kernel(x_ref, mod_ref, g_ref, w_ref, gq_ref, gk_ref, *rest, rope, nseq):
    if rope:
        cos_ref, sa_ref, sb_ref, q_ref, kt_ref, v_ref = rest
    else:
        q_ref, kt_ref, v_ref, kc_ref, vc_ref = rest
    h = _mod_norm(x_ref[...], g_ref[...], mod_ref[...]).astype(BF16)
    proj = jnp.dot(h, w_ref[...], preferred_element_type=F32)
    q = _head_rms(proj[:, :GQA_Q], gq_ref[...])
    k = _head_rms(proj[:, GQA_Q:GQA_Q + GQA_KV], gk_ref[...])
    if rope:
        cos, sa, sb = cos_ref[...], sa_ref[...], sb_ref[...]
        quarter = GQA_HEAD_DIM // 4
        q = _rope(q, _tile_lanes(cos, GQA_Q // LANES), _tile_lanes(sa, GQA_Q // LANES),
                  _tile_lanes(sb, GQA_Q // LANES), quarter)
        k = _rope(k, _tile_lanes(cos, GQA_KV // LANES), _tile_lanes(sa, GQA_KV // LANES),
                  _tile_lanes(sb, GQA_KV // LANES), quarter)
    else:
        kc_ref[...] = k
        vc_ref[...] = proj[:, GQA_Q + GQA_KV:GQA_Q + 2 * GQA_KV]
    q_ref[...] = (q * (GQA_HEAD_DIM ** -0.5)).astype(BF16)
    v_ref[...] = proj[:, GQA_Q + 2 * GQA_KV:].astype(BF16)
    rows = k.shape[0] // nseq
    for s in range(nseq):
        kt_ref[s] = k[s * rows:(s + 1) * rows, :].T.astype(BF16)


def _gqa_ctx_kernel(k_ref, v_ref, kt_ref, vo_ref):
    kt_ref[0] = k_ref[...].T.astype(BF16)
    vo_ref[...] = v_ref[...].astype(BF16)


def _gqa_in(x, mod3, layer, row_fn, seq, p, tables):
    n = x.shape[0]
    rope = tables is not None
    nseq = max(TM // seq, 1)
    rows = TM // nseq
    batch = n // seq
    tiles_per_seq = max(seq // TM, 1)
    row = lambda w: pl.BlockSpec((TM, w), lambda i: (i, 0))
    in_specs = [row(D_MODEL), _mod_spec(layer, 1, row_fn), _const_spec((1, D_MODEL)),
                _const_spec((D_MODEL, GQA_W_COLS)), _const_spec((1, GQA_Q)), _const_spec((1, GQA_KV))]
    args = [x, mod3, p["g_mix"], p["w"], p["g_q"], p["g_k"]]
    kt_spec = pl.BlockSpec((nseq, GQA_KV, rows), lambda i: (i // tiles_per_seq, 0, i % tiles_per_seq))
    out_specs = [row(GQA_Q), kt_spec, row(GQA_VPAD)]
    out_shape = [jax.ShapeDtypeStruct((n, GQA_Q), BF16), jax.ShapeDtypeStruct((batch, GQA_KV, seq), BF16),
                 jax.ShapeDtypeStruct((n, GQA_VPAD), BF16)]
    if rope:
        tab = pl.BlockSpec((TM, LANES), lambda i: (i % tiles_per_seq, 0))
        in_specs += [tab, tab, tab]
        args += list(tables)
    else:
        out_specs += [row(GQA_KV), row(GQA_KV)]
        out_shape += [jax.ShapeDtypeStruct((n, GQA_KV), F32), jax.ShapeDtypeStruct((n, GQA_KV), F32)]
    return pl.pallas_call(
        functools.partial(_gqa_in_kernel, rope=rope, nseq=nseq),
        grid=(n // TM,),
        in_specs=in_specs,
        out_specs=out_specs,
        out_shape=out_shape,
        compiler_params=_params(1),
        name="gqa_in_rope" if rope else "gqa_in",
    )(*args)


def _gqa_ctx(k, vpad, seq):
    n = k.shape[0]
    return pl.pallas_call(
        _gqa_ctx_kernel,
        grid=(n // seq,),
        in_specs=[pl.BlockSpec((seq, GQA_KV), lambda i: (i, 0)), pl.BlockSpec((seq, GQA_VPAD), lambda i: (i, 0))],
        out_specs=[pl.BlockSpec((1, GQA_KV, seq), lambda i: (i, 0, 0)), pl.BlockSpec((seq, GQA_VPAD), lambda i: (i, 0))],
        out_shape=[jax.ShapeDtypeStruct((n // seq, GQA_KV, seq), BF16), jax.ShapeDtypeStruct((n, GQA_VPAD), BF16)],
        compiler_params=_params(1),
        name="gqa_ctx",
    )(k, vpad)


def _attn_kernel(*refs, nseg, bt, seq, nq, nkv, dq, dk, dv):
    q_ref = refs[0]
    segs = [(refs[1 + 2 * s], refs[2 + 2 * s]) for s in range(nseg)]
    o_ref = refs[1 + 2 * nseg]

    def sub_tile(bi, r0):
        rows = pl.ds(bi * seq + r0, Q_SUB)
        outs = []
        for j in range(nq):
            kv = j * nkv // nq
            qs = q_ref[rows, j * dq:(j + 1) * dq]
            scores = [jnp.dot(qs, kt_ref[bi, kv * dk:(kv + 1) * dk, :], preferred_element_type=F32)
                      for kt_ref, _ in segs]
            m = functools.reduce(jnp.maximum, [jnp.max(s, axis=-1, keepdims=True) for s in scores])
            probs = [jnp.exp(s - m) for s in scores]
            denom = functools.reduce(jnp.add, [jnp.sum(p, axis=-1, keepdims=True) for p in probs])
            o = None
            for p, (_, v_ref) in zip(probs, segs):
                nk = v_ref.shape[0] // bt
                pv = jnp.dot(p.astype(BF16), v_ref[bi * nk:(bi + 1) * nk, kv * HEAD_PAD:(kv + 1) * HEAD_PAD],
                             preferred_element_type=F32)
                o = pv if o is None else o + pv
            outs.append(o[:, :dv] / denom)
        o_ref[rows, :] = jnp.concatenate(outs, axis=1).astype(BF16)

    for bi in range(bt):
        if seq == Q_SUB:
            sub_tile(bi, 0)
        else:
            def body(r, carry, bi=bi):
                sub_tile(bi, pl.multiple_of(r * Q_SUB, Q_SUB))
                return carry
            lax.fori_loop(0, seq // Q_SUB, body, 0)


def _attention(q, segs, *, seq, heads, kv_heads, dq, dk, dv, bt, heads_per_step):
    n = q.shape[0]
    batch = n // seq
    nq = heads_per_step
    steps = heads // nq
    nkv = max(kv_heads // steps, 1)
    q_per_kv_block = steps // (kv_heads // nkv)
    in_specs = [pl.BlockSpec((bt * seq, nq * dq), lambda b, p: (b, p))]
    args = [q]
    for kt, v in segs:
        nk = kt.shape[2]
        in_specs.append(pl.BlockSpec((bt, nkv * dk, nk), lambda b, p: (b, p // q_per_kv_block, 0)))
        in_specs.append(pl.BlockSpec((bt * nk, nkv * HEAD_PAD), lambda b, p: (b, p // q_per_kv_block)))
        args += [kt, v]
    return pl.pallas_call(
        functools.partial(_attn_kernel, nseg=len(segs), bt=bt, seq=seq, nq=nq, nkv=nkv, dq=dq, dk=dk, dv=dv),
        grid=(batch // bt, steps),
        in_specs=in_specs,
        out_specs=pl.BlockSpec((bt * seq, nq * dv), lambda b, p: (b, p)),
        out_shape=jax.ShapeDtypeStruct((n, heads * dv), BF16),
        compiler_params=_params(2),
        name="attention",
    )(*args)


def _mla_out_kernel(x_ref, mod_ref, attn_ref, glu_ref, *rest, nseq, halo, tiles_per_seq):
    if halo:
        prev_ref, next_ref, wdw_ref, bdw_ref, gln_ref, bln_ref, wout_ref, y_ref, pad_ref, cat_ref = rest
    else:
        wdw_ref, bdw_ref, gln_ref, bln_ref, wout_ref, y_ref, pad_ref, cat_ref = rest
    rows = TM // nseq
    zeros = jnp.zeros((HALO, CONV_CH), F32)
    for s in range(nseq):
        if halo:
            j = pl.program_id(0) % tiles_per_seq
            pad_ref[s, 0:HALO, :] = jnp.where(j == 0, zeros, prev_ref[...])
            pad_ref[s, HALO + rows:2 * HALO + rows, :] = jnp.where(j == tiles_per_seq - 1, zeros, next_ref[...])
        else:
            pad_ref[s, 0:HALO, :] = zeros
            pad_ref[s, HALO + rows:2 * HALO + rows, :] = zeros
        pad_ref[s, HALO:HALO + rows, :] = glu_ref[s * rows:(s + 1) * rows, :]
    cat_ref[:, :CONV_CH] = attn_ref[...]
    for s in range(nseq):
        for c in range(rows // CONV_ROWS):
            r0 = c * CONV_ROWS
            acc = jnp.broadcast_to(bdw_ref[...], (CONV_ROWS, CONV_CH))
            for k in range(CONV_W):
                start = r0 + HALO - CONV_PAD + k
                acc = acc + pad_ref[s, start:start + CONV_ROWS, :] * wdw_ref[k:k + 1, :]
            mu = jnp.mean(acc, axis=-1, keepdims=True)
            d = acc - mu
            var = jnp.mean(d * d, axis=-1, keepdims=True)
            g = _silu(d * lax.rsqrt(var + EPS) * gln_ref[...] + bln_ref[...])
            cat_ref[s * rows + r0:s * rows + r0 + CONV_ROWS, CONV_CH:] = g.astype(BF16)
    out = jnp.dot(cat_ref[...], wout_ref[...], preferred_element_type=F32)
    y_ref[...] = x_ref[...] + mod_ref[2:3, :] * out


def _mla_out(x, mod3, layer, row_fn, seq, attn, glu, p):
    n = x.shape[0]
    nseq = max(TM // seq, 1)
    tiles_per_seq = max(seq // TM, 1)
    halo = tiles_per_seq > 1
    row = lambda w: pl.BlockSpec((TM, w), lambda i: (i, 0))
    in_specs = [row(D_MODEL), _mod_spec(layer, 1, row_fn), row(CONV_CH), row(CONV_CH)]
    args = [x, mod3, attn, glu]
    if halo:
        per = TM // HALO
        last = n // HALO - 1
        in_specs += [pl.BlockSpec((HALO, CONV_CH), lambda i: (jnp.maximum(i * per - 1, 0), 0)),
                     pl.BlockSpec((HALO, CONV_CH), lambda i: (jnp.minimum((i + 1) * per, last), 0))]
        args += [glu, glu]
    in_specs += [_const_spec((CONV_W + 1, CONV_CH)), _const_spec((1, CONV_CH)), _const_spec((1, CONV_CH)),
                 _const_spec((1, CONV_CH)), _const_spec((2 * CONV_CH, D_MODEL))]
    args += [p["w_dw"], p["b_dw"], p["g_ln"], p["b_ln"], p["w_out"]]
    return pl.pallas_call(
        functools.partial(_mla_out_kernel, nseq=nseq, halo=halo, tiles_per_seq=tiles_per_seq),
        grid=(n // TM,),
        in_specs=in_specs,
        out_specs=row(D_MODEL),
        out_shape=jax.ShapeDtypeStruct((n, D_MODEL), F32),
        scratch_shapes=[pltpu.VMEM((nseq, TM // nseq + 2 * HALO, CONV_CH), F32),
                        pltpu.VMEM((TM, 2 * CONV_CH), BF16)],
        compiler_params=_params(1),
        name="mla_out",
    )(*args)


def _gqa_out_kernel(x_ref, mod_ref, attn_ref, wout_ref, y_ref):
    out = jnp.dot(attn_ref[...], wout_ref[...], preferred_element_type=F32)
    y_ref[...] = x_ref[...] + mod_ref[2:3, :] * out


def _gqa_out(x, mod3, layer, row_fn, attn, w_out):
    n = x.shape[0]
    row = lambda w: pl.BlockSpec((TM, w), lambda i: (i, 0))
    return pl.pallas_call(
        _gqa_out_kernel,
        grid=(n // TM,),
        in_specs=[row(D_MODEL), _mod_spec(layer, 1, row_fn), row(GQA_Q), _const_spec((GQA_Q, D_MODEL))],
        out_specs=row(D_MODEL),
        out_shape=jax.ShapeDtypeStruct((n, D_MODEL), F32),
        compiler_params=_params(1),
        name="gqa_out",
    )(x, mod3, attn, w_out)


def _rope_tables(d, offset, period):
    half = d // 2
    quarter = half // 2
    rows = DEC_SEQ // GRID_W
    pos_row = jnp.repeat(jnp.arange(rows), GRID_W)
    pos_col = jnp.tile(jnp.arange(GRID_W), rows)
    inv = ROPE_BASE ** (-jnp.arange(0, half, 2, dtype=F32) / half)
    ang_r = pos_row.astype(F32)[:, None] * inv[None, :]
    ang_c = pos_col.astype(F32)[:, None] * inv[None, :]
    zero = jnp.zeros((DEC_SEQ, quarter), F32)
    cos = jnp.concatenate([jnp.cos(ang_r), jnp.cos(ang_r), jnp.cos(ang_c), jnp.cos(ang_c)], axis=1)
    sa = jnp.concatenate([-jnp.sin(ang_r), zero, -jnp.sin(ang_c), zero], axis=1)
    sb = jnp.concatenate([zero, jnp.sin(ang_r), zero, jnp.sin(ang_c)], axis=1)

    def embed(t, fill):
        blk = jnp.concatenate([jnp.full((DEC_SEQ, offset), fill, F32), t,
                               jnp.full((DEC_SEQ, period - offset - d), fill, F32)], axis=1)
        return jnp.tile(blk, (1, LANES // period))

    return embed(cos, 1.0), embed(sa, 0.0), embed(sb, 0.0)


def _pad_heads(w, heads, dim):
    k = w.shape[0]
    return jnp.pad(w.reshape(k, heads, dim), ((0, 0), (0, 0), (0, HEAD_PAD - dim))).reshape(k, heads * HEAD_PAD)


def _prep_mla(i, g_mix_l, w_in_a, g_q_lora, w_q_up, g_kv_lora, w_kv_up, w_dw, b_dw, g_conv_ln, b_conv_ln, w_out_a):
    w = w_in_a[i]
    o1 = Q_LORA
    o2 = o1 + KV_LORA
    o3 = o2 + QK_ROPE
    zeros = lambda c: jnp.zeros((D_MODEL, c), F32)
    w1 = jnp.concatenate([w[:, :o2], w[:, o3:], zeros(KR_LANE), w[:, o2:o3],
                          zeros(LANES - KR_LANE - QK_ROPE)], axis=1)
    kvu = w_kv_up[i].reshape(KV_LORA, MLA_HEADS, QK_NOPE + V_HEAD)
    w_kn = _pad_heads(kvu[:, :, :QK_NOPE].reshape(KV_LORA, MLA_HEADS * QK_NOPE), MLA_HEADS, QK_NOPE)
    w_v = _pad_heads(kvu[:, :, QK_NOPE:].reshape(KV_LORA, MLA_HEADS * V_HEAD), MLA_HEADS, V_HEAD)
    return dict(
        g_mix=g_mix_l.reshape(1, D_MODEL),
        w1=w1.astype(BF16),
        g_q=g_q_lora[i].reshape(1, Q_LORA),
        w_qu=_pad_heads(w_q_up[i], MLA_HEADS, QK_NOPE + QK_ROPE).astype(BF16),
        g_kv=g_kv_lora[i].reshape(1, KV_LORA),
        w_kv=jnp.concatenate([w_kn, w_v], axis=1).astype(BF16),
        w_dw=jnp.pad(w_dw[i], ((0, 1), (0, 0))),
        b_dw=b_dw[i].reshape(1, CONV_CH),
        g_ln=g_conv_ln[i].reshape(1, CONV_CH),
        b_ln=b_conv_ln[i].reshape(1, CONV_CH),
        w_out=w_out_a[i].astype(BF16),
    )


def _prep_gqa(i, g_mix_l, w_in_c, g_q_head, g_k_head, w_out_c):
    w = w_in_c[i]
    v_pad = _pad_heads(w[:, GQA_Q + GQA_KV:], GQA_KV_HEADS, GQA_HEAD_DIM)
    return dict(
        g_mix=g_mix_l.reshape(1, D_MODEL),
        w=jnp.concatenate([w, v_pad], axis=1).astype(BF16),
        g_q=jnp.tile(g_q_head[i], GQA_HEADS).reshape(1, GQA_Q),
        g_k=jnp.tile(g_k_head[i], GQA_KV_HEADS).reshape(1, GQA_KV),
        w_out=w_out_c[i].astype(BF16),
    )


def _trunk(x, mod3, row_fn, seq, layers, ffn_w, g_final, ctx, tables):
    sample = ctx is not None
    batch = x.shape[0] // seq
    saved = {}
    for l in range(DEPTH):
        fw = ffn_w[l]
        x = _ffn(x, mod3, l, 0, row_fn, fw["g1"], fw["w1_in"], fw["w1_out"])
        p = layers[l]
        if l % 2 == 0:
            outs = _mla_in(x, mod3, l, row_fn, seq, p, tables["mla"] if sample else None)
            q, kt, v, glu = outs[:4]
            segs = [(kt, v)]
            if sample:
                segs = [_mla_ctx(ctx["mla_ckv"], ctx["mla_krb"], p["w_kv"], ctx["past"])] + segs
                hps, bt = 2, 1
            else:
                saved["mla"] = outs[4:]
                hps, bt = MLA_HEADS, 2
            attn = _attention(q, segs, seq=seq, heads=MLA_HEADS, kv_heads=MLA_HEADS, dq=HEAD_PAD, dk=HEAD_PAD,
                              dv=V_HEAD, bt=bt, heads_per_step=hps)
            x = _mla_out(x, mod3, l, row_fn, seq, attn, glu, p)
        else:
            outs = _gqa_in(x, mod3, l, row_fn, seq, p, tables["gqa"] if sample else None)
            q, kt, v = outs[:3]
            segs = [(kt, v)]
            if sample:
                segs = [_gqa_ctx(ctx["gqa_k"], ctx["gqa_vpad"], ctx["past"])] + segs
                hps, bt = 2, 1
            else:
                saved["gqa"] = outs[3:]
                hps, bt = GQA_HEADS, 2
            attn = _attention(q, segs, seq=seq, heads=GQA_HEADS, kv_heads=GQA_KV_HEADS, dq=GQA_HEAD_DIM,
                              dk=GQA_HEAD_DIM, dv=GQA_HEAD_DIM, bt=bt, heads_per_step=hps)
            x = _gqa_out(x, mod3, l, row_fn, attn, p["w_out"])
        x = _ffn(x, mod3, l, 2, row_fn, fw["g2"], fw["w2_in"], fw["w2_out"],
                 g_final=g_final if l == DEPTH - 1 else None)
    return x, saved


def kernel(x_prompt, x_sample, cache_mla_ckv, cache_mla_krope, cache_gqa_k, cache_gqa_v, c, c_ctx, g_ff1, w_ff1_in, w_ff1_out, g_mix, g_ff2, w_ff2_in, w_ff2_out, w_mod, b_mod, w_in_a, g_q_lora, w_q_up, g_kv_lora, w_kv_up, w_dw, b_dw, g_conv_ln, b_conv_ln, w_out_a, w_in_c, g_q_head, g_k_head, w_out_c, g_final):
    batch, seq, _ = x_prompt.shape
    dec_batch, dec_seq, _ = x_sample.shape
    past = cache_mla_ckv.shape[2]
    assert DEPTH == 2 and dec_seq == DEC_SEQ and 1 + dec_batch <= MOD_ROWS

    cond = jnp.concatenate([c_ctx[None, :], c, jnp.zeros((MOD_ROWS - 1 - dec_batch, D_MODEL), F32)], axis=0)
    mod3 = _modulation(cond, w_mod, b_mod)

    ffn_w = [dict(g1=g_ff1[l], w1_in=w_ff1_in[l].astype(BF16), w1_out=w_ff1_out[l].astype(BF16),
                  g2=g_ff2[l], w2_in=w_ff2_in[l].astype(BF16), w2_out=w_ff2_out[l].astype(BF16))
             for l in range(DEPTH)]
    layers = [
        _prep_mla(0, g_mix[0], w_in_a, g_q_lora, w_q_up, g_kv_lora, w_kv_up, w_dw, b_dw, g_conv_ln, b_conv_ln,
                  w_out_a),
        _prep_gqa(0, g_mix[1], w_in_c, g_q_head, g_k_head, w_out_c),
    ]
    tables = dict(mla=_rope_tables(QK_ROPE, KR_LANE, HEAD_PAD), gqa=_rope_tables(GQA_HEAD_DIM, 0, GQA_HEAD_DIM))

    y_prompt, saved = _trunk(x_prompt.reshape(batch * seq, D_MODEL), mod3, lambda i: 0, seq, layers, ffn_w,
                             g_final, None, tables)
    ckv_new, kr_new = saved["mla"]
    k_new, v_new = saved["gqa"]

    ctx = dict(
        past=past,
        mla_ckv=cache_mla_ckv[:, 0].reshape(dec_batch * past, KV_LORA),
        mla_krb=jnp.pad(cache_mla_krope[:, 0].reshape(dec_batch * past, QK_ROPE),
                        ((0, 0), (KR_LANE, LANES - KR_LANE - QK_ROPE))),
        gqa_k=cache_gqa_k[:, 0].reshape(dec_batch * past, GQA_KV),
        gqa_vpad=jnp.pad(cache_gqa_v[:, 0], ((0, 0), (0, 0), (0, 0), (0, HEAD_PAD - GQA_HEAD_DIM))
                         ).reshape(dec_batch * past, GQA_VPAD),
    )
    y_sample, _ = _trunk(x_sample.reshape(dec_batch * dec_seq, D_MODEL), mod3,
                         lambda i: 1 + (i * TM) // DEC_SEQ, dec_seq, layers, ffn_w, g_final, ctx, tables)

    return (y_prompt.reshape(batch, seq, D_MODEL),
            y_sample.reshape(dec_batch, dec_seq, D_MODEL),
            ckv_new.reshape(batch, 1, seq, KV_LORA),
            kr_new.reshape(batch, 1, seq, QK_ROPE),
            k_new.reshape(batch, 1, seq, GQA_KV_HEADS, GQA_HEAD_DIM),
            v_new.reshape(batch, 1, seq, GQA_KV_HEADS, GQA_HEAD_DIM))
```

```python
import functools

import jax
import jax.numpy as jnp
from jax import lax
from jax.experimental import pallas as pl
from jax.experimental.pallas import tpu as pltpu

F32 = jnp.float32
BF16 = jnp.bfloat16

D_MODEL = 1024
DEPTH = 2
DEC_SEQ = 2048
GRID_W = 64
MLA_HEADS = 8
Q_LORA = 384
KV_LORA = 256
QK_NOPE = 64
QK_ROPE = 32
V_HEAD = 64
CONV_CH = 512
CONV_W = 31
CONV_PAD = CONV_W // 2
GQA_HEADS = 16
GQA_KV_HEADS = 4
GQA_HEAD_DIM = 64
D_FF = 2816
MACARON = 0.5
N_MOD = 9
ROPE_BASE = 10000.0
EPS = 1e-6

LANES = 128
HEAD_PAD = 128
MOD_ROWS = 8
DENOM_LANE = 64
LOG2E = 1.4426950408889634
HALO = 16
TM = 512
Q_SUB = 256
FF_CHUNK = 256
CONV_ROWS = 64
VMEM_LIMIT = 56 * 1024 * 1024


def _params(n_axes):
    return pltpu.CompilerParams(dimension_semantics=("parallel",) * n_axes, vmem_limit_bytes=VMEM_LIMIT)


def _const_spec(shape):
    return pl.BlockSpec(shape, lambda *_: (0,) * len(shape), pipeline_mode=pl.Buffered(1))


def _rms(x, g):
    return x * lax.rsqrt(jnp.mean(x * x, axis=-1, keepdims=True) + EPS) * g


def _mod_norm(x, g, mod):
    return _rms(x, g) * (1.0 + mod[1:2]) + mod[0:1]


def _silu(x):
    return x * jax.nn.sigmoid(x)


def _rope(x, cos, sa, sb, quarter):
    w = x.shape[-1]
    return x * cos + pltpu.roll(x, w - quarter, 1) * sa + pltpu.roll(x, quarter, 1) * sb


def _with_ones_lane(v):
    lane = lax.broadcasted_iota(jnp.int32, v.shape, 1)
    return jnp.where(lane % HEAD_PAD == DENOM_LANE, 1.0, v)


def _tile_lanes(t, reps):
    return t if reps == 1 else jnp.concatenate([t] * reps, axis=1)


def _mod_kernel(c_ref, w_ref, b_ref, o_ref):
    e = _silu(c_ref[...]).astype(BF16)
    o_ref[...] = jnp.dot(e, w_ref[...].astype(BF16), preferred_element_type=F32) + b_ref[...]


def _modulation(cond, w_mod, b_mod):
    n_out = N_MOD * D_MODEL
    tn = 1536
    out = pl.pallas_call(
        _mod_kernel,
        grid=(DEPTH, n_out // tn),
        in_specs=[
            pl.BlockSpec((MOD_ROWS, D_MODEL), lambda l, j: (0, 0)),
            pl.BlockSpec((None, D_MODEL, tn), lambda l, j: (l, 0, j)),
            pl.BlockSpec((None, 1, tn), lambda l, j: (l, 0, j)),
        ],
        out_specs=pl.BlockSpec((None, MOD_ROWS, tn), lambda l, j: (l, 0, j)),
        out_shape=jax.ShapeDtypeStruct((DEPTH, MOD_ROWS, n_out), F32),
        compiler_params=_params(2),
        name="modulation",
    )(cond, w_mod, b_mod.reshape(DEPTH, 1, n_out))
    return out.reshape(DEPTH * MOD_ROWS * 3, 3, D_MODEL)


def _mod_spec(layer, sub, row_fn):
    return pl.BlockSpec((None, 3, D_MODEL), lambda i, *_: ((layer * MOD_ROWS + row_fn(i)) * 3 + sub, 0, 0))


def _ffn_kernel(*refs, nparts, nblk0, final):
    x_refs = refs[:nparts]
    mod_ref, g_ref, win_ref, wout_ref = refs[nparts:nparts + 4]
    rest = refs[nparts + 4:]
    if final:
        gfin_ref, o_ref = rest
    else:
        (o_ref,) = rest
    x = x_refs[0][...]
    if nparts == 2:
        x = jnp.where(pl.program_id(0) < nblk0, x, x_refs[1][...])
    mod = mod_ref[...]
    h = _mod_norm(x, g_ref[...], mod).astype(BF16)
    acc = jnp.zeros(x.shape, F32)
    for j in range(D_FF // FF_CHUNK):
        lo = j * FF_CHUNK
        a = jnp.dot(h, win_ref[:, lo:lo + FF_CHUNK].astype(BF16), preferred_element_type=F32)
        b = jnp.dot(h, win_ref[:, D_FF + lo:D_FF + lo + FF_CHUNK].astype(BF16), preferred_element_type=F32)
        act = (_silu(a) * b).astype(BF16)
        acc = acc + jnp.dot(act, wout_ref[lo:lo + FF_CHUNK, :].astype(BF16), preferred_element_type=F32)
    y = x + (MACARON * mod[2:3]) * acc
    if final:
        y = _rms(y, gfin_ref[...])
    o_ref[...] = y


def _ffn(parts, mod3, layer, sub, row_fn, g, w_in, w_out, g_final=None):
    nblks = [p.shape[0] // TM for p in parts]
    final = g_final is not None
    if len(parts) == 1:
        in_specs = [pl.BlockSpec((TM, D_MODEL), lambda i: (i, 0))]
    else:
        nb0 = nblks[0]
        in_specs = [pl.BlockSpec((TM, D_MODEL), lambda i: (jnp.minimum(i, nb0 - 1), 0)),
                    pl.BlockSpec((TM, D_MODEL), lambda i: (jnp.maximum(i - nb0, 0), 0))]
    layer_spec = lambda shape: pl.BlockSpec((None,) + shape, lambda i: (layer, 0, 0), pipeline_mode=pl.Buffered(1))
    in_specs += [_mod_spec(layer, sub, row_fn), layer_spec((1, D_MODEL)), layer_spec((D_MODEL, 2 * D_FF)),
                 layer_spec((D_FF, D_MODEL))]
    args = list(parts) + [mod3, g.reshape(DEPTH, 1, D_MODEL), w_in, w_out]
    if final:
        in_specs.append(_const_spec((1, D_MODEL)))
        args.append(g_final.reshape(1, D_MODEL))
    n = sum(nblks) * TM
    return pl.pallas_call(
        functools.partial(_ffn_kernel, nparts=len(parts), nblk0=nblks[0], final=final),
        grid=(n // TM,),
        in_specs=in_specs,
        out_specs=pl.BlockSpec((TM, D_MODEL), lambda i: (i, 0)),
        out_shape=jax.ShapeDtypeStruct((n, D_MODEL), F32),
        compiler_params=_params(1),
        name="ffn_final" if final else "ffn",
    )(*args)


MLA_W1_COLS = Q_LORA + KV_LORA + 2 * CONV_CH + LANES
MLA_QK = MLA_HEADS * HEAD_PAD
KR_LANE = QK_NOPE


def _mla_kv(ckvn, krb, wkv_ref, kt_ref, v_ref, nseq):
    kv = jnp.dot(ckvn.astype(BF16), wkv_ref[...], preferred_element_type=F32)
    k = kv[:, :MLA_QK] + _tile_lanes(krb, MLA_HEADS)
    v_ref[...] = _with_ones_lane(kv[:, MLA_QK:]).astype(BF16)
    rows = k.shape[0] // nseq
    for s in range(nseq):
        kt_ref[s] = k[s * rows:(s + 1) * rows, :].T.astype(BF16)


def _mla_in_kernel(x_ref, mod_ref, g_ref, w1_ref, gq_ref, wqu_ref, gkv_ref, wkv_ref, *rest, rope, nseq):
    if rope:
        cos_ref, sa_ref, sb_ref, q_ref, kt_ref, v_ref, glu_ref = rest
    else:
        q_ref, kt_ref, v_ref, glu_ref, ckv_ref, kr_ref = rest
    h = _mod_norm(x_ref[...], g_ref[...], mod_ref[...]).astype(BF16)
    proj = jnp.dot(h, w1_ref[...], preferred_element_type=F32)
    o1 = Q_LORA
    o2 = o1 + KV_LORA
    o3 = o2 + CONV_CH
    o4 = o3 + CONV_CH
    cq, ckv, ua, ub, krb = proj[:, :o1], proj[:, o1:o2], proj[:, o2:o3], proj[:, o3:o4], proj[:, o4:]
    glu_ref[...] = ua * jax.nn.sigmoid(ub)
    q = jnp.dot(_rms(cq, gq_ref[...]).astype(BF16), wqu_ref[...], preferred_element_type=F32)
    ckvn = _rms(ckv, gkv_ref[...])
    if rope:
        cos, sa, sb = cos_ref[...], sa_ref[...], sb_ref[...]
        quarter = QK_ROPE // 4
        krb = _rope(krb, cos, sa, sb, quarter)
        q = _rope(q, _tile_lanes(cos, MLA_HEADS), _tile_lanes(sa, MLA_HEADS), _tile_lanes(sb, MLA_HEADS), quarter)
    else:
        ckv_ref[...] = ckvn
        kr_ref[...] = krb[:, KR_LANE:KR_LANE + QK_ROPE]
    q_ref[...] = (q * ((QK_NOPE + QK_ROPE) ** -0.5 * LOG2E)).astype(BF16)
    _mla_kv(ckvn, krb, wkv_ref, kt_ref, v_ref, nseq)


def _mla_ctx_kernel(ckv_ref, krb_ref, wkv_ref, kt_ref, v_ref):
    _mla_kv(ckv_ref[...], krb_ref[...], wkv_ref, kt_ref, v_ref, 1)


def _mla_in(x, x_off, n, mod3, layer, row_fn, seq, p, tables):
    rope = tables is not None
    nseq = max(TM // seq, 1)
    rows = TM // nseq
    batch = n // seq
    tiles_per_seq = max(seq // TM, 1)
    row = lambda w: pl.BlockSpec((TM, w), lambda i: (i, 0))
    in_specs = [pl.BlockSpec((TM, D_MODEL), lambda i: (i + x_off, 0)), _mod_spec(layer, 1, row_fn),
                _const_spec((1, D_MODEL)),
                _const_spec((D_MODEL, MLA_W1_COLS)), _const_spec((1, Q_LORA)), _const_spec((Q_LORA, MLA_QK)),
                _const_spec((1, KV_LORA)), _const_spec((KV_LORA, 2 * MLA_QK))]
    args = [x, mod3, p["g_mix"], p["w1"], p["g_q"], p["w_qu"], p["g_kv"], p["w_kv"]]
    kt_spec = pl.BlockSpec((nseq, MLA_QK, rows), lambda i: (i // tiles_per_seq, 0, i % tiles_per_seq))
    out_specs = [row(MLA_QK), kt_spec, row(MLA_QK), row(CONV_CH)]
    out_shape = [jax.ShapeDtypeStruct((n, MLA_QK), BF16), jax.ShapeDtypeStruct((batch, MLA_QK, seq), BF16),
                 jax.ShapeDtypeStruct((n, MLA_QK), BF16), jax.ShapeDtypeStruct((n, CONV_CH), F32)]
    if rope:
        tab = pl.BlockSpec((TM, LANES), lambda i: (i % tiles_per_seq, 0))
        in_specs += [tab, tab, tab]
        args += list(tables)
    else:
        out_specs += [row(KV_LORA), row(QK_ROPE)]
        out_shape += [jax.ShapeDtypeStruct((n, KV_LORA), F32), jax.ShapeDtypeStruct((n, QK_ROPE), F32)]
    return pl.pallas_call(
        functools.partial(_mla_in_kernel, rope=rope, nseq=nseq),
        grid=(n // TM,),
        in_specs=in_specs,
        out_specs=out_specs,
        out_shape=out_shape,
        compiler_params=_params(1),
        name="mla_in_rope" if rope else "mla_in",
    )(*args)


def _mla_ctx(ckv, krb, w_kv, seq):
    n = ckv.shape[0]
    return pl.pallas_call(
        _mla_ctx_kernel,
        grid=(n // seq,),
        in_specs=[pl.BlockSpec((seq, KV_LORA), lambda i: (i, 0)), pl.BlockSpec((seq, LANES), lambda i: (i, 0)),
                  _const_spec((KV_LORA, 2 * MLA_QK))],
        out_specs=[pl.BlockSpec((1, MLA_QK, seq), lambda i: (i, 0, 0)), pl.BlockSpec((seq, MLA_QK), lambda i: (i, 0))],
        out_shape=[jax.ShapeDtypeStruct((n // seq, MLA_QK, seq), BF16), jax.ShapeDtypeStruct((n, MLA_QK), BF16)],
        compiler_params=_params(1),
        name="mla_ctx",
    )(ckv, krb, w_kv)


GQA_Q = GQA_HEADS * GQA_HEAD_DIM
GQA_KV = GQA_KV_HEADS * GQA_HEAD_DIM
GQA_VPAD = GQA_KV_HEADS * HEAD_PAD
GQA_W_COLS = GQA_Q + 2 * GQA_KV + GQA_VPAD


def _head_rms(x, g):
    tm, w = x.shape
    lo = lax.broadcasted_iota(jnp.int32, (tm, LANES), 1) < GQA_HEAD_DIM
    outs = []
    for b in range(w // LANES):
        xb = x[:, b * LANES:(b + 1) * LANES]
        sq = xb * xb
        s_lo = jnp.sum(jnp.where(lo, sq, 0.0), axis=-1, keepdims=True)
        s_hi = jnp.sum(jnp.where(lo, 0.0, sq), axis=-1, keepdims=True)
        ms = jnp.where(lo, s_lo, s_hi) * (1.0 / GQA_HEAD_DIM)
        outs.append(xb * lax.rsqrt(ms + EPS))
    return jnp.concatenate(outs, axis=1) * g


def _gqa_in_kernel(x_ref, mod_ref, g_ref, w_ref, gq_ref, gk_ref, *rest, rope, nseq):
    if rope:
        cos_ref, sa_ref, sb_ref, q_ref, kt_ref, v_ref = rest
    else:
        q_ref, kt_ref, v_ref, kc_ref, vc_ref = rest
    h = _mod_norm(x_ref[...], g_ref[...], mod_ref[...]).astype(BF16)
    proj = jnp.dot(h, w_ref[...], preferred_element_type=F32)
    q = _head_rms(proj[:, :GQA_Q], gq_ref[...])
    k = _head_rms(proj[:, GQA_Q:GQA_Q + GQA_KV], gk_ref[...])
    if rope:
        cos, sa, sb = cos_ref[...], sa_ref[...], sb_ref[...]
        quarter = GQA_HEAD_DIM // 4
        q = _rope(q, _tile_lanes(cos, GQA_Q // LANES), _tile_lanes(sa, GQA_Q // LANES),
                  _tile_lanes(sb, GQA_Q // LANES), quarter)
        k = _rope(k, _tile_lanes(cos, GQA_KV // LANES), _tile_lanes(sa, GQA_KV // LANES),
                  _tile_lanes(sb, GQA_KV // LANES), quarter)
    else:
        kc_ref[...] = k
        vc_ref[...] = proj[:, GQA_Q + GQA_KV:GQA_Q + 2 * GQA_KV]
    q_ref[...] = (q * (GQA_HEAD_DIM ** -0.5 * LOG2E)).astype(BF16)
    v_ref[...] = _with_ones_lane(proj[:, GQA_Q + 2 * GQA_KV:]).astype(BF16)
    rows = k.shape[0] // nseq
    for s in range(nseq):
        kt_ref[s] = k[s * rows:(s + 1) * rows, :].T.astype(BF16)


def _gqa_ctx_kernel(k_ref, v_ref, kt_ref, vo_ref):
    kt_ref[0] = k_ref[...].T.astype(BF16)
    vo_ref[...] = _with_ones_lane(v_ref[...]).astype(BF16)


def _gqa_in(x, x_off, n, mod3, layer, row_fn, seq, p, tables):
    rope = tables is not None
    nseq = max(TM // seq, 1)
    rows = TM // nseq
    batch = n // seq
    tiles_per_seq = max(seq // TM, 1)
    row = lambda w: pl.BlockSpec((TM, w), lambda i: (i, 0))
    in_specs = [pl.BlockSpec((TM, D_MODEL), lambda i: (i + x_off, 0)), _mod_spec(layer, 1, row_fn),
                _const_spec((1, D_MODEL)),
                _const_spec((D_MODEL, GQA_W_COLS)), _const_spec((1, GQA_Q)), _const_spec((1, GQA_KV))]
    args = [x, mod3, p["g_mix"], p["w"], p["g_q"], p["g_k"]]
    kt_spec = pl.BlockSpec((nseq, GQA_KV, rows), lambda i: (i // tiles_per_seq, 0, i % tiles_per_seq))
    out_specs = [row(GQA_Q), kt_spec, row(GQA_VPAD)]
    out_shape = [jax.ShapeDtypeStruct((n, GQA_Q), BF16), jax.ShapeDtypeStruct((batch, GQA_KV, seq), BF16),
                 jax.ShapeDtypeStruct((n, GQA_VPAD), BF16)]
    if rope:
        tab = pl.BlockSpec((TM, LANES), lambda i: (i % tiles_per_seq, 0))
        in_specs += [tab, tab, tab]
        args += list(tables)
    else:
        out_specs += [row(GQA_KV), row(GQA_KV)]
        out_shape += [jax.ShapeDtypeStruct((n, GQA_KV), F32), jax.ShapeDtypeStruct((n, GQA_KV), F32)]
    return pl.pallas_call(
        functools.partial(_gqa_in_kernel, rope=rope, nseq=nseq),
        grid=(n // TM,),
        in_specs=in_specs,
        out_specs=out_specs,
        out_shape=out_shape,
        compiler_params=_params(1),
        name="gqa_in_rope" if rope else "gqa_in",
    )(*args)


def _gqa_ctx(k, vpad, seq):
    n = k.shape[0]
    return pl.pallas_call(
        _gqa_ctx_kernel,
        grid=(n // seq,),
        in_specs=[pl.BlockSpec((seq, GQA_KV), lambda i: (i, 0)), pl.BlockSpec((seq, GQA_VPAD), lambda i: (i, 0))],
        out_specs=[pl.BlockSpec((1, GQA_KV, seq), lambda i: (i, 0, 0)), pl.BlockSpec((seq, GQA_VPAD), lambda i: (i, 0))],
        out_shape=[jax.ShapeDtypeStruct((n // seq, GQA_KV, seq), BF16), jax.ShapeDtypeStruct((n, GQA_VPAD), BF16)],
        compiler_params=_params(1),
        name="gqa_ctx",
    )(k, vpad)


def _attn_kernel(*refs, nseg, bt, seq, nq, nkv, dq, dk, dv):
    q_ref = refs[0]
    segs = [(refs[1 + 2 * s], refs[2 + 2 * s]) for s in range(nseg)]
    o_ref = refs[1 + 2 * nseg]

    def sub_tile(bi, r0):
        rows = pl.ds(bi * seq + r0, Q_SUB)
        outs = []
        for j in range(nq):
            kv = j * nkv // nq
            qs = q_ref[rows, j * dq:(j + 1) * dq]
            scores = [jnp.dot(qs, kt_ref[bi, kv * dk:(kv + 1) * dk, :], preferred_element_type=F32)
                      for kt_ref, _ in segs]
            m = functools.reduce(jnp.maximum, [jnp.max(s, axis=-1, keepdims=True) for s in scores])
            probs = [jnp.exp2(s - m) for s in scores]
            o = None
            for p, (_, v_ref) in zip(probs, segs):
                nk = v_ref.shape[0] // bt
                pv = jnp.dot(p.astype(BF16), v_ref[bi * nk:(bi + 1) * nk, kv * HEAD_PAD:(kv + 1) * HEAD_PAD],
                             preferred_element_type=F32)
                o = pv if o is None else o + pv
            outs.append(o[:, :dv] / o[:, DENOM_LANE:DENOM_LANE + 1])
        o_ref[rows, :] = jnp.concatenate(outs, axis=1).astype(BF16)

    for bi in range(bt):
        if seq == Q_SUB:
            sub_tile(bi, 0)
        else:
            def body(r, carry, bi=bi):
                sub_tile(bi, pl.multiple_of(r * Q_SUB, Q_SUB))
                return carry
            lax.fori_loop(0, seq // Q_SUB, body, 0)


def _attention(q, segs, *, seq, heads, kv_heads, dq, dk, dv, bt, heads_per_step):
    n = q.shape[0]
    batch = n // seq
    nq = heads_per_step
    steps = heads // nq
    nkv = max(kv_heads // steps, 1)
    q_per_kv_block = steps // (kv_heads // nkv)
    in_specs = [pl.BlockSpec((bt * seq, nq * dq), lambda b, p: (b, p))]
    args = [q]
    for kt, v in segs:
        nk = kt.shape[2]
        in_specs.append(pl.BlockSpec((bt, nkv * dk, nk), lambda b, p: (b, p // q_per_kv_block, 0)))
        in_specs.append(pl.BlockSpec((bt * nk, nkv * HEAD_PAD), lambda b, p: (b, p // q_per_kv_block)))
        args += [kt, v]
    return pl.pallas_call(
        functools.partial(_attn_kernel, nseg=len(segs), bt=bt, seq=seq, nq=nq, nkv=nkv, dq=dq, dk=dk, dv=dv),
        grid=(batch // bt, steps),
        in_specs=in_specs,
        out_specs=pl.BlockSpec((bt * seq, nq * dv), lambda b, p: (b, p)),
        out_shape=jax.ShapeDtypeStruct((n, heads * dv), BF16),
        compiler_params=_params(2),
        name="attention",
    )(*args)


def _mla_out_kernel(x_ref, mod_ref, attn_ref, glu_ref, *rest, nseq, halo, tiles_per_seq):
    if halo:
        prev_ref, next_ref, wdw_ref, bdw_ref, gln_ref, bln_ref, wout_ref, y_ref, pad_ref, cat_ref = rest
    else:
        wdw_ref, bdw_ref, gln_ref, bln_ref, wout_ref, y_ref, pad_ref, cat_ref = rest
    rows = TM // nseq
    zeros = jnp.zeros((HALO, CONV_CH), F32)
    for s in range(nseq):
        if halo:
            j = pl.program_id(0) % tiles_per_seq
            pad_ref[s, 0:HALO, :] = jnp.where(j == 0, zeros, prev_ref[...])
            pad_ref[s, HALO + rows:2 * HALO + rows, :] = jnp.where(j == tiles_per_seq - 1, zeros, next_ref[...])
        else:
            pad_ref[s, 0:HALO, :] = zeros
            pad_ref[s, HALO + rows:2 * HALO + rows, :] = zeros
        pad_ref[s, HALO:HALO + rows, :] = glu_ref[s * rows:(s + 1) * rows, :]
    cat_ref[:, :CONV_CH] = attn_ref[...]
    sub = 8
    lead = HALO - CONV_PAD
    win_rows = CONV_ROWS + 2 * HALO

    def conv_chunk(s, r0):
        blocks = []
        for lb in range(CONV_CH // LANES):
            lanes = slice(lb * LANES, (lb + 1) * LANES)
            win = pad_ref[s, pl.ds(r0, win_rows), lanes]
            acc = jnp.broadcast_to(bdw_ref[:, lanes], (CONV_ROWS, LANES))
            for phase in range(sub):
                shifted = win if phase == 0 else pltpu.roll(win, win_rows - phase, 0)
                for a in range((CONV_W + lead) // sub + 1):
                    k = a * sub + phase - lead
                    if 0 <= k < CONV_W:
                        acc = acc + shifted[a * sub:a * sub + CONV_ROWS] * wdw_ref[k:k + 1, lanes]
            blocks.append(acc)
        acc = jnp.concatenate(blocks, axis=1)
        mu = jnp.mean(acc, axis=-1, keepdims=True)
        d = acc - mu
        var = jnp.mean(d * d, axis=-1, keepdims=True)
        g = _silu(d * lax.rsqrt(var + EPS) * gln_ref[...] + bln_ref[...])
        cat_ref[pl.ds(s * rows + r0, CONV_ROWS), CONV_CH:] = g.astype(BF16)

    for s in range(nseq):
        def body(c, carry, s=s):
            conv_chunk(s, pl.multiple_of(c * CONV_ROWS, CONV_ROWS))
            return carry
        lax.fori_loop(0, rows // CONV_ROWS, body, 0)
    out = jnp.dot(cat_ref[...], wout_ref[...], preferred_element_type=F32)
    y_ref[...] = x_ref[...] + mod_ref[2:3, :] * out


def _mla_out(x, x_off, mod3, layer, row_fn, seq, attn, glu, p):
    n = attn.shape[0]
    nseq = max(TM // seq, 1)
    tiles_per_seq = max(seq // TM, 1)
    halo = tiles_per_seq > 1
    row = lambda w: pl.BlockSpec((TM, w), lambda i: (i, 0))
    in_specs = [pl.BlockSpec((TM, D_MODEL), lambda i: (i + x_off, 0)), _mod_spec(layer, 1, row_fn),
                row(CONV_CH), row(CONV_CH)]
    args = [x, mod3, attn, glu]
    if halo:
        per = TM // HALO
        last = n // HALO - 1
        in_specs += [pl.BlockSpec((HALO, CONV_CH), lambda i: (jnp.maximum(i * per - 1, 0), 0)),
                     pl.BlockSpec((HALO, CONV_CH), lambda i: (jnp.minimum((i + 1) * per, last), 0))]
        args += [glu, glu]
    in_specs += [_const_spec((CONV_W + 1, CONV_CH)), _const_spec((1, CONV_CH)), _const_spec((1, CONV_CH)),
                 _const_spec((1, CONV_CH)), _const_spec((2 * CONV_CH, D_MODEL))]
    args += [p["w_dw"], p["b_dw"], p["g_ln"], p["b_ln"], p["w_out"]]
    return pl.pallas_call(
        functools.partial(_mla_out_kernel, nseq=nseq, halo=halo, tiles_per_seq=tiles_per_seq),
        grid=(n // TM,),
        in_specs=in_specs,
        out_specs=row(D_MODEL),
        out_shape=jax.ShapeDtypeStruct((n, D_MODEL), F32),
        scratch_shapes=[pltpu.VMEM((nseq, TM // nseq + 2 * HALO, CONV_CH), F32),
                        pltpu.VMEM((TM, 2 * CONV_CH), BF16)],
        compiler_params=_params(1),
        name="mla_out",
    )(*args)


def _gqa_out_kernel(x_ref, mod_ref, attn_ref, wout_ref, y_ref):
    out = jnp.dot(attn_ref[...], wout_ref[...], preferred_element_type=F32)
    y_ref[...] = x_ref[...] + mod_ref[2:3, :] * out


def _gqa_out(x, x_off, mod3, layer, row_fn, attn, w_out):
    n = attn.shape[0]
    row = lambda w: pl.BlockSpec((TM, w), lambda i: (i, 0))
    return pl.pallas_call(
        _gqa_out_kernel,
        grid=(n // TM,),
        in_specs=[pl.BlockSpec((TM, D_MODEL), lambda i: (i + x_off, 0)), _mod_spec(layer, 1, row_fn), row(GQA_Q),
                  _const_spec((GQA_Q, D_MODEL))],
        out_specs=row(D_MODEL),
        out_shape=jax.ShapeDtypeStruct((n, D_MODEL), F32),
        compiler_params=_params(1),
        name="gqa_out",
    )(x, mod3, attn, w_out)


def _rope_tables(d, offset, period):
    half = d // 2
    quarter = half // 2
    rows = DEC_SEQ // GRID_W
    pos_row = jnp.repeat(jnp.arange(rows), GRID_W)
    pos_col = jnp.tile(jnp.arange(GRID_W), rows)
    inv = ROPE_BASE ** (-jnp.arange(0, half, 2, dtype=F32) / half)
    ang_r = pos_row.astype(F32)[:, None] * inv[None, :]
    ang_c = pos_col.astype(F32)[:, None] * inv[None, :]
    zero = jnp.zeros((DEC_SEQ, quarter), F32)
    cos = jnp.concatenate([jnp.cos(ang_r), jnp.cos(ang_r), jnp.cos(ang_c), jnp.cos(ang_c)], axis=1)
    sa = jnp.concatenate([-jnp.sin(ang_r), zero, -jnp.sin(ang_c), zero], axis=1)
    sb = jnp.concatenate([zero, jnp.sin(ang_r), zero, jnp.sin(ang_c)], axis=1)

    def embed(t, fill):
        blk = jnp.concatenate([jnp.full((DEC_SEQ, offset), fill, F32), t,
                               jnp.full((DEC_SEQ, period - offset - d), fill, F32)], axis=1)
        return jnp.tile(blk, (1, LANES // period))

    return embed(cos, 1.0), embed(sa, 0.0), embed(sb, 0.0)


def _pad_heads(w, heads, dim):
    k = w.shape[0]
    return jnp.pad(w.reshape(k, heads, dim), ((0, 0), (0, 0), (0, HEAD_PAD - dim))).reshape(k, heads * HEAD_PAD)


def _prep_mla(i, g_mix_l, w_in_a, g_q_lora, w_q_up, g_kv_lora, w_kv_up, w_dw, b_dw, g_conv_ln, b_conv_ln, w_out_a):
    w = w_in_a[i]
    o1 = Q_LORA
    o2 = o1 + KV_LORA
    o3 = o2 + QK_ROPE
    zeros = lambda c: jnp.zeros((D_MODEL, c), F32)
    w1 = jnp.concatenate([w[:, :o2], w[:, o3:], zeros(KR_LANE), w[:, o2:o3],
                          zeros(LANES - KR_LANE - QK_ROPE)], axis=1)
    kvu = w_kv_up[i].reshape(KV_LORA, MLA_HEADS, QK_NOPE + V_HEAD)
    w_kn = _pad_heads(kvu[:, :, :QK_NOPE].reshape(KV_LORA, MLA_HEADS * QK_NOPE), MLA_HEADS, QK_NOPE)
    w_v = _pad_heads(kvu[:, :, QK_NOPE:].reshape(KV_LORA, MLA_HEADS * V_HEAD), MLA_HEADS, V_HEAD)
    return dict(
        g_mix=g_mix_l.reshape(1, D_MODEL),
        w1=w1.astype(BF16),
        g_q=g_q_lora[i].reshape(1, Q_LORA),
        w_qu=_pad_heads(w_q_up[i], MLA_HEADS, QK_NOPE + QK_ROPE).astype(BF16),
        g_kv=g_kv_lora[i].reshape(1, KV_LORA),
        w_kv=jnp.concatenate([w_kn, w_v], axis=1).astype(BF16),
        w_dw=jnp.pad(w_dw[i], ((0, 1), (0, 0))),
        b_dw=b_dw[i].reshape(1, CONV_CH),
        g_ln=g_conv_ln[i].reshape(1, CONV_CH),
        b_ln=b_conv_ln[i].reshape(1, CONV_CH),
        w_out=w_out_a[i].astype(BF16),
    )


def _prep_gqa(i, g_mix_l, w_in_c, g_q_head, g_k_head, w_out_c):
    w = w_in_c[i]
    v_pad = _pad_heads(w[:, GQA_Q + GQA_KV:], GQA_KV_HEADS, GQA_HEAD_DIM)
    return dict(
        g_mix=g_mix_l.reshape(1, D_MODEL),
        w=jnp.concatenate([w, v_pad], axis=1).astype(BF16),
        g_q=jnp.tile(g_q_head[i], GQA_HEADS).reshape(1, GQA_Q),
        g_k=jnp.tile(g_k_head[i], GQA_KV_HEADS).reshape(1, GQA_KV),
        w_out=w_out_c[i].astype(BF16),
    )


SAMPLE_HEADS_PER_STEP = 4
PROMPT_SEQS_PER_STEP = 2


def _mixer(x_all, x_off, n, mod3, l, row_fn, seq, p, ctx, tables):
    sample = ctx is not None
    if l % 2 == 0:
        outs = _mla_in(x_all, x_off, n, mod3, l, row_fn, seq, p, tables["mla"] if sample else None)
        q, kt, v, glu = outs[:4]
        segs = [(kt, v)]
        if sample:
            segs = [_mla_ctx(ctx["mla_ckv"], ctx["mla_krb"], p["w_kv"], ctx["past"])] + segs
            hps, bt = SAMPLE_HEADS_PER_STEP, 1
        else:
            hps, bt = MLA_HEADS, PROMPT_SEQS_PER_STEP
        attn = _attention(q, segs, seq=seq, heads=MLA_HEADS, kv_heads=MLA_HEADS, dq=HEAD_PAD, dk=HEAD_PAD,
                          dv=V_HEAD, bt=bt, heads_per_step=hps)
        return _mla_out(x_all, x_off, mod3, l, row_fn, seq, attn, glu, p), outs[4:]
    outs = _gqa_in(x_all, x_off, n, mod3, l, row_fn, seq, p, tables["gqa"] if sample else None)
    q, kt, v = outs[:3]
    segs = [(kt, v)]
    if sample:
        segs = [_gqa_ctx(ctx["gqa_k"], ctx["gqa_vpad"], ctx["past"])] + segs
        hps, bt = SAMPLE_HEADS_PER_STEP, 1
    else:
        hps, bt = GQA_HEADS, PROMPT_SEQS_PER_STEP
    attn = _attention(q, segs, seq=seq, heads=GQA_HEADS, kv_heads=GQA_KV_HEADS, dq=GQA_HEAD_DIM,
                      dk=GQA_HEAD_DIM, dv=GQA_HEAD_DIM, bt=bt, heads_per_step=hps)
    return _gqa_out(x_all, x_off, mod3, l, row_fn, attn, p["w_out"]), outs[3:]


def kernel(x_prompt, x_sample, cache_mla_ckv, cache_mla_krope, cache_gqa_k, cache_gqa_v, c, c_ctx, g_ff1, w_ff1_in, w_ff1_out, g_mix, g_ff2, w_ff2_in, w_ff2_out, w_mod, b_mod, w_in_a, g_q_lora, w_q_up, g_kv_lora, w_kv_up, w_dw, b_dw, g_conv_ln, b_conv_ln, w_out_a, w_in_c, g_q_head, g_k_head, w_out_c, g_final):
    batch, seq, _ = x_prompt.shape
    dec_batch, dec_seq, _ = x_sample.shape
    past = cache_mla_ckv.shape[2]
    assert DEPTH == 2 and dec_seq == DEC_SEQ and 1 + dec_batch <= MOD_ROWS

    cond = jnp.concatenate([c_ctx[None, :], c, jnp.zeros((MOD_ROWS - 1 - dec_batch, D_MODEL), F32)], axis=0)
    mod3 = _modulation(cond, w_mod, b_mod)

    layers = [
        _prep_mla(0, g_mix[0], w_in_a, g_q_lora, w_q_up, g_kv_lora, w_kv_up, w_dw, b_dw, g_conv_ln, b_conv_ln,
                  w_out_a),
        _prep_gqa(0, g_mix[1], w_in_c, g_q_head, g_k_head, w_out_c),
    ]
    tables = dict(mla=_rope_tables(QK_ROPE, KR_LANE, HEAD_PAD), gqa=_rope_tables(GQA_HEAD_DIM, 0, GQA_HEAD_DIM))

    ctx = dict(
        past=past,
        mla_ckv=cache_mla_ckv[:, 0].reshape(dec_batch * past, KV_LORA),
        mla_krb=jnp.pad(cache_mla_krope[:, 0].reshape(dec_batch * past, QK_ROPE),
                        ((0, 0), (KR_LANE, LANES - KR_LANE - QK_ROPE))),
        gqa_k=cache_gqa_k[:, 0].reshape(dec_batch * past, GQA_KV),
        gqa_vpad=jnp.pad(cache_gqa_v[:, 0], ((0, 0), (0, 0), (0, 0), (0, HEAD_PAD - GQA_HEAD_DIM))
                         ).reshape(dec_batch * past, GQA_VPAD),
    )

    n_p, n_s = batch * seq, dec_batch * dec_seq
    nb_p = n_p // TM
    row_p = lambda i: 0
    row_s = lambda i: 1 + (i * TM) // DEC_SEQ
    row_all = lambda i: jnp.where(i < nb_p, 0, 1 + ((i - nb_p) * TM) // DEC_SEQ)

    parts = [x_prompt.reshape(n_p, D_MODEL), x_sample.reshape(n_s, D_MODEL)]
    saved = []
    for l in range(DEPTH):
        x_all = _ffn(parts, mod3, l, 0, row_all, g_ff1, w_ff1_in, w_ff1_out)
        xp, st = _mixer(x_all, 0, n_p, mod3, l, row_p, seq, layers[l], None, tables)
        xs, _ = _mixer(x_all, nb_p, n_s, mod3, l, row_s, dec_seq, layers[l], ctx, tables)
        saved.append(st)
        if l < DEPTH - 1:
            parts = [_ffn([xp, xs], mod3, l, 2, row_all, g_ff2, w_ff2_in, w_ff2_out)]
        else:
            y_prompt = _ffn([xp], mod3, l, 2, row_p, g_ff2, w_ff2_in, w_ff2_out, g_final=g_final)
            y_sample = _ffn([xs], mod3, l, 2, row_s, g_ff2, w_ff2_in, w_ff2_out, g_final=g_final)
    (ckv_new, kr_new), (k_new, v_new) = saved

    return (y_prompt.reshape(batch, seq, D_MODEL),
            y_sample.reshape(dec_batch, dec_seq, D_MODEL),
            ckv_new.reshape(batch, 1, seq, KV_LORA),
            kr_new.reshape(batch, 1, seq, QK_ROPE),
            k_new.reshape(batch, 1, seq, GQA_KV_HEADS, GQA_HEAD_DIM),
            v_new.reshape(batch, 1, seq, GQA_KV_HEADS, GQA_HEAD_DIM))
```

```python
import functools

import jax
import jax.numpy as jnp
from jax import lax
from jax.experimental import pallas as pl
from jax.experimental.pallas import tpu as pltpu

F32 = jnp.float32
BF16 = jnp.bfloat16

D_MODEL = 1024
DEPTH = 2
DEC_SEQ = 2048
GRID_W = 64
MLA_HEADS = 8
Q_LORA = 384
KV_LORA = 256
QK_NOPE = 64
QK_ROPE = 32
V_HEAD = 64
CONV_CH = 512
CONV_W = 31
CONV_PAD = CONV_W // 2
GQA_HEADS = 16
GQA_KV_HEADS = 4
GQA_HEAD_DIM = 64
D_FF = 2816
MACARON = 0.5
N_MOD = 9
ROPE_BASE = 10000.0
EPS = 1e-6

LANES = 128
HEAD_PAD = 128
MOD_ROWS = 8
DENOM_LANE = 64
LOG2E = 1.4426950408889634
HALO = 16
TM = 512
Q_SUB = 256
FF_CHUNK = 256
CONV_ROWS = 64
VMEM_LIMIT = 56 * 1024 * 1024


def _params(n_axes):
    return pltpu.CompilerParams(dimension_semantics=("parallel",) * n_axes, vmem_limit_bytes=VMEM_LIMIT)


def _const_spec(shape):
    return pl.BlockSpec(shape, lambda *_: (0,) * len(shape), pipeline_mode=pl.Buffered(1))


def _rms(x, g):
    return x * lax.rsqrt(jnp.mean(x * x, axis=-1, keepdims=True) + EPS) * g


def _mod_norm(x, g, mod):
    return _rms(x, g) * (1.0 + mod[1:2]) + mod[0:1]


def _silu(x):
    return x * jax.nn.sigmoid(x)


def _rope(x, cos, sa, sb, quarter):
    w = x.shape[-1]
    return x * cos + pltpu.roll(x, w - quarter, 1) * sa + pltpu.roll(x, quarter, 1) * sb


def _with_ones_lane(v):
    lane = lax.broadcasted_iota(jnp.int32, v.shape, 1)
    return jnp.where(lane % HEAD_PAD == DENOM_LANE, 1.0, v)


def _tile_lanes(t, reps):
    return t if reps == 1 else jnp.concatenate([t] * reps, axis=1)


def _mod_kernel(c_ref, w_ref, b_ref, o_ref):
    e = _silu(c_ref[...]).astype(BF16)
    o_ref[...] = jnp.dot(e, w_ref[...].astype(BF16), preferred_element_type=F32) + b_ref[...]


def _modulation(cond, w_mod, b_mod):
    n_out = N_MOD * D_MODEL
    tn = 1536
    out = pl.pallas_call(
        _mod_kernel,
        grid=(DEPTH, n_out // tn),
        in_specs=[
            pl.BlockSpec((MOD_ROWS, D_MODEL), lambda l, j: (0, 0)),
            pl.BlockSpec((None, D_MODEL, tn), lambda l, j: (l, 0, j)),
            pl.BlockSpec((None, 1, tn), lambda l, j: (l, 0, j)),
        ],
        out_specs=pl.BlockSpec((None, MOD_ROWS, tn), lambda l, j: (l, 0, j)),
        out_shape=jax.ShapeDtypeStruct((DEPTH, MOD_ROWS, n_out), F32),
        compiler_params=_params(2),
        name="modulation",
    )(cond, w_mod, b_mod.reshape(DEPTH, 1, n_out))
    return out.reshape(DEPTH * MOD_ROWS * 3, 3, D_MODEL)


def _mod_spec(layer, sub, row_fn):
    return pl.BlockSpec((None, 3, D_MODEL), lambda i, *_: ((layer * MOD_ROWS + row_fn(i)) * 3 + sub, 0, 0))


def _ffn_kernel(*refs, nparts, nblk0, final):
    x_refs = refs[:nparts]
    mod_ref, g_ref, win_ref, wout_ref = refs[nparts:nparts + 4]
    rest = refs[nparts + 4:]
    if final:
        gfin_ref, o_ref = rest
    else:
        (o_ref,) = rest
    x = x_refs[0][...]
    if nparts == 2:
        x = jnp.where(pl.program_id(0) < nblk0, x, x_refs[1][...])
    mod = mod_ref[...]
    h = _mod_norm(x, g_ref[...], mod).astype(BF16)
    acc = jnp.zeros(x.shape, F32)
    for j in range(D_FF // FF_CHUNK):
        lo = j * FF_CHUNK
        a = jnp.dot(h, win_ref[:, lo:lo + FF_CHUNK].astype(BF16), preferred_element_type=F32)
        b = jnp.dot(h, win_ref[:, D_FF + lo:D_FF + lo + FF_CHUNK].astype(BF16), preferred_element_type=F32)
        act = (_silu(a) * b).astype(BF16)
        acc = acc + jnp.dot(act, wout_ref[lo:lo + FF_CHUNK, :].astype(BF16), preferred_element_type=F32)
    y = x + (MACARON * mod[2:3]) * acc
    if final:
        y = _rms(y, gfin_ref[...])
    o_ref[...] = y


def _ffn(parts, mod3, layer, sub, row_fn, g, w_in, w_out, g_final=None):
    nblks = [p.shape[0] // TM for p in parts]
    final = g_final is not None
    if len(parts) == 1:
        in_specs = [pl.BlockSpec((TM, D_MODEL), lambda i: (i, 0))]
    else:
        nb0 = nblks[0]
        in_specs = [pl.BlockSpec((TM, D_MODEL), lambda i: (jnp.minimum(i, nb0 - 1), 0)),
                    pl.BlockSpec((TM, D_MODEL), lambda i: (jnp.maximum(i - nb0, 0), 0))]
    layer_spec = lambda shape: pl.BlockSpec((None,) + shape, lambda i: (layer, 0, 0), pipeline_mode=pl.Buffered(1))
    in_specs += [_mod_spec(layer, sub, row_fn), layer_spec((1, D_MODEL)), layer_spec((D_MODEL, 2 * D_FF)),
                 layer_spec((D_FF, D_MODEL))]
    args = list(parts) + [mod3, g.reshape(DEPTH, 1, D_MODEL), w_in, w_out]
    if final:
        in_specs.append(_const_spec((1, D_MODEL)))
        args.append(g_final.reshape(1, D_MODEL))
    n = sum(nblks) * TM
    return pl.pallas_call(
        functools.partial(_ffn_kernel, nparts=len(parts), nblk0=nblks[0], final=final),
        grid=(n // TM,),
        in_specs=in_specs,
        out_specs=pl.BlockSpec((TM, D_MODEL), lambda i: (i, 0)),
        out_shape=jax.ShapeDtypeStruct((n, D_MODEL), F32),
        compiler_params=_params(1),
        name="ffn_final" if final else "ffn",
    )(*args)


MLA_W1_COLS = Q_LORA + KV_LORA + 2 * CONV_CH + LANES
MLA_QK = MLA_HEADS * HEAD_PAD
KR_LANE = QK_NOPE


def _mla_kv(ckvn, krb, wkv_ref, kt_ref, v_ref, nseq):
    kv = jnp.dot(ckvn.astype(BF16), wkv_ref[...], preferred_element_type=F32)
    k = kv[:, :MLA_QK] + _tile_lanes(krb, MLA_HEADS)
    v_ref[...] = _with_ones_lane(kv[:, MLA_QK:]).astype(BF16)
    rows = k.shape[0] // nseq
    for s in range(nseq):
        kt_ref[s] = k[s * rows:(s + 1) * rows, :].T.astype(BF16)


def _mla_in_kernel(x_ref, mod_ref, g_ref, w1_ref, gq_ref, wqu_ref, gkv_ref, wkv_ref, *rest, rope, nseq):
    if rope:
        cos_ref, sa_ref, sb_ref, q_ref, kt_ref, v_ref, glu_ref = rest
    else:
        q_ref, kt_ref, v_ref, glu_ref, ckv_ref, kr_ref = rest
    h = _mod_norm(x_ref[...], g_ref[...], mod_ref[...]).astype(BF16)
    proj = jnp.dot(h, w1_ref[...], preferred_element_type=F32)
    o1 = Q_LORA
    o2 = o1 + KV_LORA
    o3 = o2 + CONV_CH
    o4 = o3 + CONV_CH
    cq, ckv, ua, ub, krb = proj[:, :o1], proj[:, o1:o2], proj[:, o2:o3], proj[:, o3:o4], proj[:, o4:]
    glu_ref[...] = ua * jax.nn.sigmoid(ub)
    q = jnp.dot(_rms(cq, gq_ref[...]).astype(BF16), wqu_ref[...], preferred_element_type=F32)
    ckvn = _rms(ckv, gkv_ref[...])
    if rope:
        cos, sa, sb = cos_ref[...], sa_ref[...], sb_ref[...]
        quarter = QK_ROPE // 4
        krb = _rope(krb, cos, sa, sb, quarter)
        q = _rope(q, _tile_lanes(cos, MLA_HEADS), _tile_lanes(sa, MLA_HEADS), _tile_lanes(sb, MLA_HEADS), quarter)
    else:
        ckv_ref[...] = ckvn
        kr_ref[...] = krb[:, KR_LANE:KR_LANE + QK_ROPE]
    q_ref[...] = (q * ((QK_NOPE + QK_ROPE) ** -0.5 * LOG2E)).astype(BF16)
    _mla_kv(ckvn, krb, wkv_ref, kt_ref, v_ref, nseq)


def _mla_ctx_kernel(ckv_ref, krb_ref, wkv_ref, kt_ref, v_ref):
    _mla_kv(ckv_ref[...], krb_ref[...], wkv_ref, kt_ref, v_ref, 1)


def _mla_in(x, x_off, n, mod3, layer, row_fn, seq, p, tables):
    rope = tables is not None
    nseq = max(TM // seq, 1)
    rows = TM // nseq
    batch = n // seq
    tiles_per_seq = max(seq // TM, 1)
    row = lambda w: pl.BlockSpec((TM, w), lambda i: (i, 0))
    in_specs = [pl.BlockSpec((TM, D_MODEL), lambda i: (i + x_off, 0)), _mod_spec(layer, 1, row_fn),
                _const_spec((1, D_MODEL)),
                _const_spec((D_MODEL, MLA_W1_COLS)), _const_spec((1, Q_LORA)), _const_spec((Q_LORA, MLA_QK)),
                _const_spec((1, KV_LORA)), _const_spec((KV_LORA, 2 * MLA_QK))]
    args = [x, mod3, p["g_mix"], p["w1"], p["g_q"], p["w_qu"], p["g_kv"], p["w_kv"]]
    kt_spec = pl.BlockSpec((nseq, MLA_QK, rows), lambda i: (i // tiles_per_seq, 0, i % tiles_per_seq))
    out_specs = [row(MLA_QK), kt_spec, row(MLA_QK), row(CONV_CH)]
    out_shape = [jax.ShapeDtypeStruct((n, MLA_QK), BF16), jax.ShapeDtypeStruct((batch, MLA_QK, seq), BF16),
                 jax.ShapeDtypeStruct((n, MLA_QK), BF16), jax.ShapeDtypeStruct((n, CONV_CH), F32)]
    if rope:
        tab = pl.BlockSpec((TM, LANES), lambda i: (i % tiles_per_seq, 0))
        in_specs += [tab, tab, tab]
        args += list(tables)
    else:
        out_specs += [row(KV_LORA), row(QK_ROPE)]
        out_shape += [jax.ShapeDtypeStruct((n, KV_LORA), F32), jax.ShapeDtypeStruct((n, QK_ROPE), F32)]
    return pl.pallas_call(
        functools.partial(_mla_in_kernel, rope=rope, nseq=nseq),
        grid=(n // TM,),
        in_specs=in_specs,
        out_specs=out_specs,
        out_shape=out_shape,
        compiler_params=_params(1),
        name="mla_in_rope" if rope else "mla_in",
    )(*args)


def _mla_ctx(ckv, krb, w_kv, seq):
    n = ckv.shape[0]
    return pl.pallas_call(
        _mla_ctx_kernel,
        grid=(n // seq,),
        in_specs=[pl.BlockSpec((seq, KV_LORA), lambda i: (i, 0)), pl.BlockSpec((seq, LANES), lambda i: (i, 0)),
                  _const_spec((KV_LORA, 2 * MLA_QK))],
        out_specs=[pl.BlockSpec((1, MLA_QK, seq), lambda i: (i, 0, 0)), pl.BlockSpec((seq, MLA_QK), lambda i: (i, 0))],
        out_shape=[jax.ShapeDtypeStruct((n // seq, MLA_QK, seq), BF16), jax.ShapeDtypeStruct((n, MLA_QK), BF16)],
        compiler_params=_params(1),
        name="mla_ctx",
    )(ckv, krb, w_kv)


GQA_Q = GQA_HEADS * GQA_HEAD_DIM
GQA_KV = GQA_KV_HEADS * GQA_HEAD_DIM
GQA_VPAD = GQA_KV_HEADS * HEAD_PAD
GQA_W_COLS = GQA_Q + 2 * GQA_KV + GQA_VPAD


def _head_rms(x, g):
    tm, w = x.shape
    lo = lax.broadcasted_iota(jnp.int32, (tm, LANES), 1) < GQA_HEAD_DIM
    outs = []
    for b in range(w // LANES):
        xb = x[:, b * LANES:(b + 1) * LANES]
        sq = xb * xb
        s_lo = jnp.sum(jnp.where(lo, sq, 0.0), axis=-1, keepdims=True)
        s_hi = jnp.sum(jnp.where(lo, 0.0, sq), axis=-1, keepdims=True)
        ms = jnp.where(lo, s_lo, s_hi) * (1.0 / GQA_HEAD_DIM)
        outs.append(xb * lax.rsqrt(ms + EPS))
    return jnp.concatenate(outs, axis=1) * g


def _gqa_in_kernel(x_ref, mod_ref, g_ref, w_ref, gq_ref, gk_ref, *rest, rope, nseq):
    if rope:
        cos_ref, sa_ref, sb_ref, q_ref, kt_ref, v_ref = rest
    else:
        q_ref, kt_ref, v_ref, kc_ref, vc_ref = rest
    h = _mod_norm(x_ref[...], g_ref[...], mod_ref[...]).astype(BF16)
    proj = jnp.dot(h, w_ref[...], preferred_element_type=F32)
    q = _head_rms(proj[:, :GQA_Q], gq_ref[...])
    k = _head_rms(proj[:, GQA_Q:GQA_Q + GQA_KV], gk_ref[...])
    if rope:
        cos, sa, sb = cos_ref[...], sa_ref[...], sb_ref[...]
        quarter = GQA_HEAD_DIM // 4
        q = _rope(q, _tile_lanes(cos, GQA_Q // LANES), _tile_lanes(sa, GQA_Q // LANES),
                  _tile_lanes(sb, GQA_Q // LANES), quarter)
        k = _rope(k, _tile_lanes(cos, GQA_KV // LANES), _tile_lanes(sa, GQA_KV // LANES),
                  _tile_lanes(sb, GQA_KV // LANES), quarter)
    else:
        kc_ref[...] = k
        vc_ref[...] = proj[:, GQA_Q + GQA_KV:GQA_Q + 2 * GQA_KV]
    q_ref[...] = (q * (GQA_HEAD_DIM ** -0.5 * LOG2E)).astype(BF16)
    v_ref[...] = _with_ones_lane(proj[:, GQA_Q + 2 * GQA_KV:]).astype(BF16)
    rows = k.shape[0] // nseq
    for s in range(nseq):
        kt_ref[s] = k[s * rows:(s + 1) * rows, :].T.astype(BF16)


def _gqa_ctx_kernel(k_ref, v_ref, kt_ref, vo_ref):
    kt_ref[0] = k_ref[...].T.astype(BF16)
    vo_ref[...] = _with_ones_lane(v_ref[...]).astype(BF16)


def _gqa_in(x, x_off, n, mod3, layer, row_fn, seq, p, tables):
    rope = tables is not None
    nseq = max(TM // seq, 1)
    rows = TM // nseq
    batch = n // seq
    tiles_per_seq = max(seq // TM, 1)
    row = lambda w: pl.BlockSpec((TM, w), lambda i: (i, 0))
    in_specs = [pl.BlockSpec((TM, D_MODEL), lambda i: (i + x_off, 0)), _mod_spec(layer, 1, row_fn),
                _const_spec((1, D_MODEL)),
                _const_spec((D_MODEL, GQA_W_COLS)), _const_spec((1, GQA_Q)), _const_spec((1, GQA_KV))]
    args = [x, mod3, p["g_mix"], p["w"], p["g_q"], p["g_k"]]
    kt_spec = pl.BlockSpec((nseq, GQA_KV, rows), lambda i: (i // tiles_per_seq, 0, i % tiles_per_seq))
    out_specs = [row(GQA_Q), kt_spec, row(GQA_VPAD)]
    out_shape = [jax.ShapeDtypeStruct((n, GQA_Q), BF16), jax.ShapeDtypeStruct((batch, GQA_KV, seq), BF16),
                 jax.ShapeDtypeStruct((n, GQA_VPAD), BF16)]
    if rope:
        tab = pl.BlockSpec((TM, LANES), lambda i: (i % tiles_per_seq, 0))
        in_specs += [tab, tab, tab]
        args += list(tables)
    else:
        out_specs += [row(GQA_KV), row(GQA_KV)]
        out_shape += [jax.ShapeDtypeStruct((n, GQA_KV), F32), jax.ShapeDtypeStruct((n, GQA_KV), F32)]
    return pl.pallas_call(
        functools.partial(_gqa_in_kernel, rope=rope, nseq=nseq),
        grid=(n // TM,),
        in_specs=in_specs,
        out_specs=out_specs,
        out_shape=out_shape,
        compiler_params=_params(1),
        name="gqa_in_rope" if rope else "gqa_in",
    )(*args)


def _gqa_ctx(k, vpad, seq):
    n = k.shape[0]
    return pl.pallas_call(
        _gqa_ctx_kernel,
        grid=(n // seq,),
        in_specs=[pl.BlockSpec((seq, GQA_KV), lambda i: (i, 0)), pl.BlockSpec((seq, GQA_VPAD), lambda i: (i, 0))],
        out_specs=[pl.BlockSpec((1, GQA_KV, seq), lambda i: (i, 0, 0)), pl.BlockSpec((seq, GQA_VPAD), lambda i: (i, 0))],
        out_shape=[jax.ShapeDtypeStruct((n // seq, GQA_KV, seq), BF16), jax.ShapeDtypeStruct((n, GQA_VPAD), BF16)],
        compiler_params=_params(1),
        name="gqa_ctx",
    )(k, vpad)


def _attn_kernel(*refs, nseg, bt, seq, nq, nkv, dq, dk, dv):
    q_ref = refs[0]
    segs = [(refs[1 + 2 * s], refs[2 + 2 * s]) for s in range(nseg)]
    o_ref = refs[1 + 2 * nseg]
    s_refs = refs[2 + 2 * nseg:4 + 2 * nseg]
    nks = [kt_ref.shape[2] for kt_ref, _ in segs]
    offs = [sum(nks[:s]) for s in range(nseg)]

    def scores(bi, r0, s_ref):
        rows = pl.ds(bi * seq + r0, Q_SUB)
        for j in range(nq):
            kv = j * nkv // nq
            qs = q_ref[rows, j * dq:(j + 1) * dq]
            for (kt_ref, _), off, nk in zip(segs, offs, nks):
                s_ref[j, :, off:off + nk] = jnp.dot(qs, kt_ref[bi, kv * dk:(kv + 1) * dk, :],
                                                    preferred_element_type=F32)

    def softmax_pv(bi, r0, s_ref):
        rows = pl.ds(bi * seq + r0, Q_SUB)
        outs = []
        for j in range(nq):
            kv = j * nkv // nq
            s = s_ref[j]
            p = jnp.exp2(s - jnp.max(s, axis=-1, keepdims=True)).astype(BF16)
            o = None
            for (_, v_ref), off, nk in zip(segs, offs, nks):
                pv = jnp.dot(p[:, off:off + nk], v_ref[bi * nk:(bi + 1) * nk, kv * HEAD_PAD:(kv + 1) * HEAD_PAD],
                             preferred_element_type=F32)
                o = pv if o is None else o + pv
            outs.append(o[:, :dv] / o[:, DENOM_LANE:DENOM_LANE + 1])
        o_ref[rows, :] = jnp.concatenate(outs, axis=1).astype(BF16)

    if seq == Q_SUB:
        assert bt == len(s_refs)
        for bi in range(bt):
            scores(bi, 0, s_refs[bi])
        for bi in range(bt):
            softmax_pv(bi, 0, s_refs[bi])
    else:
        n_sub = seq // Q_SUB
        assert bt == 1 and n_sub % 2 == 0
        sub = lambda r: pl.multiple_of(r * Q_SUB, Q_SUB)
        scores(0, 0, s_refs[0])

        def body(t, carry):
            scores(0, sub(2 * t + 1), s_refs[1])
            softmax_pv(0, sub(2 * t), s_refs[0])
            scores(0, sub(2 * t + 2), s_refs[0])
            softmax_pv(0, sub(2 * t + 1), s_refs[1])
            return carry
        lax.fori_loop(0, n_sub // 2 - 1, body, 0)
        scores(0, (n_sub - 1) * Q_SUB, s_refs[1])
        softmax_pv(0, (n_sub - 2) * Q_SUB, s_refs[0])
        softmax_pv(0, (n_sub - 1) * Q_SUB, s_refs[1])


def _attention(q, segs, *, seq, heads, kv_heads, dq, dk, dv, bt, heads_per_step):
    n = q.shape[0]
    batch = n // seq
    nq = heads_per_step
    steps = heads // nq
    nkv = max(kv_heads // steps, 1)
    q_per_kv_block = steps // (kv_heads // nkv)
    in_specs = [pl.BlockSpec((bt * seq, nq * dq), lambda b, p: (b, p))]
    args = [q]
    for kt, v in segs:
        nk = kt.shape[2]
        in_specs.append(pl.BlockSpec((bt, nkv * dk, nk), lambda b, p: (b, p // q_per_kv_block, 0)))
        in_specs.append(pl.BlockSpec((bt * nk, nkv * HEAD_PAD), lambda b, p: (b, p // q_per_kv_block)))
        args += [kt, v]
    return pl.pallas_call(
        functools.partial(_attn_kernel, nseg=len(segs), bt=bt, seq=seq, nq=nq, nkv=nkv, dq=dq, dk=dk, dv=dv),
        grid=(batch // bt, steps),
        in_specs=in_specs,
        out_specs=pl.BlockSpec((bt * seq, nq * dv), lambda b, p: (b, p)),
        out_shape=jax.ShapeDtypeStruct((n, heads * dv), BF16),
        scratch_shapes=[pltpu.VMEM((nq, Q_SUB, sum(kt.shape[2] for kt, _ in segs)), F32)] * 2,
        compiler_params=_params(2),
        name="attention",
    )(*args)


def _mla_out_kernel(x_ref, mod_ref, attn_ref, glu_ref, *rest, nseq, halo, tiles_per_seq):
    if halo:
        prev_ref, next_ref, wdw_ref, bdw_ref, gln_ref, bln_ref, wout_ref, y_ref, pad_ref, cat_ref = rest
    else:
        wdw_ref, bdw_ref, gln_ref, bln_ref, wout_ref, y_ref, pad_ref, cat_ref = rest
    rows = TM // nseq
    zeros = jnp.zeros((HALO, CONV_CH), F32)
    for s in range(nseq):
        if halo:
            j = pl.program_id(0) % tiles_per_seq
            pad_ref[s, 0:HALO, :] = jnp.where(j == 0, zeros, prev_ref[...])
            pad_ref[s, HALO + rows:2 * HALO + rows, :] = jnp.where(j == tiles_per_seq - 1, zeros, next_ref[...])
        else:
            pad_ref[s, 0:HALO, :] = zeros
            pad_ref[s, HALO + rows:2 * HALO + rows, :] = zeros
        pad_ref[s, HALO:HALO + rows, :] = glu_ref[s * rows:(s + 1) * rows, :]
    cat_ref[:, :CONV_CH] = attn_ref[...]
    sub = 8
    lead = HALO - CONV_PAD
    win_rows = CONV_ROWS + 2 * HALO

    def conv_chunk(s, r0):
        blocks = []
        for lb in range(CONV_CH // LANES):
            lanes = slice(lb * LANES, (lb + 1) * LANES)
            win = pad_ref[s, pl.ds(r0, win_rows), lanes]
            acc = jnp.broadcast_to(bdw_ref[:, lanes], (CONV_ROWS, LANES))
            for phase in range(sub):
                shifted = win if phase == 0 else pltpu.roll(win, win_rows - phase, 0)
                for a in range((CONV_W + lead) // sub + 1):
                    k = a * sub + phase - lead
                    if 0 <= k < CONV_W:
                        acc = acc + shifted[a * sub:a * sub + CONV_ROWS] * wdw_ref[k:k + 1, lanes]
            blocks.append(acc)
        acc = jnp.concatenate(blocks, axis=1)
        mu = jnp.mean(acc, axis=-1, keepdims=True)
        d = acc - mu
        var = jnp.mean(d * d, axis=-1, keepdims=True)
        g = _silu(d * lax.rsqrt(var + EPS) * gln_ref[...] + bln_ref[...])
        cat_ref[pl.ds(s * rows + r0, CONV_ROWS), CONV_CH:] = g.astype(BF16)

    for s in range(nseq):
        def body(c, carry, s=s):
            conv_chunk(s, pl.multiple_of(c * CONV_ROWS, CONV_ROWS))
            return carry
        lax.fori_loop(0, rows // CONV_ROWS, body, 0)
    out = jnp.dot(cat_ref[...], wout_ref[...], preferred_element_type=F32)
    y_ref[...] = x_ref[...] + mod_ref[2:3, :] * out


def _mla_out(x, x_off, mod3, layer, row_fn, seq, attn, glu, p):
    n = attn.shape[0]
    nseq = max(TM // seq, 1)
    tiles_per_seq = max(seq // TM, 1)
    halo = tiles_per_seq > 1
    row = lambda w: pl.BlockSpec((TM, w), lambda i: (i, 0))
    in_specs = [pl.BlockSpec((TM, D_MODEL), lambda i: (i + x_off, 0)), _mod_spec(layer, 1, row_fn),
                row(CONV_CH), row(CONV_CH)]
    args = [x, mod3, attn, glu]
    if halo:
        per = TM // HALO
        last = n // HALO - 1
        in_specs += [pl.BlockSpec((HALO, CONV_CH), lambda i: (jnp.maximum(i * per - 1, 0), 0)),
                     pl.BlockSpec((HALO, CONV_CH), lambda i: (jnp.minimum((i + 1) * per, last), 0))]
        args += [glu, glu]
    in_specs += [_const_spec((CONV_W + 1, CONV_CH)), _const_spec((1, CONV_CH)), _const_spec((1, CONV_CH)),
                 _const_spec((1, CONV_CH)), _const_spec((2 * CONV_CH, D_MODEL))]
    args += [p["w_dw"], p["b_dw"], p["g_ln"], p["b_ln"], p["w_out"]]
    return pl.pallas_call(
        functools.partial(_mla_out_kernel, nseq=nseq, halo=halo, tiles_per_seq=tiles_per_seq),
        grid=(n // TM,),
        in_specs=in_specs,
        out_specs=row(D_MODEL),
        out_shape=jax.ShapeDtypeStruct((n, D_MODEL), F32),
        scratch_shapes=[pltpu.VMEM((nseq, TM // nseq + 2 * HALO, CONV_CH), F32),
                        pltpu.VMEM((TM, 2 * CONV_CH), BF16)],
        compiler_params=_params(1),
        name="mla_out",
    )(*args)


def _gqa_out_kernel(x_ref, mod_ref, attn_ref, wout_ref, y_ref):
    out = jnp.dot(attn_ref[...], wout_ref[...], preferred_element_type=F32)
    y_ref[...] = x_ref[...] + mod_ref[2:3, :] * out


def _gqa_out(x, x_off, mod3, layer, row_fn, attn, w_out):
    n = attn.shape[0]
    row = lambda w: pl.BlockSpec((TM, w), lambda i: (i, 0))
    return pl.pallas_call(
        _gqa_out_kernel,
        grid=(n // TM,),
        in_specs=[pl.BlockSpec((TM, D_MODEL), lambda i: (i + x_off, 0)), _mod_spec(layer, 1, row_fn), row(GQA_Q),
                  _const_spec((GQA_Q, D_MODEL))],
        out_specs=row(D_MODEL),
        out_shape=jax.ShapeDtypeStruct((n, D_MODEL), F32),
        compiler_params=_params(1),
        name="gqa_out",
    )(x, mod3, attn, w_out)


def _rope_tables(d, offset, period):
    half = d // 2
    quarter = half // 2
    rows = DEC_SEQ // GRID_W
    pos_row = jnp.repeat(jnp.arange(rows), GRID_W)
    pos_col = jnp.tile(jnp.arange(GRID_W), rows)
    inv = ROPE_BASE ** (-jnp.arange(0, half, 2, dtype=F32) / half)
    ang_r = pos_row.astype(F32)[:, None] * inv[None, :]
    ang_c = pos_col.astype(F32)[:, None] * inv[None, :]
    zero = jnp.zeros((DEC_SEQ, quarter), F32)
    cos = jnp.concatenate([jnp.cos(ang_r), jnp.cos(ang_r), jnp.cos(ang_c), jnp.cos(ang_c)], axis=1)
    sa = jnp.concatenate([-jnp.sin(ang_r), zero, -jnp.sin(ang_c), zero], axis=1)
    sb = jnp.concatenate([zero, jnp.sin(ang_r), zero, jnp.sin(ang_c)], axis=1)

    def embed(t, fill):
        blk = jnp.concatenate([jnp.full((DEC_SEQ, offset), fill, F32), t,
                               jnp.full((DEC_SEQ, period - offset - d), fill, F32)], axis=1)
        return jnp.tile(blk, (1, LANES // period))

    return embed(cos, 1.0), embed(sa, 0.0), embed(sb, 0.0)


def _pad_heads(w, heads, dim):
    k = w.shape[0]
    return jnp.pad(w.reshape(k, heads, dim), ((0, 0), (0, 0), (0, HEAD_PAD - dim))).reshape(k, heads * HEAD_PAD)


def _prep_mla(i, g_mix_l, w_in_a, g_q_lora, w_q_up, g_kv_lora, w_kv_up, w_dw, b_dw, g_conv_ln, b_conv_ln, w_out_a):
    w = w_in_a[i]
    o1 = Q_LORA
    o2 = o1 + KV_LORA
    o3 = o2 + QK_ROPE
    zeros = lambda c: jnp.zeros((D_MODEL, c), F32)
    w1 = jnp.concatenate([w[:, :o2], w[:, o3:], zeros(KR_LANE), w[:, o2:o3],
                          zeros(LANES - KR_LANE - QK_ROPE)], axis=1)
    kvu = w_kv_up[i].reshape(KV_LORA, MLA_HEADS, QK_NOPE + V_HEAD)
    w_kn = _pad_heads(kvu[:, :, :QK_NOPE].reshape(KV_LORA, MLA_HEADS * QK_NOPE), MLA_HEADS, QK_NOPE)
    w_v = _pad_heads(kvu[:, :, QK_NOPE:].reshape(KV_LORA, MLA_HEADS * V_HEAD), MLA_HEADS, V_HEAD)
    return dict(
        g_mix=g_mix_l.reshape(1, D_MODEL),
        w1=w1.astype(BF16),
        g_q=g_q_lora[i].reshape(1, Q_LORA),
        w_qu=_pad_heads(w_q_up[i], MLA_HEADS, QK_NOPE + QK_ROPE).astype(BF16),
        g_kv=g_kv_lora[i].reshape(1, KV_LORA),
        w_kv=jnp.concatenate([w_kn, w_v], axis=1).astype(BF16),
        w_dw=jnp.pad(w_dw[i], ((0, 1), (0, 0))),
        b_dw=b_dw[i].reshape(1, CONV_CH),
        g_ln=g_conv_ln[i].reshape(1, CONV_CH),
        b_ln=b_conv_ln[i].reshape(1, CONV_CH),
        w_out=w_out_a[i].astype(BF16),
    )


def _prep_gqa(i, g_mix_l, w_in_c, g_q_head, g_k_head, w_out_c):
    w = w_in_c[i]
    v_pad = _pad_heads(w[:, GQA_Q + GQA_KV:], GQA_KV_HEADS, GQA_HEAD_DIM)
    return dict(
        g_mix=g_mix_l.reshape(1, D_MODEL),
        w=jnp.concatenate([w, v_pad], axis=1).astype(BF16),
        g_q=jnp.tile(g_q_head[i], GQA_HEADS).reshape(1, GQA_Q),
        g_k=jnp.tile(g_k_head[i], GQA_KV_HEADS).reshape(1, GQA_KV),
        w_out=w_out_c[i].astype(BF16),
    )


SAMPLE_HEADS_PER_STEP = 4
PROMPT_SEQS_PER_STEP = 2


def _mixer(x_all, x_off, n, mod3, l, row_fn, seq, p, ctx, tables):
    sample = ctx is not None
    if l % 2 == 0:
        outs = _mla_in(x_all, x_off, n, mod3, l, row_fn, seq, p, tables["mla"] if sample else None)
        q, kt, v, glu = outs[:4]
        segs = [(kt, v)]
        if sample:
            segs = [_mla_ctx(ctx["mla_ckv"], ctx["mla_krb"], p["w_kv"], ctx["past"])] + segs
            hps, bt = SAMPLE_HEADS_PER_STEP, 1
        else:
            hps, bt = MLA_HEADS, PROMPT_SEQS_PER_STEP
        attn = _attention(q, segs, seq=seq, heads=MLA_HEADS, kv_heads=MLA_HEADS, dq=HEAD_PAD, dk=HEAD_PAD,
                          dv=V_HEAD, bt=bt, heads_per_step=hps)
        return _mla_out(x_all, x_off, mod3, l, row_fn, seq, attn, glu, p), outs[4:]
    outs = _gqa_in(x_all, x_off, n, mod3, l, row_fn, seq, p, tables["gqa"] if sample else None)
    q, kt, v = outs[:3]
    segs = [(kt, v)]
    if sample:
        segs = [_gqa_ctx(ctx["gqa_k"], ctx["gqa_vpad"], ctx["past"])] + segs
        hps, bt = SAMPLE_HEADS_PER_STEP, 1
    else:
        hps, bt = GQA_HEADS, PROMPT_SEQS_PER_STEP
    attn = _attention(q, segs, seq=seq, heads=GQA_HEADS, kv_heads=GQA_KV_HEADS, dq=GQA_HEAD_DIM,
                      dk=GQA_HEAD_DIM, dv=GQA_HEAD_DIM, bt=bt, heads_per_step=hps)
    return _gqa_out(x_all, x_off, mod3, l, row_fn, attn, p["w_out"]), outs[3:]


def kernel(x_prompt, x_sample, cache_mla_ckv, cache_mla_krope, cache_gqa_k, cache_gqa_v, c, c_ctx, g_ff1, w_ff1_in, w_ff1_out, g_mix, g_ff2, w_ff2_in, w_ff2_out, w_mod, b_mod, w_in_a, g_q_lora, w_q_up, g_kv_lora, w_kv_up, w_dw, b_dw, g_conv_ln, b_conv_ln, w_out_a, w_in_c, g_q_head, g_k_head, w_out_c, g_final):
    batch, seq, _ = x_prompt.shape
    dec_batch, dec_seq, _ = x_sample.shape
    past = cache_mla_ckv.shape[2]
    assert DEPTH == 2 and dec_seq == DEC_SEQ and 1 + dec_batch <= MOD_ROWS

    cond = jnp.concatenate([c_ctx[None, :], c, jnp.zeros((MOD_ROWS - 1 - dec_batch, D_MODEL), F32)], axis=0)
    mod3 = _modulation(cond, w_mod, b_mod)

    layers = [
        _prep_mla(0, g_mix[0], w_in_a, g_q_lora, w_q_up, g_kv_lora, w_kv_up, w_dw, b_dw, g_conv_ln, b_conv_ln,
                  w_out_a),
        _prep_gqa(0, g_mix[1], w_in_c, g_q_head, g_k_head, w_out_c),
    ]
    tables = dict(mla=_rope_tables(QK_ROPE, KR_LANE, HEAD_PAD), gqa=_rope_tables(GQA_HEAD_DIM, 0, GQA_HEAD_DIM))

    ctx = dict(
        past=past,
        mla_ckv=cache_mla_ckv[:, 0].reshape(dec_batch * past, KV_LORA),
        mla_krb=jnp.pad(cache_mla_krope[:, 0].reshape(dec_batch * past, QK_ROPE),
                        ((0, 0), (KR_LANE, LANES - KR_LANE - QK_ROPE))),
        gqa_k=cache_gqa_k[:, 0].reshape(dec_batch * past, GQA_KV),
        gqa_vpad=jnp.pad(cache_gqa_v[:, 0], ((0, 0), (0, 0), (0, 0), (0, HEAD_PAD - GQA_HEAD_DIM))
                         ).reshape(dec_batch * past, GQA_VPAD),
    )

    n_p, n_s = batch * seq, dec_batch * dec_seq
    nb_p = n_p // TM
    row_p = lambda i: 0
    row_s = lambda i: 1 + (i * TM) // DEC_SEQ
    row_all = lambda i: jnp.where(i < nb_p, 0, 1 + ((i - nb_p) * TM) // DEC_SEQ)

    parts = [x_prompt.reshape(n_p, D_MODEL), x_sample.reshape(n_s, D_MODEL)]
    saved = []
    for l in range(DEPTH):
        x_all = _ffn(parts, mod3, l, 0, row_all, g_ff1, w_ff1_in, w_ff1_out)
        xp, st = _mixer(x_all, 0, n_p, mod3, l, row_p, seq, layers[l], None, tables)
        xs, _ = _mixer(x_all, nb_p, n_s, mod3, l, row_s, dec_seq, layers[l], ctx, tables)
        saved.append(st)
        if l < DEPTH - 1:
            parts = [_ffn([xp, xs], mod3, l, 2, row_all, g_ff2, w_ff2_in, w_ff2_out)]
        else:
            y_prompt = _ffn([xp], mod3, l, 2, row_p, g_ff2, w_ff2_in, w_ff2_out, g_final=g_final)
            y_sample = _ffn([xs], mod3, l, 2, row_s, g_ff2, w_ff2_in, w_ff2_out, g_final=g_final)
    (ckv_new, kr_new), (k_new, v_new) = saved

    return (y_prompt.reshape(batch, seq, D_MODEL),
            y_sample.reshape(dec_batch, dec_seq, D_MODEL),
            ckv_new.reshape(batch, 1, seq, KV_LORA),
            kr_new.reshape(batch, 1, seq, QK_ROPE),
            k_new.reshape(batch, 1, seq, GQA_KV_HEADS, GQA_HEAD_DIM),
            v_new.reshape(batch, 1, seq, GQA_KV_HEADS, GQA_HEAD_DIM))
```

```python
import functools

import jax
import jax.numpy as jnp
from jax import lax
from jax.experimental import pallas as pl
from jax.experimental.pallas import tpu as pltpu

F32 = jnp.float32
BF16 = jnp.bfloat16

D_MODEL = 1024
DEPTH = 2
DEC_SEQ = 2048
GRID_W = 64
MLA_HEADS = 8
Q_LORA = 384
KV_LORA = 256
QK_NOPE = 64
QK_ROPE = 32
V_HEAD = 64
CONV_CH = 512
CONV_W = 31
CONV_PAD = CONV_W // 2
GQA_HEADS = 16
GQA_KV_HEADS = 4
GQA_HEAD_DIM = 64
D_FF = 2816
MACARON = 0.5
N_MOD = 9
ROPE_BASE = 10000.0
EPS = 1e-6

LANES = 128
HEAD_PAD = 128
MOD_ROWS = 8
DENOM_LANE = 64
LOG2E = 1.4426950408889634
HALO = 16
TM = 512
Q_SUB = 256
FF_CHUNK = 256
CONV_ROWS = 64
INTERLEAVE_SLACK = 2
VMEM_LIMIT = 56 * 1024 * 1024


def _params(n_axes):
    return pltpu.CompilerParams(dimension_semantics=("parallel",) * n_axes, vmem_limit_bytes=VMEM_LIMIT)


def _const_spec(shape):
    return pl.BlockSpec(shape, lambda *_: (0,) * len(shape), pipeline_mode=pl.Buffered(1))


def _rms(x, g):
    return x * lax.rsqrt(jnp.mean(x * x, axis=-1, keepdims=True) + EPS) * g


def _mod_norm(x, g, mod):
    return _rms(x, g) * (1.0 + mod[1:2]) + mod[0:1]


def _silu(x):
    return x * jax.nn.sigmoid(x)


def _rope(x, cos, sa, sb, quarter):
    w = x.shape[-1]
    return x * cos + pltpu.roll(x, w - quarter, 1) * sa + pltpu.roll(x, quarter, 1) * sb


def _with_ones_lane(v):
    lane = lax.broadcasted_iota(jnp.int32, v.shape, 1)
    return jnp.where(lane % HEAD_PAD == DENOM_LANE, 1.0, v)


def _tile_lanes(t, reps):
    return t if reps == 1 else jnp.concatenate([t] * reps, axis=1)


def _mod_kernel(c_ref, w_ref, b_ref, o_ref):
    e = _silu(c_ref[...]).astype(BF16)
    o_ref[...] = jnp.dot(e, w_ref[...].astype(BF16), preferred_element_type=F32) + b_ref[...]


def _modulation(cond, w_mod, b_mod):
    n_out = N_MOD * D_MODEL
    tn = 1536
    out = pl.pallas_call(
        _mod_kernel,
        grid=(DEPTH, n_out // tn),
        in_specs=[
            pl.BlockSpec((MOD_ROWS, D_MODEL), lambda l, j: (0, 0)),
            pl.BlockSpec((None, D_MODEL, tn), lambda l, j: (l, 0, j)),
            pl.BlockSpec((None, 1, tn), lambda l, j: (l, 0, j)),
        ],
        out_specs=pl.BlockSpec((None, MOD_ROWS, tn), lambda l, j: (l, 0, j)),
        out_shape=jax.ShapeDtypeStruct((DEPTH, MOD_ROWS, n_out), F32),
        compiler_params=_params(2),
        name="modulation",
    )(cond, w_mod, b_mod.reshape(DEPTH, 1, n_out))
    return out.reshape(DEPTH * MOD_ROWS * 3, 3, D_MODEL)


def _mod_spec(layer, sub, row_fn):
    return pl.BlockSpec((None, 3, D_MODEL), lambda i, *_: ((layer * MOD_ROWS + row_fn(i)) * 3 + sub, 0, 0))


def _cast_kernel(w_ref, o_ref):
    o_ref[...] = w_ref[...].astype(BF16)


def _to_bf16(w, rows):
    depth, k, n = w.shape
    spec = pl.BlockSpec((None, rows, n), lambda l, i: (l, i, 0))
    return pl.pallas_call(
        _cast_kernel,
        grid=(depth, k // rows),
        in_specs=[spec],
        out_specs=spec,
        out_shape=jax.ShapeDtypeStruct(w.shape, BF16),
        compiler_params=_params(2),
        name="to_bf16",
    )(w)


def _ffn_body(x, mod, g_ref, win_ref, wout_ref, between_chunks=None):
    h = _mod_norm(x, g_ref[...], mod).astype(BF16)
    acc = jnp.zeros(x.shape, F32)
    zeros = [None] * INTERLEAVE_SLACK
    for j in range(D_FF // FF_CHUNK):
        lo = j * FF_CHUNK
        a = jnp.dot(h, win_ref[:, lo:lo + FF_CHUNK].astype(BF16), preferred_element_type=F32)
        b = jnp.dot(h, win_ref[:, D_FF + lo:D_FF + lo + FF_CHUNK].astype(BF16), preferred_element_type=F32)
        act = (_silu(a) * b).astype(BF16)
        zero = zeros.pop(0)
        rows = pl.ds(lo, FF_CHUNK) if zero is None else pl.ds(pl.multiple_of(lo + zero, FF_CHUNK), FF_CHUNK)
        acc = acc + jnp.dot(act, wout_ref[rows, :].astype(BF16), preferred_element_type=F32)
        zeros.append(between_chunks(j) if between_chunks is not None else None)
    return x + (MACARON * mod[2:3]) * acc


def _ffn_kernel(*refs, nparts, nblk0):
    x_refs = refs[:nparts]
    mod_ref, g_ref, win_ref, wout_ref, o_ref = refs[nparts:]
    x = x_refs[0][...]
    if nparts == 2:
        x = jnp.where(pl.program_id(0) < nblk0, x, x_refs[1][...])
    o_ref[...] = _ffn_body(x, mod_ref[...], g_ref, win_ref, wout_ref)


def _ffn_weight_specs(layer):
    layer_spec = lambda shape: pl.BlockSpec((None,) + shape, lambda i: (layer, 0, 0), pipeline_mode=pl.Buffered(1))
    return [layer_spec((1, D_MODEL)), layer_spec((D_MODEL, 2 * D_FF)), layer_spec((D_FF, D_MODEL))]


def _ffn(parts, mod3, layer, sub, row_fn, g, w_in, w_out):
    nblks = [p.shape[0] // TM for p in parts]
    if len(parts) == 1:
        in_specs = [pl.BlockSpec((TM, D_MODEL), lambda i: (i, 0))]
    else:
        nb0 = nblks[0]
        in_specs = [pl.BlockSpec((TM, D_MODEL), lambda i: (jnp.minimum(i, nb0 - 1), 0)),
                    pl.BlockSpec((TM, D_MODEL), lambda i: (jnp.maximum(i - nb0, 0), 0))]
    in_specs += [_mod_spec(layer, sub, row_fn)] + _ffn_weight_specs(layer)
    n = sum(nblks) * TM
    return pl.pallas_call(
        functools.partial(_ffn_kernel, nparts=len(parts), nblk0=nblks[0]),
        grid=(n // TM,),
        in_specs=in_specs,
        out_specs=pl.BlockSpec((TM, D_MODEL), lambda i: (i, 0)),
        out_shape=jax.ShapeDtypeStruct((n, D_MODEL), F32),
        compiler_params=_params(1),
        name="ffn",
    )(*parts, mod3, g.reshape(DEPTH, 1, D_MODEL), w_in, w_out)


MLA_W1_COLS = Q_LORA + KV_LORA + 2 * CONV_CH + LANES
MLA_QK = MLA_HEADS * HEAD_PAD
KR_LANE = QK_NOPE


def _mla_kv(ckvn, krb, wkv_ref, kt_ref, v_ref, nseq):
    kv = jnp.dot(ckvn.astype(BF16), wkv_ref[...], preferred_element_type=F32)
    k = kv[:, :MLA_QK] + _tile_lanes(krb, MLA_HEADS)
    v_ref[...] = _with_ones_lane(kv[:, MLA_QK:]).astype(BF16)
    rows = k.shape[0] // nseq
    for s in range(nseq):
        kt_ref[s] = k[s * rows:(s + 1) * rows, :].T.astype(BF16)


def _mla_in_kernel(x_ref, mod_ref, g_ref, w1_ref, gq_ref, wqu_ref, gkv_ref, wkv_ref, *rest, rope, nseq):
    if rope:
        cos_ref, sa_ref, sb_ref, q_ref, kt_ref, v_ref, glu_ref = rest
    else:
        q_ref, kt_ref, v_ref, glu_ref, ckv_ref, kr_ref = rest
    h = _mod_norm(x_ref[...], g_ref[...], mod_ref[...]).astype(BF16)
    proj = jnp.dot(h, w1_ref[...], preferred_element_type=F32)
    o1 = Q_LORA
    o2 = o1 + KV_LORA
    o3 = o2 + CONV_CH
    o4 = o3 + CONV_CH
    cq, ckv, ua, ub, krb = proj[:, :o1], proj[:, o1:o2], proj[:, o2:o3], proj[:, o3:o4], proj[:, o4:]
    glu_ref[...] = ua * jax.nn.sigmoid(ub)
    q = jnp.dot(_rms(cq, gq_ref[...]).astype(BF16), wqu_ref[...], preferred_element_type=F32)
    ckvn = _rms(ckv, gkv_ref[...])
    if rope:
        cos, sa, sb = cos_ref[...], sa_ref[...], sb_ref[...]
        quarter = QK_ROPE // 4
        krb = _rope(krb, cos, sa, sb, quarter)
        q = _rope(q, _tile_lanes(cos, MLA_HEADS), _tile_lanes(sa, MLA_HEADS), _tile_lanes(sb, MLA_HEADS), quarter)
    else:
        ckv_ref[...] = ckvn
        kr_ref[...] = krb[:, KR_LANE:KR_LANE + QK_ROPE]
    q_ref[...] = (q * ((QK_NOPE + QK_ROPE) ** -0.5 * LOG2E)).astype(BF16)
    _mla_kv(ckvn, krb, wkv_ref, kt_ref, v_ref, nseq)


def _mla_ctx_kernel(ckv_ref, krb_ref, wkv_ref, kt_ref, v_ref):
    _mla_kv(ckv_ref[...], krb_ref[...], wkv_ref, kt_ref, v_ref, 1)


def _mla_in(x, x_off, n, mod3, layer, row_fn, seq, p, tables):
    rope = tables is not None
    nseq = max(TM // seq, 1)
    rows = TM // nseq
    batch = n // seq
    tiles_per_seq = max(seq // TM, 1)
    row = lambda w: pl.BlockSpec((TM, w), lambda i: (i, 0))
    in_specs = [pl.BlockSpec((TM, D_MODEL), lambda i: (i + x_off, 0)), _mod_spec(layer, 1, row_fn),
                _const_spec((1, D_MODEL)),
                _const_spec((D_MODEL, MLA_W1_COLS)), _const_spec((1, Q_LORA)), _const_spec((Q_LORA, MLA_QK)),
                _const_spec((1, KV_LORA)), _const_spec((KV_LORA, 2 * MLA_QK))]
    args = [x, mod3, p["g_mix"], p["w1"], p["g_q"], p["w_qu"], p["g_kv"], p["w_kv"]]
    kt_spec = pl.BlockSpec((nseq, MLA_QK, rows), lambda i: (i // tiles_per_seq, 0, i % tiles_per_seq))
    out_specs = [row(MLA_QK), kt_spec, row(MLA_QK), row(CONV_CH)]
    out_shape = [jax.ShapeDtypeStruct((n, MLA_QK), BF16), jax.ShapeDtypeStruct((batch, MLA_QK, seq), BF16),
                 jax.ShapeDtypeStruct((n, MLA_QK), BF16), jax.ShapeDtypeStruct((n, CONV_CH), F32)]
    if rope:
        tab = pl.BlockSpec((TM, LANES), lambda i: (i % tiles_per_seq, 0))
        in_specs += [tab, tab, tab]
        args += list(tables)
    else:
        out_specs += [row(KV_LORA), row(QK_ROPE)]
        out_shape += [jax.ShapeDtypeStruct((n, KV_LORA), F32), jax.ShapeDtypeStruct((n, QK_ROPE), F32)]
    return pl.pallas_call(
        functools.partial(_mla_in_kernel, rope=rope, nseq=nseq),
        grid=(n // TM,),
        in_specs=in_specs,
        out_specs=out_specs,
        out_shape=out_shape,
        compiler_params=_params(1),
        name="mla_in_rope" if rope else "mla_in",
    )(*args)


def _mla_ctx(ckv, krb, w_kv, seq):
    n = ckv.shape[0]
    return pl.pallas_call(
        _mla_ctx_kernel,
        grid=(n // seq,),
        in_specs=[pl.BlockSpec((seq, KV_LORA), lambda i: (i, 0)), pl.BlockSpec((seq, LANES), lambda i: (i, 0)),
                  _const_spec((KV_LORA, 2 * MLA_QK))],
        out_specs=[pl.BlockSpec((1, MLA_QK, seq), lambda i: (i, 0, 0)), pl.BlockSpec((seq, MLA_QK), lambda i: (i, 0))],
        out_shape=[jax.ShapeDtypeStruct((n // seq, MLA_QK, seq), BF16), jax.ShapeDtypeStruct((n, MLA_QK), BF16)],
        compiler_params=_params(1),
        name="mla_ctx",
    )(ckv, krb, w_kv)


GQA_Q = GQA_HEADS * GQA_HEAD_DIM
GQA_KV = GQA_KV_HEADS * GQA_HEAD_DIM
GQA_VPAD = GQA_KV_HEADS * HEAD_PAD
GQA_W_COLS = GQA_Q + 2 * GQA_KV + GQA_VPAD


def _head_rms(x, g):
    tm, w = x.shape
    lo = lax.broadcasted_iota(jnp.int32, (tm, LANES), 1) < GQA_HEAD_DIM
    outs = []
    for b in range(w // LANES):
        xb = x[:, b * LANES:(b + 1) * LANES]
        sq = xb * xb
        s_lo = jnp.sum(jnp.where(lo, sq, 0.0), axis=-1, keepdims=True)
        s_hi = jnp.sum(jnp.where(lo, 0.0, sq), axis=-1, keepdims=True)
        ms = jnp.where(lo, s_lo, s_hi) * (1.0 / GQA_HEAD_DIM)
        outs.append(xb * lax.rsqrt(ms + EPS))
    return jnp.concatenate(outs, axis=1) * g


def _gqa_in_kernel(x_ref, mod_ref, g_ref, w_ref, gq_ref, gk_ref, *rest, rope, nseq):
    if rope:
        cos_ref, sa_ref, sb_ref, q_ref, kt_ref, v_ref = rest
    else:
        q_ref, kt_ref, v_ref, kc_ref, vc_ref = rest
    h = _mod_norm(x_ref[...], g_ref[...], mod_ref[...]).astype(BF16)
    proj = jnp.dot(h, w_ref[...], preferred_element_type=F32)
    q = _head_rms(proj[:, :GQA_Q], gq_ref[...])
    k = _head_rms(proj[:, GQA_Q:GQA_Q + GQA_KV], gk_ref[...])
    if rope:
        cos, sa, sb = cos_ref[...], sa_ref[...], sb_ref[...]
        quarter = GQA_HEAD_DIM // 4
        q = _rope(q, _tile_lanes(cos, GQA_Q // LANES), _tile_lanes(sa, GQA_Q // LANES),
                  _tile_lanes(sb, GQA_Q // LANES), quarter)
        k = _rope(k, _tile_lanes(cos, GQA_KV // LANES), _tile_lanes(sa, GQA_KV // LANES),
                  _tile_lanes(sb, GQA_KV // LANES), quarter)
    else:
        kc_ref[...] = k
        vc_ref[...] = proj[:, GQA_Q + GQA_KV:GQA_Q + 2 * GQA_KV]
    q_ref[...] = (q * (GQA_HEAD_DIM ** -0.5 * LOG2E)).astype(BF16)
    v_ref[...] = _with_ones_lane(proj[:, GQA_Q + 2 * GQA_KV:]).astype(BF16)
    rows = k.shape[0] // nseq
    for s in range(nseq):
        kt_ref[s] = k[s * rows:(s + 1) * rows, :].T.astype(BF16)


def _gqa_ctx_kernel(k_ref, v_ref, kt_ref, vo_ref):
    kt_ref[0] = k_ref[...].T.astype(BF16)
    vo_ref[...] = _with_ones_lane(v_ref[...]).astype(BF16)


def _gqa_in(x, x_off, n, mod3, layer, row_fn, seq, p, tables):
    rope = tables is not None
    nseq = max(TM // seq, 1)
    rows = TM // nseq
    batch = n // seq
    tiles_per_seq = max(seq // TM, 1)
    row = lambda w: pl.BlockSpec((TM, w), lambda i: (i, 0))
    in_specs = [pl.BlockSpec((TM, D_MODEL), lambda i: (i + x_off, 0)), _mod_spec(layer, 1, row_fn),
                _const_spec((1, D_MODEL)),
                _const_spec((D_MODEL, GQA_W_COLS)), _const_spec((1, GQA_Q)), _const_spec((1, GQA_KV))]
    args = [x, mod3, p["g_mix"], p["w"], p["g_q"], p["g_k"]]
    kt_spec = pl.BlockSpec((nseq, GQA_KV, rows), lambda i: (i // tiles_per_seq, 0, i % tiles_per_seq))
    out_specs = [row(GQA_Q), kt_spec, row(GQA_VPAD)]
    out_shape = [jax.ShapeDtypeStruct((n, GQA_Q), BF16), jax.ShapeDtypeStruct((batch, GQA_KV, seq), BF16),
                 jax.ShapeDtypeStruct((n, GQA_VPAD), BF16)]
    if rope:
        tab = pl.BlockSpec((TM, LANES), lambda i: (i % tiles_per_seq, 0))
        in_specs += [tab, tab, tab]
        args += list(tables)
    else:
        out_specs += [row(GQA_KV), row(GQA_KV)]
        out_shape += [jax.ShapeDtypeStruct((n, GQA_KV), F32), jax.ShapeDtypeStruct((n, GQA_KV), F32)]
    return pl.pallas_call(
        functools.partial(_gqa_in_kernel, rope=rope, nseq=nseq),
        grid=(n // TM,),
        in_specs=in_specs,
        out_specs=out_specs,
        out_shape=out_shape,
        compiler_params=_params(1),
        name="gqa_in_rope" if rope else "gqa_in",
    )(*args)


def _gqa_ctx(k, vpad, seq):
    n = k.shape[0]
    return pl.pallas_call(
        _gqa_ctx_kernel,
        grid=(n // seq,),
        in_specs=[pl.BlockSpec((seq, GQA_KV), lambda i: (i, 0)), pl.BlockSpec((seq, GQA_VPAD), lambda i: (i, 0))],
        out_specs=[pl.BlockSpec((1, GQA_KV, seq), lambda i: (i, 0, 0)), pl.BlockSpec((seq, GQA_VPAD), lambda i: (i, 0))],
        out_shape=[jax.ShapeDtypeStruct((n // seq, GQA_KV, seq), BF16), jax.ShapeDtypeStruct((n, GQA_VPAD), BF16)],
        compiler_params=_params(1),
        name="gqa_ctx",
    )(k, vpad)


def _attn_kernel(*refs, nseg, bt, seq, nq, nkv, dq, dk, dv):
    q_ref = refs[0]
    segs = [(refs[1 + 2 * s], refs[2 + 2 * s]) for s in range(nseg)]
    o_ref = refs[1 + 2 * nseg]
    s_refs = refs[2 + 2 * nseg:4 + 2 * nseg]
    nks = [kt_ref.shape[2] for kt_ref, _ in segs]
    offs = [sum(nks[:s]) for s in range(nseg)]

    def scores(bi, r0, s_ref):
        rows = pl.ds(bi * seq + r0, Q_SUB)
        for j in range(nq):
            kv = j * nkv // nq
            qs = q_ref[rows, j * dq:(j + 1) * dq]
            for (kt_ref, _), off, nk in zip(segs, offs, nks):
                s_ref[j, :, off:off + nk] = jnp.dot(qs, kt_ref[bi, kv * dk:(kv + 1) * dk, :],
                                                    preferred_element_type=F32)

    def softmax_pv(bi, r0, s_ref):
        rows = pl.ds(bi * seq + r0, Q_SUB)
        outs = []
        for j in range(nq):
            kv = j * nkv // nq
            s = s_ref[j]
            p = jnp.exp2(s - jnp.max(s, axis=-1, keepdims=True)).astype(BF16)
            o = None
            for (_, v_ref), off, nk in zip(segs, offs, nks):
                pv = jnp.dot(p[:, off:off + nk], v_ref[bi * nk:(bi + 1) * nk, kv * HEAD_PAD:(kv + 1) * HEAD_PAD],
                             preferred_element_type=F32)
                o = pv if o is None else o + pv
            outs.append(o[:, :dv] / o[:, DENOM_LANE:DENOM_LANE + 1])
        o_ref[rows, :] = jnp.concatenate(outs, axis=1).astype(BF16)

    if seq == Q_SUB:
        assert bt == len(s_refs)
        for bi in range(bt):
            scores(bi, 0, s_refs[bi])
        for bi in range(bt):
            softmax_pv(bi, 0, s_refs[bi])
    else:
        n_sub = seq // Q_SUB
        assert bt == 1 and n_sub % 2 == 0
        sub = lambda r: pl.multiple_of(r * Q_SUB, Q_SUB)
        scores(0, 0, s_refs[0])

        def body(t, carry):
            scores(0, sub(2 * t + 1), s_refs[1])
            softmax_pv(0, sub(2 * t), s_refs[0])
            scores(0, sub(2 * t + 2), s_refs[0])
            softmax_pv(0, sub(2 * t + 1), s_refs[1])
            return carry
        lax.fori_loop(0, n_sub // 2 - 1, body, 0)
        scores(0, (n_sub - 1) * Q_SUB, s_refs[1])
        softmax_pv(0, (n_sub - 2) * Q_SUB, s_refs[0])
        softmax_pv(0, (n_sub - 1) * Q_SUB, s_refs[1])


def _attention(q, segs, *, seq, heads, kv_heads, dq, dk, dv, bt, heads_per_step):
    n = q.shape[0]
    batch = n // seq
    nq = heads_per_step
    steps = heads // nq
    nkv = max(kv_heads // steps, 1)
    q_per_kv_block = steps // (kv_heads // nkv)
    in_specs = [pl.BlockSpec((bt * seq, nq * dq), lambda b, p: (b, p))]
    args = [q]
    for kt, v in segs:
        nk = kt.shape[2]
        in_specs.append(pl.BlockSpec((bt, nkv * dk, nk), lambda b, p: (b, p // q_per_kv_block, 0)))
        in_specs.append(pl.BlockSpec((bt * nk, nkv * HEAD_PAD), lambda b, p: (b, p // q_per_kv_block)))
        args += [kt, v]
    return pl.pallas_call(
        functools.partial(_attn_kernel, nseg=len(segs), bt=bt, seq=seq, nq=nq, nkv=nkv, dq=dq, dk=dk, dv=dv),
        grid=(batch // bt, steps),
        in_specs=in_specs,
        out_specs=pl.BlockSpec((bt * seq, nq * dv), lambda b, p: (b, p)),
        out_shape=jax.ShapeDtypeStruct((n, heads * dv), BF16),
        scratch_shapes=[pltpu.VMEM((nq, Q_SUB, sum(kt.shape[2] for kt, _ in segs)), F32)] * 2,
        compiler_params=_params(2),
        name="attention",
    )(*args)


SUBLANES = 8
CONV_LEAD = HALO - CONV_PAD
CONV_WIN = CONV_ROWS + 2 * HALO


def _fill_pad(pad_ref, glu_ref, prev_ref, next_ref, first, last, nseq):
    rows = TM // nseq
    zeros = jnp.zeros((HALO, CONV_CH), F32)
    for s in range(nseq):
        pad_ref[s, 0:HALO, :] = zeros if prev_ref is None else jnp.where(first, zeros, prev_ref[...])
        pad_ref[s, HALO + rows:2 * HALO + rows, :] = (
            zeros if next_ref is None else jnp.where(last, zeros, next_ref[...]))
        pad_ref[s, HALO:HALO + rows, :] = glu_ref[s * rows:(s + 1) * rows, :]


def _conv_chunk(pad_ref, s, r0, wdw_ref, bdw_ref, gln_ref, bln_ref):
    blocks = []
    for lb in range(CONV_CH // LANES):
        lanes = slice(lb * LANES, (lb + 1) * LANES)
        win = pad_ref[s, pl.ds(r0, CONV_WIN), lanes]
        acc = jnp.broadcast_to(bdw_ref[:, lanes], (CONV_ROWS, LANES))
        for phase in range(SUBLANES):
            shifted = win if phase == 0 else pltpu.roll(win, CONV_WIN - phase, 0)
            for a in range((CONV_W + CONV_LEAD) // SUBLANES + 1):
                k = a * SUBLANES + phase - CONV_LEAD
                if 0 <= k < CONV_W:
                    acc = acc + shifted[a * SUBLANES:a * SUBLANES + CONV_ROWS] * wdw_ref[k:k + 1, lanes]
        blocks.append(acc)
    acc = jnp.concatenate(blocks, axis=1)
    mu = jnp.mean(acc, axis=-1, keepdims=True)
    d = acc - mu
    var = jnp.mean(d * d, axis=-1, keepdims=True)
    return _silu(d * lax.rsqrt(var + EPS) * gln_ref[...] + bln_ref[...])


def _out_ffn_kernel(*refs, conv, nseq, halo, tiles_per_seq, final):
    it = iter(refs)
    x_ref, modm_ref, modf_ref, attn_ref = next(it), next(it), next(it), next(it)
    if conv:
        glu0_ref = next(it)
        next0_ref = next(it) if halo else None
        glun_ref = next(it)
        prevn_ref, nextn_ref = (next(it), next(it)) if halo else (None, None)
        conv_w = [next(it) for _ in range(4)]
        zero_ref = next(it)
    wout_ref, gff_ref, win_ref, wff_ref = next(it), next(it), next(it), next(it)
    gfin_ref = next(it) if final else None
    y_ref = next(it)
    i = pl.program_id(0)
    rows = TM // nseq

    if conv:
        pad_ref, g_ref = next(it), next(it)

        def conv_rows(c):
            s, r0 = divmod(c * CONV_ROWS, rows)
            g = _conv_chunk(pad_ref, s, r0, *conv_w)
            g_ref[c * CONV_ROWS:(c + 1) * CONV_ROWS, :] = g.astype(BF16)
            return jnp.max(g).astype(jnp.int32) & zero_ref[0]

        @pl.when(i == 0)
        def _():
            _fill_pad(pad_ref, glu0_ref, None, next0_ref, True, tiles_per_seq == 1, nseq)
            for c in range(TM // CONV_ROWS):
                conv_rows(c)

        out = (jnp.dot(attn_ref[...], wout_ref[:CONV_CH, :], preferred_element_type=F32)
               + jnp.dot(g_ref[...], wout_ref[CONV_CH:, :], preferred_element_type=F32))
        pos = (i + 1) % tiles_per_seq
        _fill_pad(pad_ref, glun_ref, prevn_ref, nextn_ref, pos == 0, pos == tiles_per_seq - 1, nseq)
        between = lambda j: conv_rows(j) if j < TM // CONV_ROWS else None
    else:
        out = jnp.dot(attn_ref[...], wout_ref[...], preferred_element_type=F32)
        between = None
    x = x_ref[...] + modm_ref[2:3, :] * out
    y = _ffn_body(x, modf_ref[...], gff_ref, win_ref, wff_ref, between)
    if final:
        y = _rms(y, gfin_ref[...])
    y_ref[...] = y


def _out_ffn(x, x_off, mod3, layer, row_fn, seq, attn, w_out, ffn_w, glu=None, conv_p=None, g_final=None):
    n = attn.shape[0]
    n_tiles = n // TM
    conv = glu is not None
    final = g_final is not None
    nseq = max(TM // seq, 1)
    tiles_per_seq = max(seq // TM, 1)
    halo = conv and tiles_per_seq > 1
    assert TM // CONV_ROWS <= D_FF // FF_CHUNK
    row = lambda w: pl.BlockSpec((TM, w), lambda i: (i, 0))
    in_specs = [pl.BlockSpec((TM, D_MODEL), lambda i: (i + x_off, 0)), _mod_spec(layer, 1, row_fn),
                _mod_spec(layer, 2, row_fn), row(attn.shape[1])]
    args = [x, mod3, mod3, attn]
    scratch = []
    if conv:
        per = TM // HALO
        last = n // HALO - 1
        nxt = lambda i: jnp.minimum(i + 1, n_tiles - 1)
        in_specs.append(pl.BlockSpec((TM, CONV_CH), lambda i: (0, 0)))
        args.append(glu)
        if halo:
            in_specs.append(pl.BlockSpec((HALO, CONV_CH), lambda i: (per, 0)))
            args.append(glu)
        in_specs.append(pl.BlockSpec((TM, CONV_CH), lambda i: (nxt(i), 0)))
        args.append(glu)
        if halo:
            in_specs += [pl.BlockSpec((HALO, CONV_CH), lambda i: (jnp.maximum(nxt(i) * per - 1, 0), 0)),
                         pl.BlockSpec((HALO, CONV_CH), lambda i: (jnp.minimum((nxt(i) + 1) * per, last), 0))]
            args += [glu, glu]
        in_specs += [_const_spec((CONV_W + 1, CONV_CH)), _const_spec((1, CONV_CH)), _const_spec((1, CONV_CH)),
                     _const_spec((1, CONV_CH)), pl.BlockSpec(memory_space=pltpu.SMEM)]
        args += [conv_p["w_dw"], conv_p["b_dw"], conv_p["g_ln"], conv_p["b_ln"], jnp.zeros((1,), jnp.int32)]
        scratch = [pltpu.VMEM((nseq, TM // nseq + 2 * HALO, CONV_CH), F32), pltpu.VMEM((TM, CONV_CH), BF16)]
    g_ff, w_in, w_ff = ffn_w
    in_specs += [_const_spec(w_out.shape)] + _ffn_weight_specs(layer)
    args += [w_out, g_ff.reshape(DEPTH, 1, D_MODEL), w_in, w_ff]
    if final:
        in_specs.append(_const_spec((1, D_MODEL)))
        args.append(g_final.reshape(1, D_MODEL))
    return pl.pallas_call(
        functools.partial(_out_ffn_kernel, conv=conv, nseq=nseq, halo=halo, tiles_per_seq=tiles_per_seq,
                          final=final),
        grid=(n_tiles,),
        in_specs=in_specs,
        out_specs=row(D_MODEL),
        out_shape=jax.ShapeDtypeStruct((n, D_MODEL), F32),
        scratch_shapes=scratch,
        compiler_params=pltpu.CompilerParams(dimension_semantics=("arbitrary",), vmem_limit_bytes=VMEM_LIMIT),
        name="out_ffn_conv" if conv else "out_ffn",
    )(*args)


def _rope_tables(d, offset, period):
    half = d // 2
    quarter = half // 2
    rows = DEC_SEQ // GRID_W
    pos_row = jnp.repeat(jnp.arange(rows), GRID_W)
    pos_col = jnp.tile(jnp.arange(GRID_W), rows)
    inv = ROPE_BASE ** (-jnp.arange(0, half, 2, dtype=F32) / half)
    ang_r = pos_row.astype(F32)[:, None] * inv[None, :]
    ang_c = pos_col.astype(F32)[:, None] * inv[None, :]
    zero = jnp.zeros((DEC_SEQ, quarter), F32)
    cos = jnp.concatenate([jnp.cos(ang_r), jnp.cos(ang_r), jnp.cos(ang_c), jnp.cos(ang_c)], axis=1)
    sa = jnp.concatenate([-jnp.sin(ang_r), zero, -jnp.sin(ang_c), zero], axis=1)
    sb = jnp.concatenate([zero, jnp.sin(ang_r), zero, jnp.sin(ang_c)], axis=1)

    def embed(t, fill):
        blk = jnp.concatenate([jnp.full((DEC_SEQ, offset), fill, F32), t,
                               jnp.full((DEC_SEQ, period - offset - d), fill, F32)], axis=1)
        return jnp.tile(blk, (1, LANES // period))

    return embed(cos, 1.0), embed(sa, 0.0), embed(sb, 0.0)


def _pad_heads(w, heads, dim):
    k = w.shape[0]
    return jnp.pad(w.reshape(k, heads, dim), ((0, 0), (0, 0), (0, HEAD_PAD - dim))).reshape(k, heads * HEAD_PAD)


def _prep_mla(i, g_mix_l, w_in_a, g_q_lora, w_q_up, g_kv_lora, w_kv_up, w_dw, b_dw, g_conv_ln, b_conv_ln, w_out_a):
    w = w_in_a[i]
    o1 = Q_LORA
    o2 = o1 + KV_LORA
    o3 = o2 + QK_ROPE
    zeros = lambda c: jnp.zeros((D_MODEL, c), F32)
    w1 = jnp.concatenate([w[:, :o2], w[:, o3:], zeros(KR_LANE), w[:, o2:o3],
                          zeros(LANES - KR_LANE - QK_ROPE)], axis=1)
    kvu = w_kv_up[i].reshape(KV_LORA, MLA_HEADS, QK_NOPE + V_HEAD)
    w_kn = _pad_heads(kvu[:, :, :QK_NOPE].reshape(KV_LORA, MLA_HEADS * QK_NOPE), MLA_HEADS, QK_NOPE)
    w_v = _pad_heads(kvu[:, :, QK_NOPE:].reshape(KV_LORA, MLA_HEADS * V_HEAD), MLA_HEADS, V_HEAD)
    return dict(
        g_mix=g_mix_l.reshape(1, D_MODEL),
        w1=w1.astype(BF16),
        g_q=g_q_lora[i].reshape(1, Q_LORA),
        w_qu=_pad_heads(w_q_up[i], MLA_HEADS, QK_NOPE + QK_ROPE).astype(BF16),
        g_kv=g_kv_lora[i].reshape(1, KV_LORA),
        w_kv=jnp.concatenate([w_kn, w_v], axis=1).astype(BF16),
        w_dw=jnp.pad(w_dw[i], ((0, 1), (0, 0))),
        b_dw=b_dw[i].reshape(1, CONV_CH),
        g_ln=g_conv_ln[i].reshape(1, CONV_CH),
        b_ln=b_conv_ln[i].reshape(1, CONV_CH),
        w_out=w_out_a[i].astype(BF16),
    )


def _prep_gqa(i, g_mix_l, w_in_c, g_q_head, g_k_head, w_out_c):
    w = w_in_c[i]
    v_pad = _pad_heads(w[:, GQA_Q + GQA_KV:], GQA_KV_HEADS, GQA_HEAD_DIM)
    return dict(
        g_mix=g_mix_l.reshape(1, D_MODEL),
        w=jnp.concatenate([w, v_pad], axis=1).astype(BF16),
        g_q=jnp.tile(g_q_head[i], GQA_HEADS).reshape(1, GQA_Q),
        g_k=jnp.tile(g_k_head[i], GQA_KV_HEADS).reshape(1, GQA_KV),
        w_out=w_out_c[i].astype(BF16),
    )


SAMPLE_HEADS_PER_STEP = 4
PROMPT_SEQS_PER_STEP = 2


def _mixer_ffn(x_all, x_off, n, mod3, l, row_fn, seq, p, ctx, tables, ffn_w, g_final):
    sample = ctx is not None
    if l % 2 == 0:
        outs = _mla_in(x_all, x_off, n, mod3, l, row_fn, seq, p, tables["mla"] if sample else None)
        q, kt, v, glu = outs[:4]
        segs = [(kt, v)]
        if sample:
            segs = [_mla_ctx(ctx["mla_ckv"], ctx["mla_krb"], p["w_kv"], ctx["past"])] + segs
            hps, bt = SAMPLE_HEADS_PER_STEP, 1
        else:
            hps, bt = MLA_HEADS, PROMPT_SEQS_PER_STEP
        attn = _attention(q, segs, seq=seq, heads=MLA_HEADS, kv_heads=MLA_HEADS, dq=HEAD_PAD, dk=HEAD_PAD,
                          dv=V_HEAD, bt=bt, heads_per_step=hps)
        return _out_ffn(x_all, x_off, mod3, l, row_fn, seq, attn, p["w_out"], ffn_w, glu=glu, conv_p=p,
                        g_final=g_final), outs[4:]
    outs = _gqa_in(x_all, x_off, n, mod3, l, row_fn, seq, p, tables["gqa"] if sample else None)
    q, kt, v = outs[:3]
    segs = [(kt, v)]
    if sample:
        segs = [_gqa_ctx(ctx["gqa_k"], ctx["gqa_vpad"], ctx["past"])] + segs
        hps, bt = SAMPLE_HEADS_PER_STEP, 1
    else:
        hps, bt = GQA_HEADS, PROMPT_SEQS_PER_STEP
    attn = _attention(q, segs, seq=seq, heads=GQA_HEADS, kv_heads=GQA_KV_HEADS, dq=GQA_HEAD_DIM,
                      dk=GQA_HEAD_DIM, dv=GQA_HEAD_DIM, bt=bt, heads_per_step=hps)
    return _out_ffn(x_all, x_off, mod3, l, row_fn, seq, attn, p["w_out"], ffn_w, g_final=g_final), outs[3:]


def kernel(x_prompt, x_sample, cache_mla_ckv, cache_mla_krope, cache_gqa_k, cache_gqa_v, c, c_ctx, g_ff1, w_ff1_in, w_ff1_out, g_mix, g_ff2, w_ff2_in, w_ff2_out, w_mod, b_mod, w_in_a, g_q_lora, w_q_up, g_kv_lora, w_kv_up, w_dw, b_dw, g_conv_ln, b_conv_ln, w_out_a, w_in_c, g_q_head, g_k_head, w_out_c, g_final):
    batch, seq, _ = x_prompt.shape
    dec_batch, dec_seq, _ = x_sample.shape
    past = cache_mla_ckv.shape[2]
    assert DEPTH == 2 and dec_seq == DEC_SEQ and 1 + dec_batch <= MOD_ROWS

    cond = jnp.concatenate([c_ctx[None, :], c, jnp.zeros((MOD_ROWS - 1 - dec_batch, D_MODEL), F32)], axis=0)
    mod3 = _modulation(cond, w_mod, b_mod)

    layers = [
        _prep_mla(0, g_mix[0], w_in_a, g_q_lora, w_q_up, g_kv_lora, w_kv_up, w_dw, b_dw, g_conv_ln, b_conv_ln,
                  w_out_a),
        _prep_gqa(0, g_mix[1], w_in_c, g_q_head, g_k_head, w_out_c),
    ]
    tables = dict(mla=_rope_tables(QK_ROPE, KR_LANE, HEAD_PAD), gqa=_rope_tables(GQA_HEAD_DIM, 0, GQA_HEAD_DIM))

    ctx = dict(
        past=past,
        mla_ckv=cache_mla_ckv[:, 0].reshape(dec_batch * past, KV_LORA),
        mla_krb=jnp.pad(cache_mla_krope[:, 0].reshape(dec_batch * past, QK_ROPE),
                        ((0, 0), (KR_LANE, LANES - KR_LANE - QK_ROPE))),
        gqa_k=cache_gqa_k[:, 0].reshape(dec_batch * past, GQA_KV),
        gqa_vpad=jnp.pad(cache_gqa_v[:, 0], ((0, 0), (0, 0), (0, 0), (0, HEAD_PAD - GQA_HEAD_DIM))
                         ).reshape(dec_batch * past, GQA_VPAD),
    )

    n_p, n_s = batch * seq, dec_batch * dec_seq
    nb_p = n_p // TM
    row_p = lambda i: 0
    row_s = lambda i: 1 + (i * TM) // DEC_SEQ
    row_all = lambda i: jnp.where(i < nb_p, 0, 1 + ((i - nb_p) * TM) // DEC_SEQ)

    cast_rows = 256
    ffn2_w = (g_ff2, _to_bf16(w_ff2_in, cast_rows), _to_bf16(w_ff2_out, cast_rows))

    parts = [x_prompt.reshape(n_p, D_MODEL), x_sample.reshape(n_s, D_MODEL)]
    saved = []
    for l in range(DEPTH):
        gf = g_final if l == DEPTH - 1 else None
        x_all = _ffn(parts, mod3, l, 0, row_all, g_ff1, w_ff1_in, w_ff1_out)
        xp, st = _mixer_ffn(x_all, 0, n_p, mod3, l, row_p, seq, layers[l], None, tables, ffn2_w, gf)
        xs, _ = _mixer_ffn(x_all, nb_p, n_s, mod3, l, row_s, dec_seq, layers[l], ctx, tables, ffn2_w, gf)
        saved.append(st)
        parts = [xp, xs]
    y_prompt, y_sample = parts
    (ckv_new, kr_new), (k_new, v_new) = saved

    return (y_prompt.reshape(batch, seq, D_MODEL),
            y_sample.reshape(dec_batch, dec_seq, D_MODEL),
            ckv_new.reshape(batch, 1, seq, KV_LORA),
            kr_new.reshape(batch, 1, seq, QK_ROPE),
            k_new.reshape(batch, 1, seq, GQA_KV_HEADS, GQA_HEAD_DIM),
            v_new.reshape(batch, 1, seq, GQA_KV_HEADS, GQA_HEAD_DIM))
```

```python
import functools

import jax
import jax.numpy as jnp
import numpy as np
from jax import lax
from jax.experimental import pallas as pl
from jax.experimental.pallas import tpu as pltpu

F32 = jnp.float32
BF16 = jnp.bfloat16

D_MODEL = 1024
DEPTH = 2
DEC_SEQ = 2048
GRID_W = 64
MLA_HEADS = 8
Q_LORA = 384
KV_LORA = 256
QK_NOPE = 64
QK_ROPE = 32
V_HEAD = 64
CONV_CH = 512
CONV_W = 31
CONV_PAD = CONV_W // 2
GQA_HEADS = 16
GQA_KV_HEADS = 4
GQA_HEAD_DIM = 64
D_FF = 2816
MACARON = 0.5
N_MOD = 9
ROPE_BASE = 10000.0
EPS = 1e-6

LANES = 128
HEAD_PAD = 128
MOD_ROWS = 8
DENOM_LANE = 64
LOG2E = 1.4426950408889634
HALO = 16
TM = 512
Q_SUB = 256
FF_CHUNK = 256
CONV_ROWS = 64
INTERLEAVE_SLACK = 2
VMEM_LIMIT = 56 * 1024 * 1024


def _params(n_axes):
    return pltpu.CompilerParams(dimension_semantics=("parallel",) * n_axes, vmem_limit_bytes=VMEM_LIMIT)


def _const_spec(shape):
    return pl.BlockSpec(shape, lambda *_: (0,) * len(shape), pipeline_mode=pl.Buffered(1))


def _rms(x, g):
    return x * lax.rsqrt(jnp.mean(x * x, axis=-1, keepdims=True) + EPS) * g


def _mod_norm(x, g, mod):
    return _rms(x, g) * (1.0 + mod[1:2]) + mod[0:1]


def _silu(x):
    return x * jax.nn.sigmoid(x)


def _rope(x, cos, sa, sb, quarter):
    w = x.shape[-1]
    return x * cos + pltpu.roll(x, w - quarter, 1) * sa + pltpu.roll(x, quarter, 1) * sb


def _with_ones_lane(v):
    lane = lax.broadcasted_iota(jnp.int32, v.shape, 1)
    return jnp.where(lane % HEAD_PAD == DENOM_LANE, 1.0, v)


def _tile_lanes(t, reps):
    return t if reps == 1 else jnp.concatenate([t] * reps, axis=1)


def _mod_kernel(c_ref, w_ref, b_ref, o_ref):
    e = _silu(c_ref[...]).astype(BF16)
    o_ref[...] = jnp.dot(e, w_ref[...].astype(BF16), preferred_element_type=F32) + b_ref[...]


def _modulation(cond, w_mod, b_mod):
    n_out = N_MOD * D_MODEL
    tn = 1536
    out = pl.pallas_call(
        _mod_kernel,
        grid=(DEPTH, n_out // tn),
        in_specs=[
            pl.BlockSpec((MOD_ROWS, D_MODEL), lambda l, j: (0, 0)),
            pl.BlockSpec((None, D_MODEL, tn), lambda l, j: (l, 0, j)),
            pl.BlockSpec((None, 1, tn), lambda l, j: (l, 0, j)),
        ],
        out_specs=pl.BlockSpec((None, MOD_ROWS, tn), lambda l, j: (l, 0, j)),
        out_shape=jax.ShapeDtypeStruct((DEPTH, MOD_ROWS, n_out), F32),
        compiler_params=_params(2),
        name="modulation",
    )(cond, w_mod, b_mod.reshape(DEPTH, 1, n_out))
    return out.reshape(DEPTH * MOD_ROWS * 3, 3, D_MODEL)


def _mod_spec(layer, sub, row_fn):
    return pl.BlockSpec((None, 3, D_MODEL), lambda i, *_: ((layer * MOD_ROWS + row_fn(i)) * 3 + sub, 0, 0))


def _cast_ffn2_slabs(wi_ref, wo_ref, wi_out, wo_out):
    wi_out[...] = wi_ref[...].astype(BF16)
    wo_out[...] = wo_ref[...].astype(BF16)


def _cast_ffn2_specs(layer, steps, w_in, w_out):
    shapes = [w_in.shape[1:], w_out.shape[1:]]
    assert all(k % (steps * 16) == 0 for k, _ in shapes)
    in_specs = [pl.BlockSpec((None, k // steps, n), lambda i: (layer, i, 0)) for k, n in shapes]
    out_specs = [pl.BlockSpec((k // steps, n), lambda i: (i, 0)) for k, n in shapes]
    return in_specs, out_specs, [jax.ShapeDtypeStruct(s, BF16) for s in shapes]


def _ffn_body(x, mod, g_ref, win_ref, wout_ref, between_chunks=None):
    h = _mod_norm(x, g_ref[...], mod).astype(BF16)
    acc = jnp.zeros(x.shape, F32)
    zeros = [None] * INTERLEAVE_SLACK
    for j in range(D_FF // FF_CHUNK):
        lo = j * FF_CHUNK
        a = jnp.dot(h, win_ref[:, lo:lo + FF_CHUNK].astype(BF16), preferred_element_type=F32)
        b = jnp.dot(h, win_ref[:, D_FF + lo:D_FF + lo + FF_CHUNK].astype(BF16), preferred_element_type=F32)
        act = (_silu(a) * b).astype(BF16)
        zero = zeros.pop(0)
        rows = pl.ds(lo, FF_CHUNK) if zero is None else pl.ds(pl.multiple_of(lo + zero, FF_CHUNK), FF_CHUNK)
        acc = acc + jnp.dot(act, wout_ref[rows, :].astype(BF16), preferred_element_type=F32)
        zeros.append(between_chunks(j) if between_chunks is not None else None)
    return x + (MACARON * mod[2:3]) * acc


def _ffn_kernel(*refs, nparts, nblk0):
    x_refs = refs[:nparts]
    mod_ref, g_ref, win_ref, wout_ref, o_ref = refs[nparts:]
    x = x_refs[0][...]
    if nparts == 2:
        x = jnp.where(pl.program_id(0) < nblk0, x, x_refs[1][...])
    o_ref[...] = _ffn_body(x, mod_ref[...], g_ref, win_ref, wout_ref)


def _ffn_weight_specs(layer):
    layer_spec = lambda shape: pl.BlockSpec((None,) + shape, lambda i: (layer, 0, 0), pipeline_mode=pl.Buffered(1))
    return [layer_spec((1, D_MODEL)), layer_spec((D_MODEL, 2 * D_FF)), layer_spec((D_FF, D_MODEL))]


def _ffn(parts, mod3, layer, sub, row_fn, g, w_in, w_out):
    nblks = [p.shape[0] // TM for p in parts]
    if len(parts) == 1:
        in_specs = [pl.BlockSpec((TM, D_MODEL), lambda i: (i, 0))]
    else:
        nb0 = nblks[0]
        in_specs = [pl.BlockSpec((TM, D_MODEL), lambda i: (jnp.minimum(i, nb0 - 1), 0)),
                    pl.BlockSpec((TM, D_MODEL), lambda i: (jnp.maximum(i - nb0, 0), 0))]
    in_specs += [_mod_spec(layer, sub, row_fn)] + _ffn_weight_specs(layer)
    n = sum(nblks) * TM
    return pl.pallas_call(
        functools.partial(_ffn_kernel, nparts=len(parts), nblk0=nblks[0]),
        grid=(n // TM,),
        in_specs=in_specs,
        out_specs=pl.BlockSpec((TM, D_MODEL), lambda i: (i, 0)),
        out_shape=jax.ShapeDtypeStruct((n, D_MODEL), F32),
        compiler_params=_params(1),
        name="ffn",
    )(*parts, mod3, g.reshape(DEPTH, 1, D_MODEL), w_in, w_out)


MLA_W1_COLS = Q_LORA + KV_LORA + 2 * CONV_CH + LANES
MLA_QK = MLA_HEADS * HEAD_PAD
KR_LANE = QK_NOPE


def _mla_kv(ckvn, krb, wkv_ref, kt_ref, v_ref, nseq):
    kv = jnp.dot(ckvn.astype(BF16), wkv_ref[...], preferred_element_type=F32)
    k = kv[:, :MLA_QK] + _tile_lanes(krb, MLA_HEADS)
    v_ref[...] = _with_ones_lane(kv[:, MLA_QK:]).astype(BF16)
    rows = k.shape[0] // nseq
    for s in range(nseq):
        kt_ref[s] = k[s * rows:(s + 1) * rows, :].astype(BF16).T


def _mla_in_kernel(x_ref, mod_ref, g_ref, w1_ref, gq_ref, wqu_ref, gkv_ref, wkv_ref, *rest, rope, nseq):
    if rope:
        cos_ref, sa_ref, sb_ref, q_ref, kt_ref, v_ref, glu_ref = rest
    else:
        wi_ref, wo_ref, q_ref, kt_ref, v_ref, glu_ref, ckv_ref, kr_ref, wi_out, wo_out = rest
        _cast_ffn2_slabs(wi_ref, wo_ref, wi_out, wo_out)
    h = _mod_norm(x_ref[...], g_ref[...], mod_ref[...]).astype(BF16)
    proj = jnp.dot(h, w1_ref[...], preferred_element_type=F32)
    o1 = Q_LORA
    o2 = o1 + KV_LORA
    o3 = o2 + CONV_CH
    o4 = o3 + CONV_CH
    cq, ckv, ua, ub, krb = proj[:, :o1], proj[:, o1:o2], proj[:, o2:o3], proj[:, o3:o4], proj[:, o4:]
    glu_ref[...] = ua * jax.nn.sigmoid(ub)
    q = jnp.dot(_rms(cq, gq_ref[...]).astype(BF16), wqu_ref[...], preferred_element_type=F32)
    ckvn = _rms(ckv, gkv_ref[...])
    if rope:
        cos, sa, sb = cos_ref[...], sa_ref[...], sb_ref[...]
        quarter = QK_ROPE // 4
        krb = _rope(krb, cos, sa, sb, quarter)
        q = _rope(q, _tile_lanes(cos, MLA_HEADS), _tile_lanes(sa, MLA_HEADS), _tile_lanes(sb, MLA_HEADS), quarter)
    else:
        ckv_ref[...] = ckvn
        kr_ref[...] = krb[:, KR_LANE:KR_LANE + QK_ROPE]
    q_ref[...] = (q * ((QK_NOPE + QK_ROPE) ** -0.5 * LOG2E)).astype(BF16)
    _mla_kv(ckvn, krb, wkv_ref, kt_ref, v_ref, nseq)


def _mla_ctx_kernel(ckv_ref, krb_ref, wkv_ref, kt_ref, v_ref):
    _mla_kv(ckv_ref[...], krb_ref[...], wkv_ref, kt_ref, v_ref, 1)


def _mla_in(x, x_off, n, mod3, layer, row_fn, seq, p, tables, ffn2_f32=None):
    rope = tables is not None
    nseq = max(TM // seq, 1)
    rows = TM // nseq
    batch = n // seq
    tiles_per_seq = max(seq // TM, 1)
    row = lambda w: pl.BlockSpec((TM, w), lambda i: (i, 0))
    in_specs = [pl.BlockSpec((TM, D_MODEL), lambda i: (i + x_off, 0)), _mod_spec(layer, 1, row_fn),
                _const_spec((1, D_MODEL)),
                _const_spec((D_MODEL, MLA_W1_COLS)), _const_spec((1, Q_LORA)), _const_spec((Q_LORA, MLA_QK)),
                _const_spec((1, KV_LORA)), _const_spec((KV_LORA, 2 * MLA_QK))]
    args = [x, mod3, p["g_mix"], p["w1"], p["g_q"], p["w_qu"], p["g_kv"], p["w_kv"]]
    kt_spec = pl.BlockSpec((nseq, MLA_QK, rows), lambda i: (i // tiles_per_seq, 0, i % tiles_per_seq))
    out_specs = [row(MLA_QK), kt_spec, row(MLA_QK), row(CONV_CH)]
    out_shape = [jax.ShapeDtypeStruct((n, MLA_QK), BF16), jax.ShapeDtypeStruct((batch, MLA_QK, seq), BF16),
                 jax.ShapeDtypeStruct((n, MLA_QK), BF16), jax.ShapeDtypeStruct((n, CONV_CH), F32)]
    if rope:
        tab = pl.BlockSpec((TM, LANES), lambda i: (i % tiles_per_seq, 0))
        in_specs += [tab, tab, tab]
        args += list(tables)
    else:
        out_specs += [row(KV_LORA), row(QK_ROPE)]
        out_shape += [jax.ShapeDtypeStruct((n, KV_LORA), F32), jax.ShapeDtypeStruct((n, QK_ROPE), F32)]
        cast_in, cast_out, cast_shape = _cast_ffn2_specs(layer, n // TM, *ffn2_f32)
        in_specs += cast_in
        args += list(ffn2_f32)
        out_specs += cast_out
        out_shape += cast_shape
    return pl.pallas_call(
        functools.partial(_mla_in_kernel, rope=rope, nseq=nseq),
        grid=(n // TM,),
        in_specs=in_specs,
        out_specs=out_specs,
        out_shape=out_shape,
        compiler_params=_params(1),
        name="mla_in_rope" if rope else "mla_in",
    )(*args)


def _mla_ctx(ckv, krb, w_kv, seq):
    n = ckv.shape[0]
    return pl.pallas_call(
        _mla_ctx_kernel,
        grid=(n // seq,),
        in_specs=[pl.BlockSpec((seq, KV_LORA), lambda i: (i, 0)), pl.BlockSpec((seq, LANES), lambda i: (i, 0)),
                  _const_spec((KV_LORA, 2 * MLA_QK))],
        out_specs=[pl.BlockSpec((1, MLA_QK, seq), lambda i: (i, 0, 0)), pl.BlockSpec((seq, MLA_QK), lambda i: (i, 0))],
        out_shape=[jax.ShapeDtypeStruct((n // seq, MLA_QK, seq), BF16), jax.ShapeDtypeStruct((n, MLA_QK), BF16)],
        compiler_params=_params(1),
        name="mla_ctx",
    )(ckv, krb, w_kv)


GQA_Q = GQA_HEADS * GQA_HEAD_DIM
GQA_KV = GQA_KV_HEADS * GQA_HEAD_DIM
GQA_VPAD = GQA_KV_HEADS * HEAD_PAD
GQA_W_COLS = GQA_Q + 2 * GQA_KV + GQA_VPAD


def _head_rms(x, g):
    tm, w = x.shape
    lo = lax.broadcasted_iota(jnp.int32, (tm, LANES), 1) < GQA_HEAD_DIM
    outs = []
    for b in range(w // LANES):
        xb = x[:, b * LANES:(b + 1) * LANES]
        sq = xb * xb
        s_lo = jnp.sum(jnp.where(lo, sq, 0.0), axis=-1, keepdims=True)
        s_hi = jnp.sum(jnp.where(lo, 0.0, sq), axis=-1, keepdims=True)
        ms = jnp.where(lo, s_lo, s_hi) * (1.0 / GQA_HEAD_DIM)
        outs.append(xb * lax.rsqrt(ms + EPS))
    return jnp.concatenate(outs, axis=1) * g


def _gqa_in_kernel(x_ref, mod_ref, g_ref, w_ref, gq_ref, gk_ref, *rest, rope, nseq):
    if rope:
        cos_ref, sa_ref, sb_ref, q_ref, kt_ref, v_ref = rest
    else:
        wi_ref, wo_ref, q_ref, kt_ref, v_ref, kc_ref, vc_ref, wi_out, wo_out = rest
        _cast_ffn2_slabs(wi_ref, wo_ref, wi_out, wo_out)
    h = _mod_norm(x_ref[...], g_ref[...], mod_ref[...]).astype(BF16)
    proj = jnp.dot(h, w_ref[...], preferred_element_type=F32)
    q = _head_rms(proj[:, :GQA_Q], gq_ref[...])
    k = _head_rms(proj[:, GQA_Q:GQA_Q + GQA_KV], gk_ref[...])
    if rope:
        cos, sa, sb = cos_ref[...], sa_ref[...], sb_ref[...]
        quarter = GQA_HEAD_DIM // 4
        q = _rope(q, _tile_lanes(cos, GQA_Q // LANES), _tile_lanes(sa, GQA_Q // LANES),
                  _tile_lanes(sb, GQA_Q // LANES), quarter)
        k = _rope(k, _tile_lanes(cos, GQA_KV // LANES), _tile_lanes(sa, GQA_KV // LANES),
                  _tile_lanes(sb, GQA_KV // LANES), quarter)
    else:
        kc_ref[...] = k
        vc_ref[...] = proj[:, GQA_Q + GQA_KV:GQA_Q + 2 * GQA_KV]
    q_ref[...] = (q * (GQA_HEAD_DIM ** -0.5 * LOG2E)).astype(BF16)
    v_ref[...] = _with_ones_lane(proj[:, GQA_Q + 2 * GQA_KV:]).astype(BF16)
    rows = k.shape[0] // nseq
    for s in range(nseq):
        kt_ref[s] = k[s * rows:(s + 1) * rows, :].astype(BF16).T


def _gqa_ctx_kernel(k_ref, v_ref, kt_ref, vo_ref):
    kt_ref[0] = k_ref[...].astype(BF16).T
    vo_ref[...] = _with_ones_lane(v_ref[...]).astype(BF16)


def _gqa_in(x, x_off, n, mod3, layer, row_fn, seq, p, tables, ffn2_f32=None):
    rope = tables is not None
    nseq = max(TM // seq, 1)
    rows = TM // nseq
    batch = n // seq
    tiles_per_seq = max(seq // TM, 1)
    row = lambda w: pl.BlockSpec((TM, w), lambda i: (i, 0))
    in_specs = [pl.BlockSpec((TM, D_MODEL), lambda i: (i + x_off, 0)), _mod_spec(layer, 1, row_fn),
                _const_spec((1, D_MODEL)),
                _const_spec((D_MODEL, GQA_W_COLS)), _const_spec((1, GQA_Q)), _const_spec((1, GQA_KV))]
    args = [x, mod3, p["g_mix"], p["w"], p["g_q"], p["g_k"]]
    kt_spec = pl.BlockSpec((nseq, GQA_KV, rows), lambda i: (i // tiles_per_seq, 0, i % tiles_per_seq))
    out_specs = [row(GQA_Q), kt_spec, row(GQA_VPAD)]
    out_shape = [jax.ShapeDtypeStruct((n, GQA_Q), BF16), jax.ShapeDtypeStruct((batch, GQA_KV, seq), BF16),
                 jax.ShapeDtypeStruct((n, GQA_VPAD), BF16)]
    if rope:
        tab = pl.BlockSpec((TM, LANES), lambda i: (i % tiles_per_seq, 0))
        in_specs += [tab, tab, tab]
        args += list(tables)
    else:
        out_specs += [row(GQA_KV), row(GQA_KV)]
        out_shape += [jax.ShapeDtypeStruct((n, GQA_KV), F32), jax.ShapeDtypeStruct((n, GQA_KV), F32)]
        cast_in, cast_out, cast_shape = _cast_ffn2_specs(layer, n // TM, *ffn2_f32)
        in_specs += cast_in
        args += list(ffn2_f32)
        out_specs += cast_out
        out_shape += cast_shape
    return pl.pallas_call(
        functools.partial(_gqa_in_kernel, rope=rope, nseq=nseq),
        grid=(n // TM,),
        in_specs=in_specs,
        out_specs=out_specs,
        out_shape=out_shape,
        compiler_params=_params(1),
        name="gqa_in_rope" if rope else "gqa_in",
    )(*args)


def _gqa_ctx(k, vpad, seq):
    n = k.shape[0]
    return pl.pallas_call(
        _gqa_ctx_kernel,
        grid=(n // seq,),
        in_specs=[pl.BlockSpec((seq, GQA_KV), lambda i: (i, 0)), pl.BlockSpec((seq, GQA_VPAD), lambda i: (i, 0))],
        out_specs=[pl.BlockSpec((1, GQA_KV, seq), lambda i: (i, 0, 0)), pl.BlockSpec((seq, GQA_VPAD), lambda i: (i, 0))],
        out_shape=[jax.ShapeDtypeStruct((n // seq, GQA_KV, seq), BF16), jax.ShapeDtypeStruct((n, GQA_VPAD), BF16)],
        compiler_params=_params(1),
        name="gqa_ctx",
    )(k, vpad)


def _attn_kernel(*refs, nseg, bt, seq, nq, nkv, dq, dk, dv):
    q_ref = refs[0]
    segs = [(refs[1 + 2 * s], refs[2 + 2 * s]) for s in range(nseg)]
    o_ref = refs[1 + 2 * nseg]
    s_refs = refs[2 + 2 * nseg:4 + 2 * nseg]
    nks = [kt_ref.shape[2] for kt_ref, _ in segs]
    offs = [sum(nks[:s]) for s in range(nseg)]

    def scores(bi, r0, s_ref):
        rows = pl.ds(bi * seq + r0, Q_SUB)
        for j in range(nq):
            kv = j * nkv // nq
            qs = q_ref[rows, j * dq:(j + 1) * dq]
            for (kt_ref, _), off, nk in zip(segs, offs, nks):
                s_ref[j, :, off:off + nk] = jnp.dot(qs, kt_ref[bi, kv * dk:(kv + 1) * dk, :],
                                                    preferred_element_type=F32)

    def softmax_pv(bi, r0, s_ref):
        rows = pl.ds(bi * seq + r0, Q_SUB)
        outs = []
        for j in range(nq):
            kv = j * nkv // nq
            s = s_ref[j]
            p = jnp.exp2(s - jnp.max(s, axis=-1, keepdims=True)).astype(BF16)
            o = None
            for (_, v_ref), off, nk in zip(segs, offs, nks):
                pv = jnp.dot(p[:, off:off + nk], v_ref[bi * nk:(bi + 1) * nk, kv * HEAD_PAD:(kv + 1) * HEAD_PAD],
                             preferred_element_type=F32)
                o = pv if o is None else o + pv
            outs.append(o[:, :dv] / o[:, DENOM_LANE:DENOM_LANE + 1])
        o_ref[rows, :] = jnp.concatenate(outs, axis=1).astype(BF16)

    if seq == Q_SUB:
        assert bt == len(s_refs)
        for bi in range(bt):
            scores(bi, 0, s_refs[bi])
        for bi in range(bt):
            softmax_pv(bi, 0, s_refs[bi])
    else:
        n_sub = seq // Q_SUB
        assert bt == 1 and n_sub % 2 == 0
        sub = lambda r: pl.multiple_of(r * Q_SUB, Q_SUB)
        scores(0, 0, s_refs[0])

        def body(t, carry):
            scores(0, sub(2 * t + 1), s_refs[1])
            softmax_pv(0, sub(2 * t), s_refs[0])
            scores(0, sub(2 * t + 2), s_refs[0])
            softmax_pv(0, sub(2 * t + 1), s_refs[1])
            return carry
        lax.fori_loop(0, n_sub // 2 - 1, body, 0)
        scores(0, (n_sub - 1) * Q_SUB, s_refs[1])
        softmax_pv(0, (n_sub - 2) * Q_SUB, s_refs[0])
        softmax_pv(0, (n_sub - 1) * Q_SUB, s_refs[1])


def _attention(q, segs, *, seq, heads, kv_heads, dq, dk, dv, bt, heads_per_step):
    n = q.shape[0]
    batch = n // seq
    nq = heads_per_step
    steps = heads // nq
    nkv = max(kv_heads // steps, 1)
    q_per_kv_block = steps // (kv_heads // nkv)
    in_specs = [pl.BlockSpec((bt * seq, nq * dq), lambda b, p: (b, p))]
    args = [q]
    for kt, v in segs:
        nk = kt.shape[2]
        in_specs.append(pl.BlockSpec((bt, nkv * dk, nk), lambda b, p: (b, p // q_per_kv_block, 0)))
        in_specs.append(pl.BlockSpec((bt * nk, nkv * HEAD_PAD), lambda b, p: (b, p // q_per_kv_block)))
        args += [kt, v]
    return pl.pallas_call(
        functools.partial(_attn_kernel, nseg=len(segs), bt=bt, seq=seq, nq=nq, nkv=nkv, dq=dq, dk=dk, dv=dv),
        grid=(batch // bt, steps),
        in_specs=in_specs,
        out_specs=pl.BlockSpec((bt * seq, nq * dv), lambda b, p: (b, p)),
        out_shape=jax.ShapeDtypeStruct((n, heads * dv), BF16),
        scratch_shapes=[pltpu.VMEM((nq, Q_SUB, sum(kt.shape[2] for kt, _ in segs)), F32)] * 2,
        compiler_params=_params(2),
        name="attention",
    )(*args)


SUBLANES = 8
CONV_LEAD = HALO - CONV_PAD
CONV_WIN = CONV_ROWS + 2 * HALO


def _fill_pad(pad_ref, glu_ref, prev_ref, next_ref, first, last, nseq):
    rows = TM // nseq
    zeros = jnp.zeros((HALO, CONV_CH), F32)
    for s in range(nseq):
        pad_ref[s, 0:HALO, :] = zeros if prev_ref is None else jnp.where(first, zeros, prev_ref[...])
        pad_ref[s, HALO + rows:2 * HALO + rows, :] = (
            zeros if next_ref is None else jnp.where(last, zeros, next_ref[...]))
        pad_ref[s, HALO:HALO + rows, :] = glu_ref[s * rows:(s + 1) * rows, :]


def _conv_chunk(pad_ref, s, r0, wdw_ref, bdw_ref, gln_ref, bln_ref):
    blocks = []
    for lb in range(CONV_CH // LANES):
        lanes = slice(lb * LANES, (lb + 1) * LANES)
        win = pad_ref[s, pl.ds(r0, CONV_WIN), lanes]
        acc = jnp.broadcast_to(bdw_ref[:, lanes], (CONV_ROWS, LANES))
        for phase in range(SUBLANES):
            shifted = win if phase == 0 else pltpu.roll(win, CONV_WIN - phase, 0)
            for a in range((CONV_W + CONV_LEAD) // SUBLANES + 1):
                k = a * SUBLANES + phase - CONV_LEAD
                if 0 <= k < CONV_W:
                    acc = acc + shifted[a * SUBLANES:a * SUBLANES + CONV_ROWS] * wdw_ref[k:k + 1, lanes]
        blocks.append(acc)
    acc = jnp.concatenate(blocks, axis=1)
    mu = jnp.mean(acc, axis=-1, keepdims=True)
    d = acc - mu
    var = jnp.mean(d * d, axis=-1, keepdims=True)
    return _silu(d * lax.rsqrt(var + EPS) * gln_ref[...] + bln_ref[...])


def _out_ffn_kernel(*refs, conv, nseq, halo, tiles_per_seq, final):
    it = iter(refs)
    x_ref, modm_ref, modf_ref, attn_ref = next(it), next(it), next(it), next(it)
    if conv:
        glu0_ref = next(it)
        next0_ref = next(it) if halo else None
        glun_ref = next(it)
        prevn_ref, nextn_ref = (next(it), next(it)) if halo else (None, None)
        conv_w = [next(it) for _ in range(4)]
        zero_ref = next(it)
    wout_ref, gff_ref, win_ref, wff_ref = next(it), next(it), next(it), next(it)
    gfin_ref = next(it) if final else None
    y_ref = next(it)
    i = pl.program_id(0)
    rows = TM // nseq

    if conv:
        pad_ref, g_ref = next(it), next(it)

        def conv_rows(c):
            s, r0 = divmod(c * CONV_ROWS, rows)
            g = _conv_chunk(pad_ref, s, r0, *conv_w)
            g_ref[c * CONV_ROWS:(c + 1) * CONV_ROWS, :] = g.astype(BF16)
            return jnp.max(g).astype(jnp.int32) & zero_ref[0]

        @pl.when(i == 0)
        def _():
            _fill_pad(pad_ref, glu0_ref, None, next0_ref, True, tiles_per_seq == 1, nseq)
            for c in range(TM // CONV_ROWS):
                conv_rows(c)

        out = (jnp.dot(attn_ref[...], wout_ref[:CONV_CH, :], preferred_element_type=F32)
               + jnp.dot(g_ref[...], wout_ref[CONV_CH:, :], preferred_element_type=F32))
        pos = (i + 1) % tiles_per_seq
        _fill_pad(pad_ref, glun_ref, prevn_ref, nextn_ref, pos == 0, pos == tiles_per_seq - 1, nseq)
        between = lambda j: conv_rows(j) if j < TM // CONV_ROWS else None
    else:
        out = jnp.dot(attn_ref[...], wout_ref[...], preferred_element_type=F32)
        between = None
    x = x_ref[...] + modm_ref[2:3, :] * out
    y = _ffn_body(x, modf_ref[...], gff_ref, win_ref, wff_ref, between)
    if final:
        y = _rms(y, gfin_ref[...])
    y_ref[...] = y


def _out_ffn(x, x_off, mod3, layer, row_fn, seq, attn, w_out, ffn_w, glu=None, conv_p=None, g_final=None):
    n = attn.shape[0]
    n_tiles = n // TM
    conv = glu is not None
    final = g_final is not None
    nseq = max(TM // seq, 1)
    tiles_per_seq = max(seq // TM, 1)
    halo = conv and tiles_per_seq > 1
    assert TM // CONV_ROWS <= D_FF // FF_CHUNK
    row = lambda w: pl.BlockSpec((TM, w), lambda i: (i, 0))
    in_specs = [pl.BlockSpec((TM, D_MODEL), lambda i: (i + x_off, 0)), _mod_spec(layer, 1, row_fn),
                _mod_spec(layer, 2, row_fn), row(attn.shape[1])]
    args = [x, mod3, mod3, attn]
    scratch = []
    if conv:
        per = TM // HALO
        last = n // HALO - 1
        nxt = lambda i: jnp.minimum(i + 1, n_tiles - 1)
        in_specs.append(pl.BlockSpec((TM, CONV_CH), lambda i: (0, 0)))
        args.append(glu)
        if halo:
            in_specs.append(pl.BlockSpec((HALO, CONV_CH), lambda i: (per, 0)))
            args.append(glu)
        in_specs.append(pl.BlockSpec((TM, CONV_CH), lambda i: (nxt(i), 0)))
        args.append(glu)
        if halo:
            in_specs += [pl.BlockSpec((HALO, CONV_CH), lambda i: (jnp.maximum(nxt(i) * per - 1, 0), 0)),
                         pl.BlockSpec((HALO, CONV_CH), lambda i: (jnp.minimum((nxt(i) + 1) * per, last), 0))]
            args += [glu, glu]
        in_specs += [_const_spec((CONV_W + 1, CONV_CH)), _const_spec((1, CONV_CH)), _const_spec((1, CONV_CH)),
                     _const_spec((1, CONV_CH)), pl.BlockSpec(memory_space=pltpu.SMEM)]
        args += [conv_p["w_dw"], conv_p["b_dw"], conv_p["g_ln"], conv_p["b_ln"], jnp.zeros((1,), jnp.int32)]
        scratch = [pltpu.VMEM((nseq, TM // nseq + 2 * HALO, CONV_CH), F32), pltpu.VMEM((TM, CONV_CH), BF16)]
    g_ff, w_in, w_ff = ffn_w
    in_specs += [_const_spec(w_out.shape), _ffn_weight_specs(layer)[0], _const_spec(w_in.shape),
                 _const_spec(w_ff.shape)]
    args += [w_out, g_ff.reshape(DEPTH, 1, D_MODEL), w_in, w_ff]
    if final:
        in_specs.append(_const_spec((1, D_MODEL)))
        args.append(g_final.reshape(1, D_MODEL))
    return pl.pallas_call(
        functools.partial(_out_ffn_kernel, conv=conv, nseq=nseq, halo=halo, tiles_per_seq=tiles_per_seq,
                          final=final),
        grid=(n_tiles,),
        in_specs=in_specs,
        out_specs=row(D_MODEL),
        out_shape=jax.ShapeDtypeStruct((n, D_MODEL), F32),
        scratch_shapes=scratch,
        compiler_params=pltpu.CompilerParams(dimension_semantics=("arbitrary",), vmem_limit_bytes=VMEM_LIMIT),
        name="out_ffn_conv" if conv else "out_ffn",
    )(*args)


def _rope_tables(d, offset, period):
    half = d // 2
    quarter = half // 2
    rows = DEC_SEQ // GRID_W
    pos_row = np.repeat(np.arange(rows), GRID_W)
    pos_col = np.tile(np.arange(GRID_W), rows)
    inv = ROPE_BASE ** (-np.arange(0, half, 2, dtype=np.float64) / half)
    ang_r = pos_row.astype(np.float64)[:, None] * inv[None, :]
    ang_c = pos_col.astype(np.float64)[:, None] * inv[None, :]
    zero = np.zeros((DEC_SEQ, quarter))
    cos = np.concatenate([np.cos(ang_r), np.cos(ang_r), np.cos(ang_c), np.cos(ang_c)], axis=1)
    sa = np.concatenate([-np.sin(ang_r), zero, -np.sin(ang_c), zero], axis=1)
    sb = np.concatenate([zero, np.sin(ang_r), zero, np.sin(ang_c)], axis=1)

    def embed(t, fill):
        blk = np.concatenate([np.full((DEC_SEQ, offset), fill), t,
                              np.full((DEC_SEQ, period - offset - d), fill)], axis=1)
        return jnp.asarray(np.tile(blk, (1, LANES // period)), dtype=F32)

    return embed(cos, 1.0), embed(sa, 0.0), embed(sb, 0.0)


def _pad_heads(w, heads, dim):
    k = w.shape[0]
    return jnp.pad(w.reshape(k, heads, dim), ((0, 0), (0, 0), (0, HEAD_PAD - dim))).reshape(k, heads * HEAD_PAD)


def _prep_mla(i, g_mix_l, w_in_a, g_q_lora, w_q_up, g_kv_lora, w_kv_up, w_dw, b_dw, g_conv_ln, b_conv_ln, w_out_a):
    w = w_in_a[i]
    o1 = Q_LORA
    o2 = o1 + KV_LORA
    o3 = o2 + QK_ROPE
    zeros = lambda c: jnp.zeros((D_MODEL, c), F32)
    w1 = jnp.concatenate([w[:, :o2], w[:, o3:], zeros(KR_LANE), w[:, o2:o3],
                          zeros(LANES - KR_LANE - QK_ROPE)], axis=1)
    kvu = w_kv_up[i].reshape(KV_LORA, MLA_HEADS, QK_NOPE + V_HEAD)
    w_kn = _pad_heads(kvu[:, :, :QK_NOPE].reshape(KV_LORA, MLA_HEADS * QK_NOPE), MLA_HEADS, QK_NOPE)
    w_v = _pad_heads(kvu[:, :, QK_NOPE:].reshape(KV_LORA, MLA_HEADS * V_HEAD), MLA_HEADS, V_HEAD)
    return dict(
        g_mix=g_mix_l.reshape(1, D_MODEL),
        w1=w1.astype(BF16),
        g_q=g_q_lora[i].reshape(1, Q_LORA),
        w_qu=_pad_heads(w_q_up[i], MLA_HEADS, QK_NOPE + QK_ROPE).astype(BF16),
        g_kv=g_kv_lora[i].reshape(1, KV_LORA),
        w_kv=jnp.concatenate([w_kn, w_v], axis=1).astype(BF16),
        w_dw=jnp.pad(w_dw[i], ((0, 1), (0, 0))),
        b_dw=b_dw[i].reshape(1, CONV_CH),
        g_ln=g_conv_ln[i].reshape(1, CONV_CH),
        b_ln=b_conv_ln[i].reshape(1, CONV_CH),
        w_out=w_out_a[i].astype(BF16),
    )


def _prep_gqa(i, g_mix_l, w_in_c, g_q_head, g_k_head, w_out_c):
    w = w_in_c[i]
    v_pad = _pad_heads(w[:, GQA_Q + GQA_KV:], GQA_KV_HEADS, GQA_HEAD_DIM)
    return dict(
        g_mix=g_mix_l.reshape(1, D_MODEL),
        w=jnp.concatenate([w, v_pad], axis=1).astype(BF16),
        g_q=jnp.tile(g_q_head[i], GQA_HEADS).reshape(1, GQA_Q),
        g_k=jnp.tile(g_k_head[i], GQA_KV_HEADS).reshape(1, GQA_KV),
        w_out=w_out_c[i].astype(BF16),
    )


SAMPLE_HEADS_PER_STEP = 4
PROMPT_SEQS_PER_STEP = 2


def _mixer_ffn(x_all, x_off, n, mod3, l, row_fn, seq, p, ctx, tables, g_ff2, ffn2_w, g_final):
    sample = ctx is not None
    in_proj, n_core = (_mla_in, 4) if l % 2 == 0 else (_gqa_in, 3)
    outs = in_proj(x_all, x_off, n, mod3, l, row_fn, seq, p, tables["mla" if l % 2 == 0 else "gqa"] if sample else None,
                   None if sample else ffn2_w)
    if not sample:
        outs, ffn2_w = outs[:-2], tuple(outs[-2:])
    ffn_w = (g_ff2,) + tuple(ffn2_w)
    if l % 2 == 0:
        q, kt, v, glu = outs[:4]
        segs = [(kt, v)]
        if sample:
            segs = [_mla_ctx(ctx["mla_ckv"], ctx["mla_krb"], p["w_kv"], ctx["past"])] + segs
            hps, bt = SAMPLE_HEADS_PER_STEP, 1
        else:
            hps, bt = MLA_HEADS, PROMPT_SEQS_PER_STEP
        attn = _attention(q, segs, seq=seq, heads=MLA_HEADS, kv_heads=MLA_HEADS, dq=HEAD_PAD, dk=HEAD_PAD,
                          dv=V_HEAD, bt=bt, heads_per_step=hps)
        return _out_ffn(x_all, x_off, mod3, l, row_fn, seq, attn, p["w_out"], ffn_w, glu=glu, conv_p=p,
                        g_final=g_final), outs[n_core:], ffn2_w
    q, kt, v = outs[:3]
    segs = [(kt, v)]
    if sample:
        segs = [_gqa_ctx(ctx["gqa_k"], ctx["gqa_vpad"], ctx["past"])] + segs
        hps, bt = SAMPLE_HEADS_PER_STEP, 1
    else:
        hps, bt = GQA_HEADS, PROMPT_SEQS_PER_STEP
    attn = _attention(q, segs, seq=seq, heads=GQA_HEADS, kv_heads=GQA_KV_HEADS, dq=GQA_HEAD_DIM,
                      dk=GQA_HEAD_DIM, dv=GQA_HEAD_DIM, bt=bt, heads_per_step=hps)
    return (_out_ffn(x_all, x_off, mod3, l, row_fn, seq, attn, p["w_out"], ffn_w, g_final=g_final), outs[n_core:],
            ffn2_w)


def kernel(x_prompt, x_sample, cache_mla_ckv, cache_mla_krope, cache_gqa_k, cache_gqa_v, c, c_ctx, g_ff1, w_ff1_in, w_ff1_out, g_mix, g_ff2, w_ff2_in, w_ff2_out, w_mod, b_mod, w_in_a, g_q_lora, w_q_up, g_kv_lora, w_kv_up, w_dw, b_dw, g_conv_ln, b_conv_ln, w_out_a, w_in_c, g_q_head, g_k_head, w_out_c, g_final):
    batch, seq, _ = x_prompt.shape
    dec_batch, dec_seq, _ = x_sample.shape
    past = cache_mla_ckv.shape[2]
    assert DEPTH == 2 and dec_seq == DEC_SEQ and 1 + dec_batch <= MOD_ROWS

    cond = jnp.concatenate([c_ctx[None, :], c, jnp.zeros((MOD_ROWS - 1 - dec_batch, D_MODEL), F32)], axis=0)
    mod3 = _modulation(cond, w_mod, b_mod)

    layers = [
        _prep_mla(0, g_mix[0], w_in_a, g_q_lora, w_q_up, g_kv_lora, w_kv_up, w_dw, b_dw, g_conv_ln, b_conv_ln,
                  w_out_a),
        _prep_gqa(0, g_mix[1], w_in_c, g_q_head, g_k_head, w_out_c),
    ]
    tables = dict(mla=_rope_tables(QK_ROPE, KR_LANE, HEAD_PAD), gqa=_rope_tables(GQA_HEAD_DIM, 0, GQA_HEAD_DIM))

    ctx = dict(
        past=past,
        mla_ckv=cache_mla_ckv[:, 0].reshape(dec_batch * past, KV_LORA),
        mla_krb=jnp.pad(cache_mla_krope[:, 0].reshape(dec_batch * past, QK_ROPE),
                        ((0, 0), (KR_LANE, LANES - KR_LANE - QK_ROPE))),
        gqa_k=cache_gqa_k[:, 0].reshape(dec_batch * past, GQA_KV),
        gqa_vpad=jnp.pad(cache_gqa_v[:, 0], ((0, 0), (0, 0), (0, 0), (0, HEAD_PAD - GQA_HEAD_DIM))
                         ).reshape(dec_batch * past, GQA_VPAD),
    )

    n_p, n_s = batch * seq, dec_batch * dec_seq
    nb_p = n_p // TM
    row_p = lambda i: 0
    row_s = lambda i: 1 + (i * TM) // DEC_SEQ
    row_all = lambda i: jnp.where(i < nb_p, 0, 1 + ((i - nb_p) * TM) // DEC_SEQ)

    parts = [x_prompt.reshape(n_p, D_MODEL), x_sample.reshape(n_s, D_MODEL)]
    saved = []
    for l in range(DEPTH):
        gf = g_final if l == DEPTH - 1 else None
        x_all = _ffn(parts, mod3, l, 0, row_all, g_ff1, w_ff1_in, w_ff1_out)
        xp, st, w2 = _mixer_ffn(x_all, 0, n_p, mod3, l, row_p, seq, layers[l], None, tables, g_ff2,
                                (w_ff2_in, w_ff2_out), gf)
        xs, _, _ = _mixer_ffn(x_all, nb_p, n_s, mod3, l, row_s, dec_seq, layers[l], ctx, tables, g_ff2, w2, gf)
        saved.append(st)
        parts = [xp, xs]
    y_prompt, y_sample = parts
    (ckv_new, kr_new), (k_new, v_new) = saved

    return (y_prompt.reshape(batch, seq, D_MODEL),
            y_sample.reshape(dec_batch, dec_seq, D_MODEL),
            ckv_new.reshape(batch, 1, seq, KV_LORA),
            kr_new.reshape(batch, 1, seq, QK_ROPE),
            k_new.reshape(batch, 1, seq, GQA_KV_HEADS, GQA_HEAD_DIM),
            v_new.reshape(batch, 1, seq, GQA_KV_HEADS, GQA_HEAD_DIM))
```

```python
import functools

import jax
import jax.numpy as jnp
import numpy as np
from jax import lax
from jax.experimental import pallas as pl
from jax.experimental.pallas import tpu as pltpu

F32 = jnp.float32
BF16 = jnp.bfloat16

D_MODEL = 1024
DEPTH = 2
DEC_SEQ = 2048
GRID_W = 64
MLA_HEADS = 8
Q_LORA = 384
KV_LORA = 256
QK_NOPE = 64
QK_ROPE = 32
V_HEAD = 64
CONV_CH = 512
CONV_W = 31
CONV_PAD = CONV_W // 2
GQA_HEADS = 16
GQA_KV_HEADS = 4
GQA_HEAD_DIM = 64
D_FF = 2816
MACARON = 0.5
N_MOD = 9
ROPE_BASE = 10000.0
EPS = 1e-6

LANES = 128
HEAD_PAD = 128
MOD_ROWS = 8
DENOM_LANE = 64
LOG2E = 1.4426950408889634
HALO = 16
TM = 512
Q_SUB = 256
FF_CHUNK = 256
CONV_ROWS = 64
INTERLEAVE_SLACK = 2
VMEM_LIMIT = 56 * 1024 * 1024


def _params(n_axes):
    return pltpu.CompilerParams(dimension_semantics=("parallel",) * n_axes, vmem_limit_bytes=VMEM_LIMIT)


def _const_spec(shape):
    return pl.BlockSpec(shape, lambda *_: (0,) * len(shape), pipeline_mode=pl.Buffered(1))


def _rms(x, g):
    return x * lax.rsqrt(jnp.mean(x * x, axis=-1, keepdims=True) + EPS) * g


def _mod_norm(x, g, mod):
    return _rms(x, g) * (1.0 + mod[1:2]) + mod[0:1]


def _silu(x):
    return x * jax.nn.sigmoid(x)


def _rope(x, cos, sa, sb, quarter):
    w = x.shape[-1]
    return x * cos + pltpu.roll(x, w - quarter, 1) * sa + pltpu.roll(x, quarter, 1) * sb


def _with_ones_lane(v):
    lane = lax.broadcasted_iota(jnp.int32, v.shape, 1)
    return jnp.where(lane % HEAD_PAD == DENOM_LANE, 1.0, v)


def _tile_lanes(t, reps):
    return t if reps == 1 else jnp.concatenate([t] * reps, axis=1)


def _mod_kernel(c_ref, w_ref, b_ref, o_ref):
    e = _silu(c_ref[...]).astype(BF16)
    o_ref[...] = jnp.dot(e, w_ref[...].astype(BF16), preferred_element_type=F32) + b_ref[...]


def _modulation(cond, w_mod, b_mod):
    n_out = N_MOD * D_MODEL
    tn = 1536
    out = pl.pallas_call(
        _mod_kernel,
        grid=(DEPTH, n_out // tn),
        in_specs=[
            pl.BlockSpec((MOD_ROWS, D_MODEL), lambda l, j: (0, 0)),
            pl.BlockSpec((None, D_MODEL, tn), lambda l, j: (l, 0, j)),
            pl.BlockSpec((None, 1, tn), lambda l, j: (l, 0, j)),
        ],
        out_specs=pl.BlockSpec((None, MOD_ROWS, tn), lambda l, j: (l, 0, j)),
        out_shape=jax.ShapeDtypeStruct((DEPTH, MOD_ROWS, n_out), F32),
        compiler_params=_params(2),
        name="modulation",
    )(cond, w_mod, b_mod.reshape(DEPTH, 1, n_out))
    return out.reshape(DEPTH * MOD_ROWS * 3, 3, D_MODEL)


def _mod_spec(layer, sub, row_fn):
    return pl.BlockSpec((None, 3, D_MODEL), lambda i, *_: ((layer * MOD_ROWS + row_fn(i)) * 3 + sub, 0, 0))


def _cast_ffn2_slabs(wi_ref, wo_ref, wi_out, wo_out):
    wi_out[...] = wi_ref[...].astype(BF16)
    wo_out[...] = wo_ref[...].astype(BF16)


def _cast_ffn2_specs(layer, steps, w_in, w_out):
    shapes = [w_in.shape[1:], w_out.shape[1:]]
    assert all(k % (steps * 16) == 0 for k, _ in shapes)
    in_specs = [pl.BlockSpec((None, k // steps, n), lambda i: (layer, i, 0)) for k, n in shapes]
    out_specs = [pl.BlockSpec((k // steps, n), lambda i: (i, 0)) for k, n in shapes]
    return in_specs, out_specs, [jax.ShapeDtypeStruct(s, BF16) for s in shapes]


def _ffn_body(x, mod, g_ref, win_ref, wout_ref, between_chunks=None):
    h = _mod_norm(x, g_ref[...], mod).astype(BF16)
    acc = jnp.zeros(x.shape, F32)
    zeros = [None] * INTERLEAVE_SLACK
    for j in range(D_FF // FF_CHUNK):
        lo = j * FF_CHUNK
        a = jnp.dot(h, win_ref[:, lo:lo + FF_CHUNK].astype(BF16), preferred_element_type=F32)
        b = jnp.dot(h, win_ref[:, D_FF + lo:D_FF + lo + FF_CHUNK].astype(BF16), preferred_element_type=F32)
        act = (_silu(a) * b).astype(BF16)
        zero = zeros.pop(0)
        rows = pl.ds(lo, FF_CHUNK) if zero is None else pl.ds(pl.multiple_of(lo + zero, FF_CHUNK), FF_CHUNK)
        acc = acc + jnp.dot(act, wout_ref[rows, :].astype(BF16), preferred_element_type=F32)
        zeros.append(between_chunks(j) if between_chunks is not None else None)
    return x + (MACARON * mod[2:3]) * acc


def _ffn_kernel(*refs, nparts, nblk0):
    x_refs = refs[:nparts]
    mod_ref, g_ref, win_ref, wout_ref, o_ref = refs[nparts:]
    x = x_refs[0][...]
    if nparts == 2:
        x = jnp.where(pl.program_id(0) < nblk0, x, x_refs[1][...])
    o_ref[...] = _ffn_body(x, mod_ref[...], g_ref, win_ref, wout_ref)


def _ffn_weight_specs(layer):
    layer_spec = lambda shape: pl.BlockSpec((None,) + shape, lambda i: (layer, 0, 0), pipeline_mode=pl.Buffered(1))
    return [layer_spec((1, D_MODEL)), layer_spec((D_MODEL, 2 * D_FF)), layer_spec((D_FF, D_MODEL))]


def _ffn(parts, mod3, layer, sub, row_fn, g, w_in, w_out):
    nblks = [p.shape[0] // TM for p in parts]
    if len(parts) == 1:
        in_specs = [pl.BlockSpec((TM, D_MODEL), lambda i: (i, 0))]
    else:
        nb0 = nblks[0]
        in_specs = [pl.BlockSpec((TM, D_MODEL), lambda i: (jnp.minimum(i, nb0 - 1), 0)),
                    pl.BlockSpec((TM, D_MODEL), lambda i: (jnp.maximum(i - nb0, 0), 0))]
    in_specs += [_mod_spec(layer, sub, row_fn)] + _ffn_weight_specs(layer)
    n = sum(nblks) * TM
    return pl.pallas_call(
        functools.partial(_ffn_kernel, nparts=len(parts), nblk0=nblks[0]),
        grid=(n // TM,),
        in_specs=in_specs,
        out_specs=pl.BlockSpec((TM, D_MODEL), lambda i: (i, 0)),
        out_shape=jax.ShapeDtypeStruct((n, D_MODEL), F32),
        compiler_params=_params(1),
        name="ffn",
    )(*parts, mod3, g.reshape(DEPTH, 1, D_MODEL), w_in, w_out)


MLA_W1_COLS = Q_LORA + KV_LORA + 2 * CONV_CH + LANES
MLA_QK = MLA_HEADS * HEAD_PAD
KR_LANE = QK_NOPE


def _mla_kv(ckvn, krb, wkv_ref, kt_ref, v_ref, nseq):
    kv = jnp.dot(ckvn.astype(BF16), wkv_ref[...], preferred_element_type=F32)
    k = kv[:, :MLA_QK] + _tile_lanes(krb, MLA_HEADS)
    v_ref[...] = _with_ones_lane(kv[:, MLA_QK:]).astype(BF16)
    rows = k.shape[0] // nseq
    for s in range(nseq):
        kt_ref[s] = k[s * rows:(s + 1) * rows, :].T.astype(BF16)


def _mla_in_kernel(x_ref, mod_ref, g_ref, w1_ref, gq_ref, wqu_ref, gkv_ref, wkv_ref, *rest, rope, nseq):
    if rope:
        cos_ref, sa_ref, sb_ref, q_ref, kt_ref, v_ref, glu_ref = rest
    else:
        wi_ref, wo_ref, q_ref, kt_ref, v_ref, glu_ref, ckv_ref, kr_ref, wi_out, wo_out = rest
        _cast_ffn2_slabs(wi_ref, wo_ref, wi_out, wo_out)
    h = _mod_norm(x_ref[...], g_ref[...], mod_ref[...]).astype(BF16)
    proj = jnp.dot(h, w1_ref[...], preferred_element_type=F32)
    o1 = Q_LORA
    o2 = o1 + KV_LORA
    o3 = o2 + CONV_CH
    o4 = o3 + CONV_CH
    cq, ckv, ua, ub, krb = proj[:, :o1], proj[:, o1:o2], proj[:, o2:o3], proj[:, o3:o4], proj[:, o4:]
    glu_ref[...] = ua * jax.nn.sigmoid(ub)
    q = jnp.dot(_rms(cq, gq_ref[...]).astype(BF16), wqu_ref[...], preferred_element_type=F32)
    ckvn = _rms(ckv, gkv_ref[...])
    if rope:
        cos, sa, sb = cos_ref[...], sa_ref[...], sb_ref[...]
        quarter = QK_ROPE // 4
        krb = _rope(krb, cos, sa, sb, quarter)
        q = _rope(q, _tile_lanes(cos, MLA_HEADS), _tile_lanes(sa, MLA_HEADS), _tile_lanes(sb, MLA_HEADS), quarter)
    else:
        ckv_ref[...] = ckvn
        kr_ref[...] = krb[:, KR_LANE:KR_LANE + QK_ROPE]
    q_ref[...] = (q * ((QK_NOPE + QK_ROPE) ** -0.5 * LOG2E)).astype(BF16)
    _mla_kv(ckvn, krb, wkv_ref, kt_ref, v_ref, nseq)


def _mla_ctx_kernel(ckv_ref, krb_ref, wkv_ref, kt_ref, v_ref):
    _mla_kv(ckv_ref[...], krb_ref[...], wkv_ref, kt_ref, v_ref, 1)


def _mla_in(x, x_off, n, mod3, layer, row_fn, seq, p, tables, ffn2_f32=None):
    rope = tables is not None
    nseq = max(TM // seq, 1)
    rows = TM // nseq
    batch = n // seq
    tiles_per_seq = max(seq // TM, 1)
    row = lambda w: pl.BlockSpec((TM, w), lambda i: (i, 0))
    in_specs = [pl.BlockSpec((TM, D_MODEL), lambda i: (i + x_off, 0)), _mod_spec(layer, 1, row_fn),
                _const_spec((1, D_MODEL)),
                _const_spec((D_MODEL, MLA_W1_COLS)), _const_spec((1, Q_LORA)), _const_spec((Q_LORA, MLA_QK)),
                _const_spec((1, KV_LORA)), _const_spec((KV_LORA, 2 * MLA_QK))]
    args = [x, mod3, p["g_mix"], p["w1"], p["g_q"], p["w_qu"], p["g_kv"], p["w_kv"]]
    kt_spec = pl.BlockSpec((nseq, MLA_QK, rows), lambda i: (i // tiles_per_seq, 0, i % tiles_per_seq))
    out_specs = [row(MLA_QK), kt_spec, row(MLA_QK), row(CONV_CH)]
    out_shape = [jax.ShapeDtypeStruct((n, MLA_QK), BF16), jax.ShapeDtypeStruct((batch, MLA_QK, seq), BF16),
                 jax.ShapeDtypeStruct((n, MLA_QK), BF16), jax.ShapeDtypeStruct((n, CONV_CH), F32)]
    if rope:
        tab = pl.BlockSpec((TM, LANES), lambda i: (i % tiles_per_seq, 0))
        in_specs += [tab, tab, tab]
        args += list(tables)
    else:
        out_specs += [row(KV_LORA), row(QK_ROPE)]
        out_shape += [jax.ShapeDtypeStruct((n, KV_LORA), F32), jax.ShapeDtypeStruct((n, QK_ROPE), F32)]
        cast_in, cast_out, cast_shape = _cast_ffn2_specs(layer, n // TM, *ffn2_f32)
        in_specs += cast_in
        args += list(ffn2_f32)
        out_specs += cast_out
        out_shape += cast_shape
    return pl.pallas_call(
        functools.partial(_mla_in_kernel, rope=rope, nseq=nseq),
        grid=(n // TM,),
        in_specs=in_specs,
        out_specs=out_specs,
        out_shape=out_shape,
        compiler_params=_params(1),
        name="mla_in_rope" if rope else "mla_in",
    )(*args)


def _mla_ctx(ckv, krb, w_kv, seq):
    n = ckv.shape[0]
    return pl.pallas_call(
        _mla_ctx_kernel,
        grid=(n // seq,),
        in_specs=[pl.BlockSpec((seq, KV_LORA), lambda i: (i, 0)), pl.BlockSpec((seq, LANES), lambda i: (i, 0)),
                  _const_spec((KV_LORA, 2 * MLA_QK))],
        out_specs=[pl.BlockSpec((1, MLA_QK, seq), lambda i: (i, 0, 0)), pl.BlockSpec((seq, MLA_QK), lambda i: (i, 0))],
        out_shape=[jax.ShapeDtypeStruct((n // seq, MLA_QK, seq), BF16), jax.ShapeDtypeStruct((n, MLA_QK), BF16)],
        compiler_params=_params(1),
        name="mla_ctx",
    )(ckv, krb, w_kv)


GQA_Q = GQA_HEADS * GQA_HEAD_DIM
GQA_KV = GQA_KV_HEADS * GQA_HEAD_DIM
GQA_VPAD = GQA_KV_HEADS * HEAD_PAD
GQA_W_COLS = GQA_Q + 2 * GQA_KV + GQA_VPAD


def _head_rms(x, g):
    tm, w = x.shape
    lo = lax.broadcasted_iota(jnp.int32, (tm, LANES), 1) < GQA_HEAD_DIM
    outs = []
    for b in range(w // LANES):
        xb = x[:, b * LANES:(b + 1) * LANES]
        sq = xb * xb
        s_lo = jnp.sum(jnp.where(lo, sq, 0.0), axis=-1, keepdims=True)
        s_hi = jnp.sum(jnp.where(lo, 0.0, sq), axis=-1, keepdims=True)
        ms = jnp.where(lo, s_lo, s_hi) * (1.0 / GQA_HEAD_DIM)
        outs.append(xb * lax.rsqrt(ms + EPS))
    return jnp.concatenate(outs, axis=1) * g


def _gqa_in_kernel(x_ref, mod_ref, g_ref, w_ref, gq_ref, gk_ref, *rest, rope, nseq):
    if rope:
        cos_ref, sa_ref, sb_ref, q_ref, kt_ref, v_ref = rest
    else:
        wi_ref, wo_ref, q_ref, kt_ref, v_ref, kc_ref, vc_ref, wi_out, wo_out = rest
        _cast_ffn2_slabs(wi_ref, wo_ref, wi_out, wo_out)
    h = _mod_norm(x_ref[...], g_ref[...], mod_ref[...]).astype(BF16)
    proj = jnp.dot(h, w_ref[...], preferred_element_type=F32)
    q = _head_rms(proj[:, :GQA_Q], gq_ref[...])
    k = _head_rms(proj[:, GQA_Q:GQA_Q + GQA_KV], gk_ref[...])
    if rope:
        cos, sa, sb = cos_ref[...], sa_ref[...], sb_ref[...]
        quarter = GQA_HEAD_DIM // 4
        q = _rope(q, _tile_lanes(cos, GQA_Q // LANES), _tile_lanes(sa, GQA_Q // LANES),
                  _tile_lanes(sb, GQA_Q // LANES), quarter)
        k = _rope(k, _tile_lanes(cos, GQA_KV // LANES), _tile_lanes(sa, GQA_KV // LANES),
                  _tile_lanes(sb, GQA_KV // LANES), quarter)
    else:
        kc_ref[...] = k
        vc_ref[...] = proj[:, GQA_Q + GQA_KV:GQA_Q + 2 * GQA_KV]
    q_ref[...] = (q * (GQA_HEAD_DIM ** -0.5 * LOG2E)).astype(BF16)
    v_ref[...] = _with_ones_lane(proj[:, GQA_Q + 2 * GQA_KV:]).astype(BF16)
    rows = k.shape[0] // nseq
    for s in range(nseq):
        kt_ref[s] = k[s * rows:(s + 1) * rows, :].T.astype(BF16)


def _gqa_ctx_kernel(k_ref, v_ref, kt_ref, vo_ref):
    kt_ref[0] = k_ref[...].T.astype(BF16)
    vo_ref[...] = _with_ones_lane(v_ref[...]).astype(BF16)


def _gqa_in(x, x_off, n, mod3, layer, row_fn, seq, p, tables, ffn2_f32=None):
    rope = tables is not None
    nseq = max(TM // seq, 1)
    rows = TM // nseq
    batch = n // seq
    tiles_per_seq = max(seq // TM, 1)
    row = lambda w: pl.BlockSpec((TM, w), lambda i: (i, 0))
    in_specs = [pl.BlockSpec((TM, D_MODEL), lambda i: (i + x_off, 0)), _mod_spec(layer, 1, row_fn),
                _const_spec((1, D_MODEL)),
                _const_spec((D_MODEL, GQA_W_COLS)), _const_spec((1, GQA_Q)), _const_spec((1, GQA_KV))]
    args = [x, mod3, p["g_mix"], p["w"], p["g_q"], p["g_k"]]
    kt_spec = pl.BlockSpec((nseq, GQA_KV, rows), lambda i: (i // tiles_per_seq, 0, i % tiles_per_seq))
    out_specs = [row(GQA_Q), kt_spec, row(GQA_VPAD)]
    out_shape = [jax.ShapeDtypeStruct((n, GQA_Q), BF16), jax.ShapeDtypeStruct((batch, GQA_KV, seq), BF16),
                 jax.ShapeDtypeStruct((n, GQA_VPAD), BF16)]
    if rope:
        tab = pl.BlockSpec((TM, LANES), lambda i: (i % tiles_per_seq, 0))
        in_specs += [tab, tab, tab]
        args += list(tables)
    else:
        out_specs += [row(GQA_KV), row(GQA_KV)]
        out_shape += [jax.ShapeDtypeStruct((n, GQA_KV), F32), jax.ShapeDtypeStruct((n, GQA_KV), F32)]
        cast_in, cast_out, cast_shape = _cast_ffn2_specs(layer, n // TM, *ffn2_f32)
        in_specs += cast_in
        args += list(ffn2_f32)
        out_specs += cast_out
        out_shape += cast_shape
    return pl.pallas_call(
        functools.partial(_gqa_in_kernel, rope=rope, nseq=nseq),
        grid=(n // TM,),
        in_specs=in_specs,
        out_specs=out_specs,
        out_shape=out_shape,
        compiler_params=_params(1),
        name="gqa_in_rope" if rope else "gqa_in",
    )(*args)


def _gqa_ctx(k, vpad, seq):
    n = k.shape[0]
    return pl.pallas_call(
        _gqa_ctx_kernel,
        grid=(n // seq,),
        in_specs=[pl.BlockSpec((seq, GQA_KV), lambda i: (i, 0)), pl.BlockSpec((seq, GQA_VPAD), lambda i: (i, 0))],
        out_specs=[pl.BlockSpec((1, GQA_KV, seq), lambda i: (i, 0, 0)), pl.BlockSpec((seq, GQA_VPAD), lambda i: (i, 0))],
        out_shape=[jax.ShapeDtypeStruct((n // seq, GQA_KV, seq), BF16), jax.ShapeDtypeStruct((n, GQA_VPAD), BF16)],
        compiler_params=_params(1),
        name="gqa_ctx",
    )(k, vpad)


def _attn_kernel(*refs, nseg, bt, seq, nq, nkv, dq, dk, dv):
    q_ref = refs[0]
    segs = [(refs[1 + 2 * s], refs[2 + 2 * s]) for s in range(nseg)]
    o_ref = refs[1 + 2 * nseg]
    s_refs = refs[2 + 2 * nseg:4 + 2 * nseg]
    nks = [kt_ref.shape[2] for kt_ref, _ in segs]
    offs = [sum(nks[:s]) for s in range(nseg)]

    def scores(bi, r0, s_ref):
        rows = pl.ds(bi * seq + r0, Q_SUB)
        for j in range(nq):
            kv = j * nkv // nq
            qs = q_ref[rows, j * dq:(j + 1) * dq]
            for (kt_ref, _), off, nk in zip(segs, offs, nks):
                s_ref[j, :, off:off + nk] = jnp.dot(qs, kt_ref[bi, kv * dk:(kv + 1) * dk, :],
                                                    preferred_element_type=F32)

    def softmax_pv(bi, r0, s_ref):
        rows = pl.ds(bi * seq + r0, Q_SUB)
        outs = []
        for j in range(nq):
            kv = j * nkv // nq
            s = s_ref[j]
            p = jnp.exp2(s - jnp.max(s, axis=-1, keepdims=True)).astype(BF16)
            o = None
            for (_, v_ref), off, nk in zip(segs, offs, nks):
                pv = jnp.dot(p[:, off:off + nk], v_ref[bi * nk:(bi + 1) * nk, kv * HEAD_PAD:(kv + 1) * HEAD_PAD],
                             preferred_element_type=F32)
                o = pv if o is None else o + pv
            outs.append(o[:, :dv] / o[:, DENOM_LANE:DENOM_LANE + 1])
        o_ref[rows, :] = jnp.concatenate(outs, axis=1).astype(BF16)

    if seq == Q_SUB:
        assert bt == len(s_refs)
        for bi in range(bt):
            scores(bi, 0, s_refs[bi])
        for bi in range(bt):
            softmax_pv(bi, 0, s_refs[bi])
    else:
        n_sub = seq // Q_SUB
        assert bt == 1
        scores(0, 0, s_refs[0])
        for r in range(n_sub):
            if r + 1 < n_sub:
                scores(0, (r + 1) * Q_SUB, s_refs[(r + 1) % 2])
            softmax_pv(0, r * Q_SUB, s_refs[r % 2])


def _attention(q, segs, *, seq, heads, kv_heads, dq, dk, dv, bt, heads_per_step):
    n = q.shape[0]
    batch = n // seq
    nq = heads_per_step
    steps = heads // nq
    nkv = max(kv_heads // steps, 1)
    q_per_kv_block = steps // (kv_heads // nkv)
    in_specs = [pl.BlockSpec((bt * seq, nq * dq), lambda b, p: (b, p))]
    args = [q]
    for kt, v in segs:
        nk = kt.shape[2]
        in_specs.append(pl.BlockSpec((bt, nkv * dk, nk), lambda b, p: (b, p // q_per_kv_block, 0)))
        in_specs.append(pl.BlockSpec((bt * nk, nkv * HEAD_PAD), lambda b, p: (b, p // q_per_kv_block)))
        args += [kt, v]
    return pl.pallas_call(
        functools.partial(_attn_kernel, nseg=len(segs), bt=bt, seq=seq, nq=nq, nkv=nkv, dq=dq, dk=dk, dv=dv),
        grid=(batch // bt, steps),
        in_specs=in_specs,
        out_specs=pl.BlockSpec((bt * seq, nq * dv), lambda b, p: (b, p)),
        out_shape=jax.ShapeDtypeStruct((n, heads * dv), BF16),
        scratch_shapes=[pltpu.VMEM((nq, Q_SUB, sum(kt.shape[2] for kt, _ in segs)), F32)] * 2,
        compiler_params=_params(2),
        name="attention",
    )(*args)


SUBLANES = 8
CONV_LEAD = HALO - CONV_PAD
CONV_WIN = CONV_ROWS + 2 * HALO


def _fill_pad(pad_ref, glu_ref, prev_ref, next_ref, first, last, nseq):
    rows = TM // nseq
    zeros = jnp.zeros((HALO, CONV_CH), F32)
    for s in range(nseq):
        pad_ref[s, 0:HALO, :] = zeros if prev_ref is None else jnp.where(first, zeros, prev_ref[...])
        pad_ref[s, HALO + rows:2 * HALO + rows, :] = (
            zeros if next_ref is None else jnp.where(last, zeros, next_ref[...]))
        pad_ref[s, HALO:HALO + rows, :] = glu_ref[s * rows:(s + 1) * rows, :]


def _conv_chunk(pad_ref, s, r0, wdw_ref, bdw_ref, gln_ref, bln_ref):
    blocks = []
    for lb in range(CONV_CH // LANES):
        lanes = slice(lb * LANES, (lb + 1) * LANES)
        win = pad_ref[s, pl.ds(r0, CONV_WIN), lanes]
        acc = jnp.broadcast_to(bdw_ref[:, lanes], (CONV_ROWS, LANES))
        for phase in range(SUBLANES):
            shifted = win if phase == 0 else pltpu.roll(win, CONV_WIN - phase, 0)
            for a in range((CONV_W + CONV_LEAD) // SUBLANES + 1):
                k = a * SUBLANES + phase - CONV_LEAD
                if 0 <= k < CONV_W:
                    acc = acc + shifted[a * SUBLANES:a * SUBLANES + CONV_ROWS] * wdw_ref[k:k + 1, lanes]
        blocks.append(acc)
    acc = jnp.concatenate(blocks, axis=1)
    mu = jnp.mean(acc, axis=-1, keepdims=True)
    d = acc - mu
    var = jnp.mean(d * d, axis=-1, keepdims=True)
    return _silu(d * lax.rsqrt(var + EPS) * gln_ref[...] + bln_ref[...])


def _out_ffn_kernel(*refs, conv, nseq, halo, tiles_per_seq, final):
    it = iter(refs)
    x_ref, modm_ref, modf_ref, attn_ref = next(it), next(it), next(it), next(it)
    if conv:
        glu0_ref = next(it)
        next0_ref = next(it) if halo else None
        glun_ref = next(it)
        prevn_ref, nextn_ref = (next(it), next(it)) if halo else (None, None)
        conv_w = [next(it) for _ in range(4)]
        zero_ref = next(it)
    wout_ref, gff_ref, win_ref, wff_ref = next(it), next(it), next(it), next(it)
    gfin_ref = next(it) if final else None
    y_ref = next(it)
    i = pl.program_id(0)
    rows = TM // nseq

    if conv:
        pad_ref, g_ref = next(it), next(it)

        def conv_rows(c):
            s, r0 = divmod(c * CONV_ROWS, rows)
            g = _conv_chunk(pad_ref, s, r0, *conv_w)
            g_ref[c * CONV_ROWS:(c + 1) * CONV_ROWS, :] = g.astype(BF16)
            return jnp.max(g).astype(jnp.int32) & zero_ref[0]

        @pl.when(i == 0)
        def _():
            _fill_pad(pad_ref, glu0_ref, None, next0_ref, True, tiles_per_seq == 1, nseq)
            for c in range(TM // CONV_ROWS):
                conv_rows(c)

        out = (jnp.dot(attn_ref[...], wout_ref[:CONV_CH, :], preferred_element_type=F32)
               + jnp.dot(g_ref[...], wout_ref[CONV_CH:, :], preferred_element_type=F32))
        pos = (i + 1) % tiles_per_seq
        _fill_pad(pad_ref, glun_ref, prevn_ref, nextn_ref, pos == 0, pos == tiles_per_seq - 1, nseq)
        between = lambda j: conv_rows(j) if j < TM // CONV_ROWS else None
    else:
        out = jnp.dot(attn_ref[...], wout_ref[...], preferred_element_type=F32)
        between = None
    x = x_ref[...] + modm_ref[2:3, :] * out
    y = _ffn_body(x, modf_ref[...], gff_ref, win_ref, wff_ref, between)
    if final:
        y = _rms(y, gfin_ref[...])
    y_ref[...] = y


def _out_ffn(x, x_off, mod3, layer, row_fn, seq, attn, w_out, ffn_w, glu=None, conv_p=None, g_final=None):
    n = attn.shape[0]
    n_tiles = n // TM
    conv = glu is not None
    final = g_final is not None
    nseq = max(TM // seq, 1)
    tiles_per_seq = max(seq // TM, 1)
    halo = conv and tiles_per_seq > 1
    assert TM // CONV_ROWS <= D_FF // FF_CHUNK
    row = lambda w: pl.BlockSpec((TM, w), lambda i: (i, 0))
    in_specs = [pl.BlockSpec((TM, D_MODEL), lambda i: (i + x_off, 0)), _mod_spec(layer, 1, row_fn),
                _mod_spec(layer, 2, row_fn), row(attn.shape[1])]
    args = [x, mod3, mod3, attn]
    scratch = []
    if conv:
        per = TM // HALO
        last = n // HALO - 1
        nxt = lambda i: jnp.minimum(i + 1, n_tiles - 1)
        in_specs.append(pl.BlockSpec((TM, CONV_CH), lambda i: (0, 0)))
        args.append(glu)
        if halo:
            in_specs.append(pl.BlockSpec((HALO, CONV_CH), lambda i: (per, 0)))
            args.append(glu)
        in_specs.append(pl.BlockSpec((TM, CONV_CH), lambda i: (nxt(i), 0)))
        args.append(glu)
        if halo:
            in_specs += [pl.BlockSpec((HALO, CONV_CH), lambda i: (jnp.maximum(nxt(i) * per - 1, 0), 0)),
                         pl.BlockSpec((HALO, CONV_CH), lambda i: (jnp.minimum((nxt(i) + 1) * per, last), 0))]
            args += [glu, glu]
        in_specs += [_const_spec((CONV_W + 1, CONV_CH)), _const_spec((1, CONV_CH)), _const_spec((1, CONV_CH)),
                     _const_spec((1, CONV_CH)), pl.BlockSpec(memory_space=pltpu.SMEM)]
        args += [conv_p["w_dw"], conv_p["b_dw"], conv_p["g_ln"], conv_p["b_ln"], jnp.zeros((1,), jnp.int32)]
        scratch = [pltpu.VMEM((nseq, TM // nseq + 2 * HALO, CONV_CH), F32), pltpu.VMEM((TM, CONV_CH), BF16)]
    g_ff, w_in, w_ff = ffn_w
    in_specs += [_const_spec(w_out.shape), _ffn_weight_specs(layer)[0], _const_spec(w_in.shape),
                 _const_spec(w_ff.shape)]
    args += [w_out, g_ff.reshape(DEPTH, 1, D_MODEL), w_in, w_ff]
    if final:
        in_specs.append(_const_spec((1, D_MODEL)))
        args.append(g_final.reshape(1, D_MODEL))
    return pl.pallas_call(
        functools.partial(_out_ffn_kernel, conv=conv, nseq=nseq, halo=halo, tiles_per_seq=tiles_per_seq,
                          final=final),
        grid=(n_tiles,),
        in_specs=in_specs,
        out_specs=row(D_MODEL),
        out_shape=jax.ShapeDtypeStruct((n, D_MODEL), F32),
        scratch_shapes=scratch,
        compiler_params=pltpu.CompilerParams(dimension_semantics=("arbitrary",), vmem_limit_bytes=VMEM_LIMIT),
        name="out_ffn_conv" if conv else "out_ffn",
    )(*args)


def _rope_tables(d, offset, period):
    half = d // 2
    quarter = half // 2
    rows = DEC_SEQ // GRID_W
    pos_row = np.repeat(np.arange(rows), GRID_W)
    pos_col = np.tile(np.arange(GRID_W), rows)
    inv = ROPE_BASE ** (-np.arange(0, half, 2, dtype=np.float64) / half)
    ang_r = pos_row.astype(np.float64)[:, None] * inv[None, :]
    ang_c = pos_col.astype(np.float64)[:, None] * inv[None, :]
    zero = np.zeros((DEC_SEQ, quarter))
    cos = np.concatenate([np.cos(ang_r), np.cos(ang_r), np.cos(ang_c), np.cos(ang_c)], axis=1)
    sa = np.concatenate([-np.sin(ang_r), zero, -np.sin(ang_c), zero], axis=1)
    sb = np.concatenate([zero, np.sin(ang_r), zero, np.sin(ang_c)], axis=1)

    def embed(t, fill):
        blk = np.concatenate([np.full((DEC_SEQ, offset), fill), t,
                              np.full((DEC_SEQ, period - offset - d), fill)], axis=1)
        return jnp.asarray(np.tile(blk, (1, LANES // period)), dtype=F32)

    return embed(cos, 1.0), embed(sa, 0.0), embed(sb, 0.0)


def _pad_heads(w, heads, dim):
    k = w.shape[0]
    return jnp.pad(w.reshape(k, heads, dim), ((0, 0), (0, 0), (0, HEAD_PAD - dim))).reshape(k, heads * HEAD_PAD)


def _prep_mla(i, g_mix_l, w_in_a, g_q_lora, w_q_up, g_kv_lora, w_kv_up, w_dw, b_dw, g_conv_ln, b_conv_ln, w_out_a):
    w = w_in_a[i]
    o1 = Q_LORA
    o2 = o1 + KV_LORA
    o3 = o2 + QK_ROPE
    zeros = lambda c: jnp.zeros((D_MODEL, c), F32)
    w1 = jnp.concatenate([w[:, :o2], w[:, o3:], zeros(KR_LANE), w[:, o2:o3],
                          zeros(LANES - KR_LANE - QK_ROPE)], axis=1)
    kvu = w_kv_up[i].reshape(KV_LORA, MLA_HEADS, QK_NOPE + V_HEAD)
    w_kn = _pad_heads(kvu[:, :, :QK_NOPE].reshape(KV_LORA, MLA_HEADS * QK_NOPE), MLA_HEADS, QK_NOPE)
    w_v = _pad_heads(kvu[:, :, QK_NOPE:].reshape(KV_LORA, MLA_HEADS * V_HEAD), MLA_HEADS, V_HEAD)
    return dict(
        g_mix=g_mix_l.reshape(1, D_MODEL),
        w1=w1.astype(BF16),
        g_q=g_q_lora[i].reshape(1, Q_LORA),
        w_qu=_pad_heads(w_q_up[i], MLA_HEADS, QK_NOPE + QK_ROPE).astype(BF16),
        g_kv=g_kv_lora[i].reshape(1, KV_LORA),
        w_kv=jnp.concatenate([w_kn, w_v], axis=1).astype(BF16),
        w_dw=jnp.pad(w_dw[i], ((0, 1), (0, 0))),
        b_dw=b_dw[i].reshape(1, CONV_CH),
        g_ln=g_conv_ln[i].reshape(1, CONV_CH),
        b_ln=b_conv_ln[i].reshape(1, CONV_CH),
        w_out=w_out_a[i].astype(BF16),
    )


def _prep_gqa(i, g_mix_l, w_in_c, g_q_head, g_k_head, w_out_c):
    w = w_in_c[i]
    v_pad = _pad_heads(w[:, GQA_Q + GQA_KV:], GQA_KV_HEADS, GQA_HEAD_DIM)
    return dict(
        g_mix=g_mix_l.reshape(1, D_MODEL),
        w=jnp.concatenate([w, v_pad], axis=1).astype(BF16),
        g_q=jnp.tile(g_q_head[i], GQA_HEADS).reshape(1, GQA_Q),
        g_k=jnp.tile(g_k_head[i], GQA_KV_HEADS).reshape(1, GQA_KV),
        w_out=w_out_c[i].astype(BF16),
    )


SAMPLE_HEADS_PER_STEP = 4
PROMPT_SEQS_PER_STEP = 2


def _mixer_ffn(x_all, x_off, n, mod3, l, row_fn, seq, p, ctx, tables, g_ff2, ffn2_w, g_final):
    sample = ctx is not None
    in_proj, n_core = (_mla_in, 4) if l % 2 == 0 else (_gqa_in, 3)
    outs = in_proj(x_all, x_off, n, mod3, l, row_fn, seq, p, tables["mla" if l % 2 == 0 else "gqa"] if sample else None,
                   None if sample else ffn2_w)
    if not sample:
        outs, ffn2_w = outs[:-2], tuple(outs[-2:])
    ffn_w = (g_ff2,) + tuple(ffn2_w)
    if l % 2 == 0:
        q, kt, v, glu = outs[:4]
        segs = [(kt, v)]
        if sample:
            segs = [_mla_ctx(ctx["mla_ckv"], ctx["mla_krb"], p["w_kv"], ctx["past"])] + segs
            hps, bt = SAMPLE_HEADS_PER_STEP, 1
        else:
            hps, bt = MLA_HEADS, PROMPT_SEQS_PER_STEP
        attn = _attention(q, segs, seq=seq, heads=MLA_HEADS, kv_heads=MLA_HEADS, dq=HEAD_PAD, dk=HEAD_PAD,
                          dv=V_HEAD, bt=bt, heads_per_step=hps)
        return _out_ffn(x_all, x_off, mod3, l, row_fn, seq, attn, p["w_out"], ffn_w, glu=glu, conv_p=p,
                        g_final=g_final), outs[n_core:], ffn2_w
    q, kt, v = outs[:3]
    segs = [(kt, v)]
    if sample:
        segs = [_gqa_ctx(ctx["gqa_k"], ctx["gqa_vpad"], ctx["past"])] + segs
        hps, bt = SAMPLE_HEADS_PER_STEP, 1
    else:
        hps, bt = GQA_HEADS, PROMPT_SEQS_PER_STEP
    attn = _attention(q, segs, seq=seq, heads=GQA_HEADS, kv_heads=GQA_KV_HEADS, dq=GQA_HEAD_DIM,
                      dk=GQA_HEAD_DIM, dv=GQA_HEAD_DIM, bt=bt, heads_per_step=hps)
    return (_out_ffn(x_all, x_off, mod3, l, row_fn, seq, attn, p["w_out"], ffn_w, g_final=g_final), outs[n_core:],
            ffn2_w)


def kernel(x_prompt, x_sample, cache_mla_ckv, cache_mla_krope, cache_gqa_k, cache_gqa_v, c, c_ctx, g_ff1, w_ff1_in, w_ff1_out, g_mix, g_ff2, w_ff2_in, w_ff2_out, w_mod, b_mod, w_in_a, g_q_lora, w_q_up, g_kv_lora, w_kv_up, w_dw, b_dw, g_conv_ln, b_conv_ln, w_out_a, w_in_c, g_q_head, g_k_head, w_out_c, g_final):
    batch, seq, _ = x_prompt.shape
    dec_batch, dec_seq, _ = x_sample.shape
    past = cache_mla_ckv.shape[2]
    assert DEPTH == 2 and dec_seq == DEC_SEQ and 1 + dec_batch <= MOD_ROWS

    cond = jnp.concatenate([c_ctx[None, :], c, jnp.zeros((MOD_ROWS - 1 - dec_batch, D_MODEL), F32)], axis=0)
    mod3 = _modulation(cond, w_mod, b_mod)

    layers = [
        _prep_mla(0, g_mix[0], w_in_a, g_q_lora, w_q_up, g_kv_lora, w_kv_up, w_dw, b_dw, g_conv_ln, b_conv_ln,
                  w_out_a),
        _prep_gqa(0, g_mix[1], w_in_c, g_q_head, g_k_head, w_out_c),
    ]
    tables = dict(mla=_rope_tables(QK_ROPE, KR_LANE, HEAD_PAD), gqa=_rope_tables(GQA_HEAD_DIM, 0, GQA_HEAD_DIM))

    ctx = dict(
        past=past,
        mla_ckv=cache_mla_ckv[:, 0].reshape(dec_batch * past, KV_LORA),
        mla_krb=jnp.pad(cache_mla_krope[:, 0].reshape(dec_batch * past, QK_ROPE),
                        ((0, 0), (KR_LANE, LANES - KR_LANE - QK_ROPE))),
        gqa_k=cache_gqa_k[:, 0].reshape(dec_batch * past, GQA_KV),
        gqa_vpad=jnp.pad(cache_gqa_v[:, 0], ((0, 0), (0, 0), (0, 0), (0, HEAD_PAD - GQA_HEAD_DIM))
                         ).reshape(dec_batch * past, GQA_VPAD),
    )

    n_p, n_s = batch * seq, dec_batch * dec_seq
    nb_p = n_p // TM
    row_p = lambda i: 0
    row_s = lambda i: 1 + (i * TM) // DEC_SEQ
    row_all = lambda i: jnp.where(i < nb_p, 0, 1 + ((i - nb_p) * TM) // DEC_SEQ)

    parts = [x_prompt.reshape(n_p, D_MODEL), x_sample.reshape(n_s, D_MODEL)]
    saved = []
    for l in range(DEPTH):
        gf = g_final if l == DEPTH - 1 else None
        x_all = _ffn(parts, mod3, l, 0, row_all, g_ff1, w_ff1_in, w_ff1_out)
        xp, st, w2 = _mixer_ffn(x_all, 0, n_p, mod3, l, row_p, seq, layers[l], None, tables, g_ff2,
                                (w_ff2_in, w_ff2_out), gf)
        xs, _, _ = _mixer_ffn(x_all, nb_p, n_s, mod3, l, row_s, dec_seq, layers[l], ctx, tables, g_ff2, w2, gf)
        saved.append(st)
        parts = [xp, xs]
    y_prompt, y_sample = parts
    (ckv_new, kr_new), (k_new, v_new) = saved

    return (y_prompt.reshape(batch, seq, D_MODEL),
            y_sample.reshape(dec_batch, dec_seq, D_MODEL),
            ckv_new.reshape(batch, 1, seq, KV_LORA),
            kr_new.reshape(batch, 1, seq, QK_ROPE),
            k_new.reshape(batch, 1, seq, GQA_KV_HEADS, GQA_HEAD_DIM),
            v_new.reshape(batch, 1, seq, GQA_KV_HEADS, GQA_HEAD_DIM))
```

```python
import functools

import jax
import jax.numpy as jnp
import numpy as np
from jax import lax
from jax.experimental import pallas as pl
from jax.experimental.pallas import tpu as pltpu

F32 = jnp.float32
BF16 = jnp.bfloat16

D_MODEL = 1024
DEPTH = 2
DEC_SEQ = 2048
GRID_W = 64
MLA_HEADS = 8
Q_LORA = 384
KV_LORA = 256
QK_NOPE = 64
QK_ROPE = 32
V_HEAD = 64
CONV_CH = 512
CONV_W = 31
CONV_PAD = CONV_W // 2
GQA_HEADS = 16
GQA_KV_HEADS = 4
GQA_HEAD_DIM = 64
D_FF = 2816
MACARON = 0.5
N_MOD = 9
ROPE_BASE = 10000.0
EPS = 1e-6

LANES = 128
HEAD_PAD = 128
MOD_ROWS = 8
DENOM_LANE = 64
LOG2E = 1.4426950408889634
HALO = 16
TM = 512
Q_SUB = 256
FF_CHUNK = 256
CONV_ROWS = 64
INTERLEAVE_SLACK = 2
VMEM_LIMIT = 56 * 1024 * 1024


def _params(n_axes):
    return pltpu.CompilerParams(dimension_semantics=("parallel",) * n_axes, vmem_limit_bytes=VMEM_LIMIT)


def _const_spec(shape):
    return pl.BlockSpec(shape, lambda *_: (0,) * len(shape), pipeline_mode=pl.Buffered(1))


def _rms(x, g):
    return x * lax.rsqrt(jnp.mean(x * x, axis=-1, keepdims=True) + EPS) * g


def _mod_norm(x, g, mod):
    return _rms(x, g) * (1.0 + mod[1:2]) + mod[0:1]


def _silu(x):
    return x * jax.nn.sigmoid(x)


def _rope(x, cos, sa, sb, quarter):
    w = x.shape[-1]
    return x * cos + pltpu.roll(x, w - quarter, 1) * sa + pltpu.roll(x, quarter, 1) * sb


def _with_ones_lane(v):
    lane = lax.broadcasted_iota(jnp.int32, v.shape, 1)
    return jnp.where(lane % HEAD_PAD == DENOM_LANE, 1.0, v)


def _tile_lanes(t, reps):
    return t if reps == 1 else jnp.concatenate([t] * reps, axis=1)


def _mod_kernel(c_ref, w_ref, b_ref, o_ref):
    e = _silu(c_ref[...]).astype(BF16)
    o_ref[...] = jnp.dot(e, w_ref[...].astype(BF16), preferred_element_type=F32) + b_ref[...]


def _modulation(cond, w_mod, b_mod):
    n_out = N_MOD * D_MODEL
    tn = 1536
    out = pl.pallas_call(
        _mod_kernel,
        grid=(DEPTH, n_out // tn),
        in_specs=[
            pl.BlockSpec((MOD_ROWS, D_MODEL), lambda l, j: (0, 0)),
            pl.BlockSpec((None, D_MODEL, tn), lambda l, j: (l, 0, j)),
            pl.BlockSpec((None, 1, tn), lambda l, j: (l, 0, j)),
        ],
        out_specs=pl.BlockSpec((None, MOD_ROWS, tn), lambda l, j: (l, 0, j)),
        out_shape=jax.ShapeDtypeStruct((DEPTH, MOD_ROWS, n_out), F32),
        compiler_params=_params(2),
        name="modulation",
    )(cond, w_mod, b_mod.reshape(DEPTH, 1, n_out))
    return out.reshape(DEPTH * MOD_ROWS * 3, 3, D_MODEL)


def _mod_spec(layer, sub, row_fn):
    return pl.BlockSpec((None, 3, D_MODEL), lambda i, *_: ((layer * MOD_ROWS + row_fn(i)) * 3 + sub, 0, 0))


def _cast_ffn2_slabs(wi_ref, wo_ref, wi_out, wo_out):
    wi_out[...] = wi_ref[...].astype(BF16)
    wo_out[...] = wo_ref[...].astype(BF16)


def _cast_ffn2_specs(layer, steps, w_in, w_out):
    shapes = [w_in.shape[1:], w_out.shape[1:]]
    assert all(k % (steps * 16) == 0 for k, _ in shapes)
    in_specs = [pl.BlockSpec((None, k // steps, n), lambda i: (layer, i, 0)) for k, n in shapes]
    out_specs = [pl.BlockSpec((k // steps, n), lambda i: (i, 0)) for k, n in shapes]
    return in_specs, out_specs, [jax.ShapeDtypeStruct(s, BF16) for s in shapes]


def _ffn_body(x, mod, g_ref, win_ref, wout_ref, between_chunks=None):
    h = _mod_norm(x, g_ref[...], mod).astype(BF16)
    acc = jnp.zeros(x.shape, F32)
    zeros = [None] * INTERLEAVE_SLACK
    for j in range(D_FF // FF_CHUNK):
        lo = j * FF_CHUNK
        a = jnp.dot(h, win_ref[:, lo:lo + FF_CHUNK].astype(BF16), preferred_element_type=F32)
        b = jnp.dot(h, win_ref[:, D_FF + lo:D_FF + lo + FF_CHUNK].astype(BF16), preferred_element_type=F32)
        act = (_silu(a) * b).astype(BF16)
        zero = zeros.pop(0)
        rows = pl.ds(lo, FF_CHUNK) if zero is None else pl.ds(pl.multiple_of(lo + zero, FF_CHUNK), FF_CHUNK)
        acc = acc + jnp.dot(act, wout_ref[rows, :].astype(BF16), preferred_element_type=F32)
        zeros.append(between_chunks(j) if between_chunks is not None else None)
    return x + (MACARON * mod[2:3]) * acc


def _ffn_kernel(*refs, nparts, nblk0):
    x_refs = refs[:nparts]
    mod_ref, g_ref, win_ref, wout_ref, o_ref = refs[nparts:]
    x = x_refs[0][...]
    if nparts == 2:
        x = jnp.where(pl.program_id(0) < nblk0, x, x_refs[1][...])
    o_ref[...] = _ffn_body(x, mod_ref[...], g_ref, win_ref, wout_ref)


def _ffn_weight_specs(layer):
    layer_spec = lambda shape: pl.BlockSpec((None,) + shape, lambda i: (layer, 0, 0), pipeline_mode=pl.Buffered(1))
    return [layer_spec((1, D_MODEL)), layer_spec((D_MODEL, 2 * D_FF)), layer_spec((D_FF, D_MODEL))]


def _ffn(parts, mod3, layer, sub, row_fn, g, w_in, w_out):
    nblks = [p.shape[0] // TM for p in parts]
    if len(parts) == 1:
        in_specs = [pl.BlockSpec((TM, D_MODEL), lambda i: (i, 0))]
    else:
        nb0 = nblks[0]
        in_specs = [pl.BlockSpec((TM, D_MODEL), lambda i: (jnp.minimum(i, nb0 - 1), 0)),
                    pl.BlockSpec((TM, D_MODEL), lambda i: (jnp.maximum(i - nb0, 0), 0))]
    in_specs += [_mod_spec(layer, sub, row_fn)] + _ffn_weight_specs(layer)
    n = sum(nblks) * TM
    return pl.pallas_call(
        functools.partial(_ffn_kernel, nparts=len(parts), nblk0=nblks[0]),
        grid=(n // TM,),
        in_specs=in_specs,
        out_specs=pl.BlockSpec((TM, D_MODEL), lambda i: (i, 0)),
        out_shape=jax.ShapeDtypeStruct((n, D_MODEL), F32),
        compiler_params=_params(1),
        name="ffn",
    )(*parts, mod3, g.reshape(DEPTH, 1, D_MODEL), w_in, w_out)


MLA_W1_COLS = Q_LORA + KV_LORA + 2 * CONV_CH + LANES
MLA_QK = MLA_HEADS * HEAD_PAD
KR_LANE = QK_NOPE


def _mla_kv(ckvn, krb, wkv_ref, kt_ref, v_ref, nseq):
    kv = jnp.dot(ckvn.astype(BF16), wkv_ref[...], preferred_element_type=F32)
    k = kv[:, :MLA_QK] + _tile_lanes(krb, MLA_HEADS)
    v_ref[...] = _with_ones_lane(kv[:, MLA_QK:]).astype(BF16)
    rows = k.shape[0] // nseq
    for s in range(nseq):
        kt_ref[s] = k[s * rows:(s + 1) * rows, :].T.astype(BF16)


def _mla_in_kernel(x_ref, mod_ref, g_ref, w1_ref, gq_ref, wqu_ref, gkv_ref, wkv_ref, *rest, rope, nseq):
    if rope:
        cos_ref, sa_ref, sb_ref, q_ref, kt_ref, v_ref, glu_ref = rest
    else:
        wi_ref, wo_ref, q_ref, kt_ref, v_ref, glu_ref, ckv_ref, kr_ref, wi_out, wo_out = rest
        _cast_ffn2_slabs(wi_ref, wo_ref, wi_out, wo_out)
    h = _mod_norm(x_ref[...], g_ref[...], mod_ref[...]).astype(BF16)
    proj = jnp.dot(h, w1_ref[...], preferred_element_type=F32)
    o1 = Q_LORA
    o2 = o1 + KV_LORA
    o3 = o2 + CONV_CH
    o4 = o3 + CONV_CH
    cq, ckv, ua, ub, krb = proj[:, :o1], proj[:, o1:o2], proj[:, o2:o3], proj[:, o3:o4], proj[:, o4:]
    glu_ref[...] = ua * jax.nn.sigmoid(ub)
    q = jnp.dot(_rms(cq, gq_ref[...]).astype(BF16), wqu_ref[...], preferred_element_type=F32)
    ckvn = _rms(ckv, gkv_ref[...])
    if rope:
        cos, sa, sb = cos_ref[...], sa_ref[...], sb_ref[...]
        quarter = QK_ROPE // 4
        krb = _rope(krb, cos, sa, sb, quarter)
        q = _rope(q, _tile_lanes(cos, MLA_HEADS), _tile_lanes(sa, MLA_HEADS), _tile_lanes(sb, MLA_HEADS), quarter)
    else:
        ckv_ref[...] = ckvn
        kr_ref[...] = krb[:, KR_LANE:KR_LANE + QK_ROPE]
    q_ref[...] = (q * ((QK_NOPE + QK_ROPE) ** -0.5 * LOG2E)).astype(BF16)
    _mla_kv(ckvn, krb, wkv_ref, kt_ref, v_ref, nseq)


def _mla_ctx_kernel(ckv_ref, krb_ref, wkv_ref, kt_ref, v_ref):
    _mla_kv(ckv_ref[...], krb_ref[...], wkv_ref, kt_ref, v_ref, 1)


def _mla_in(x, x_off, n, mod3, layer, row_fn, seq, p, tables, ffn2_f32=None):
    rope = tables is not None
    nseq = max(TM // seq, 1)
    rows = TM // nseq
    batch = n // seq
    tiles_per_seq = max(seq // TM, 1)
    row = lambda w: pl.BlockSpec((TM, w), lambda i: (i, 0))
    in_specs = [pl.BlockSpec((TM, D_MODEL), lambda i: (i + x_off, 0)), _mod_spec(layer, 1, row_fn),
                _const_spec((1, D_MODEL)),
                _const_spec((D_MODEL, MLA_W1_COLS)), _const_spec((1, Q_LORA)), _const_spec((Q_LORA, MLA_QK)),
                _const_spec((1, KV_LORA)), _const_spec((KV_LORA, 2 * MLA_QK))]
    args = [x, mod3, p["g_mix"], p["w1"], p["g_q"], p["w_qu"], p["g_kv"], p["w_kv"]]
    kt_spec = pl.BlockSpec((nseq, MLA_QK, rows), lambda i: (i // tiles_per_seq, 0, i % tiles_per_seq))
    out_specs = [row(MLA_QK), kt_spec, row(MLA_QK), row(CONV_CH)]
    out_shape = [jax.ShapeDtypeStruct((n, MLA_QK), BF16), jax.ShapeDtypeStruct((batch, MLA_QK, seq), BF16),
                 jax.ShapeDtypeStruct((n, MLA_QK), BF16), jax.ShapeDtypeStruct((n, CONV_CH), F32)]
    if rope:
        tab = pl.BlockSpec((TM, LANES), lambda i: (i % tiles_per_seq, 0))
        in_specs += [tab, tab, tab]
        args += list(tables)
    else:
        out_specs += [row(KV_LORA), row(QK_ROPE)]
        out_shape += [jax.ShapeDtypeStruct((n, KV_LORA), F32), jax.ShapeDtypeStruct((n, QK_ROPE), F32)]
        cast_in, cast_out, cast_shape = _cast_ffn2_specs(layer, n // TM, *ffn2_f32)
        in_specs += cast_in
        args += list(ffn2_f32)
        out_specs += cast_out
        out_shape += cast_shape
    return pl.pallas_call(
        functools.partial(_mla_in_kernel, rope=rope, nseq=nseq),
        grid=(n // TM,),
        in_specs=in_specs,
        out_specs=out_specs,
        out_shape=out_shape,
        compiler_params=_params(1),
        name="mla_in_rope" if rope else "mla_in",
    )(*args)


def _mla_ctx(ckv, krb, w_kv, seq):
    n = ckv.shape[0]
    return pl.pallas_call(
        _mla_ctx_kernel,
        grid=(n // seq,),
        in_specs=[pl.BlockSpec((seq, KV_LORA), lambda i: (i, 0)), pl.BlockSpec((seq, LANES), lambda i: (i, 0)),
                  _const_spec((KV_LORA, 2 * MLA_QK))],
        out_specs=[pl.BlockSpec((1, MLA_QK, seq), lambda i: (i, 0, 0)), pl.BlockSpec((seq, MLA_QK), lambda i: (i, 0))],
        out_shape=[jax.ShapeDtypeStruct((n // seq, MLA_QK, seq), BF16), jax.ShapeDtypeStruct((n, MLA_QK), BF16)],
        compiler_params=_params(1),
        name="mla_ctx",
    )(ckv, krb, w_kv)


GQA_Q = GQA_HEADS * GQA_HEAD_DIM
GQA_KV = GQA_KV_HEADS * GQA_HEAD_DIM
GQA_VPAD = GQA_KV_HEADS * HEAD_PAD
GQA_W_COLS = GQA_Q + 2 * GQA_KV + GQA_VPAD


def _head_rms(x, g):
    tm, w = x.shape
    lo = lax.broadcasted_iota(jnp.int32, (tm, LANES), 1) < GQA_HEAD_DIM
    outs = []
    for b in range(w // LANES):
        xb = x[:, b * LANES:(b + 1) * LANES]
        sq = xb * xb
        s_lo = jnp.sum(jnp.where(lo, sq, 0.0), axis=-1, keepdims=True)
        s_hi = jnp.sum(jnp.where(lo, 0.0, sq), axis=-1, keepdims=True)
        ms = jnp.where(lo, s_lo, s_hi) * (1.0 / GQA_HEAD_DIM)
        outs.append(xb * lax.rsqrt(ms + EPS))
    return jnp.concatenate(outs, axis=1) * g


def _gqa_in_kernel(x_ref, mod_ref, g_ref, w_ref, gq_ref, gk_ref, *rest, rope, nseq):
    if rope:
        cos_ref, sa_ref, sb_ref, q_ref, kt_ref, v_ref = rest
    else:
        wi_ref, wo_ref, q_ref, kt_ref, v_ref, kc_ref, vc_ref, wi_out, wo_out = rest
        _cast_ffn2_slabs(wi_ref, wo_ref, wi_out, wo_out)
    h = _mod_norm(x_ref[...], g_ref[...], mod_ref[...]).astype(BF16)
    proj = jnp.dot(h, w_ref[...], preferred_element_type=F32)
    q = _head_rms(proj[:, :GQA_Q], gq_ref[...])
    k = _head_rms(proj[:, GQA_Q:GQA_Q + GQA_KV], gk_ref[...])
    if rope:
        cos, sa, sb = cos_ref[...], sa_ref[...], sb_ref[...]
        quarter = GQA_HEAD_DIM // 4
        q = _rope(q, _tile_lanes(cos, GQA_Q // LANES), _tile_lanes(sa, GQA_Q // LANES),
                  _tile_lanes(sb, GQA_Q // LANES), quarter)
        k = _rope(k, _tile_lanes(cos, GQA_KV // LANES), _tile_lanes(sa, GQA_KV // LANES),
                  _tile_lanes(sb, GQA_KV // LANES), quarter)
    else:
        kc_ref[...] = k
        vc_ref[...] = proj[:, GQA_Q + GQA_KV:GQA_Q + 2 * GQA_KV]
    q_ref[...] = (q * (GQA_HEAD_DIM ** -0.5 * LOG2E)).astype(BF16)
    v_ref[...] = _with_ones_lane(proj[:, GQA_Q + 2 * GQA_KV:]).astype(BF16)
    rows = k.shape[0] // nseq
    for s in range(nseq):
        kt_ref[s] = k[s * rows:(s + 1) * rows, :].T.astype(BF16)


def _gqa_ctx_kernel(k_ref, v_ref, kt_ref, vo_ref):
    kt_ref[0] = k_ref[...].T.astype(BF16)
    vo_ref[...] = _with_ones_lane(v_ref[...]).astype(BF16)


def _gqa_in(x, x_off, n, mod3, layer, row_fn, seq, p, tables, ffn2_f32=None):
    rope = tables is not None
    nseq = max(TM // seq, 1)
    rows = TM // nseq
    batch = n // seq
    tiles_per_seq = max(seq // TM, 1)
    row = lambda w: pl.BlockSpec((TM, w), lambda i: (i, 0))
    in_specs = [pl.BlockSpec((TM, D_MODEL), lambda i: (i + x_off, 0)), _mod_spec(layer, 1, row_fn),
                _const_spec((1, D_MODEL)),
                _const_spec((D_MODEL, GQA_W_COLS)), _const_spec((1, GQA_Q)), _const_spec((1, GQA_KV))]
    args = [x, mod3, p["g_mix"], p["w"], p["g_q"], p["g_k"]]
    kt_spec = pl.BlockSpec((nseq, GQA_KV, rows), lambda i: (i // tiles_per_seq, 0, i % tiles_per_seq))
    out_specs = [row(GQA_Q), kt_spec, row(GQA_VPAD)]
    out_shape = [jax.ShapeDtypeStruct((n, GQA_Q), BF16), jax.ShapeDtypeStruct((batch, GQA_KV, seq), BF16),
                 jax.ShapeDtypeStruct((n, GQA_VPAD), BF16)]
    if rope:
        tab = pl.BlockSpec((TM, LANES), lambda i: (i % tiles_per_seq, 0))
        in_specs += [tab, tab, tab]
        args += list(tables)
    else:
        out_specs += [row(GQA_KV), row(GQA_KV)]
        out_shape += [jax.ShapeDtypeStruct((n, GQA_KV), F32), jax.ShapeDtypeStruct((n, GQA_KV), F32)]
        cast_in, cast_out, cast_shape = _cast_ffn2_specs(layer, n // TM, *ffn2_f32)
        in_specs += cast_in
        args += list(ffn2_f32)
        out_specs += cast_out
        out_shape += cast_shape
    return pl.pallas_call(
        functools.partial(_gqa_in_kernel, rope=rope, nseq=nseq),
        grid=(n // TM,),
        in_specs=in_specs,
        out_specs=out_specs,
        out_shape=out_shape,
        compiler_params=_params(1),
        name="gqa_in_rope" if rope else "gqa_in",
    )(*args)


def _gqa_ctx(k, vpad, seq):
    n = k.shape[0]
    return pl.pallas_call(
        _gqa_ctx_kernel,
        grid=(n // seq,),
        in_specs=[pl.BlockSpec((seq, GQA_KV), lambda i: (i, 0)), pl.BlockSpec((seq, GQA_VPAD), lambda i: (i, 0))],
        out_specs=[pl.BlockSpec((1, GQA_KV, seq), lambda i: (i, 0, 0)), pl.BlockSpec((seq, GQA_VPAD), lambda i: (i, 0))],
        out_shape=[jax.ShapeDtypeStruct((n // seq, GQA_KV, seq), BF16), jax.ShapeDtypeStruct((n, GQA_VPAD), BF16)],
        compiler_params=_params(1),
        name="gqa_ctx",
    )(k, vpad)


def _attn_kernel(*refs, nseg, bt, seq, nq, nkv, dq, dk, dv):
    q_ref = refs[0]
    segs = [(refs[1 + 2 * s], refs[2 + 2 * s]) for s in range(nseg)]
    o_ref = refs[1 + 2 * nseg]
    s_refs = refs[2 + 2 * nseg:4 + 2 * nseg]
    nks = [kt_ref.shape[2] for kt_ref, _ in segs]
    offs = [sum(nks[:s]) for s in range(nseg)]

    def scores(bi, r0, s_ref):
        rows = pl.ds(bi * seq + r0, Q_SUB)
        for j in range(nq):
            kv = j * nkv // nq
            qs = q_ref[rows, j * dq:(j + 1) * dq]
            for (kt_ref, _), off, nk in zip(segs, offs, nks):
                s_ref[j, :, off:off + nk] = jnp.dot(qs, kt_ref[bi, kv * dk:(kv + 1) * dk, :],
                                                    preferred_element_type=F32)

    def softmax_pv(bi, r0, s_ref):
        rows = pl.ds(bi * seq + r0, Q_SUB)
        outs = []
        for j in range(nq):
            kv = j * nkv // nq
            s = s_ref[j]
            p = jnp.exp2(s - jnp.max(s, axis=-1, keepdims=True)).astype(BF16)
            o = None
            for (_, v_ref), off, nk in zip(segs, offs, nks):
                pv = jnp.dot(p[:, off:off + nk], v_ref[bi * nk:(bi + 1) * nk, kv * HEAD_PAD:(kv + 1) * HEAD_PAD],
                             preferred_element_type=F32)
                o = pv if o is None else o + pv
            outs.append(o[:, :dv] / o[:, DENOM_LANE:DENOM_LANE + 1])
        o_ref[rows, :] = jnp.concatenate(outs, axis=1).astype(BF16)

    if seq == Q_SUB:
        assert bt == len(s_refs)
        for bi in range(bt):
            scores(bi, 0, s_refs[bi])
        for bi in range(bt):
            softmax_pv(bi, 0, s_refs[bi])
    else:
        n_sub = seq // Q_SUB
        assert bt == 1
        scores(0, 0, s_refs[0])
        for r in range(n_sub):
            if r + 1 < n_sub:
                scores(0, (r + 1) * Q_SUB, s_refs[(r + 1) % 2])
            softmax_pv(0, r * Q_SUB, s_refs[r % 2])


def _attention(q, segs, *, seq, heads, kv_heads, dq, dk, dv, bt, heads_per_step):
    n = q.shape[0]
    batch = n // seq
    nq = heads_per_step
    steps = heads // nq
    nkv = max(kv_heads // steps, 1)
    q_per_kv_block = steps // (kv_heads // nkv)
    in_specs = [pl.BlockSpec((bt * seq, nq * dq), lambda b, p: (b, p))]
    args = [q]
    for kt, v in segs:
        nk = kt.shape[2]
        in_specs.append(pl.BlockSpec((bt, nkv * dk, nk), lambda b, p: (b, p // q_per_kv_block, 0)))
        in_specs.append(pl.BlockSpec((bt * nk, nkv * HEAD_PAD), lambda b, p: (b, p // q_per_kv_block)))
        args += [kt, v]
    return pl.pallas_call(
        functools.partial(_attn_kernel, nseg=len(segs), bt=bt, seq=seq, nq=nq, nkv=nkv, dq=dq, dk=dk, dv=dv),
        grid=(batch // bt, steps),
        in_specs=in_specs,
        out_specs=pl.BlockSpec((bt * seq, nq * dv), lambda b, p: (b, p)),
        out_shape=jax.ShapeDtypeStruct((n, heads * dv), BF16),
        scratch_shapes=[pltpu.VMEM((nq, Q_SUB, sum(kt.shape[2] for kt, _ in segs)), F32)] * 2,
        compiler_params=_params(2),
        name="attention",
    )(*args)


SUBLANES = 8
CONV_LEAD = HALO - CONV_PAD
CONV_WIN = CONV_ROWS + 2 * HALO


def _fill_pad(pad_ref, glu_ref, prev_ref, next_ref, first, last, nseq):
    rows = TM // nseq
    zeros = jnp.zeros((HALO, CONV_CH), F32)
    for s in range(nseq):
        pad_ref[s, 0:HALO, :] = zeros if prev_ref is None else jnp.where(first, zeros, prev_ref[...])
        pad_ref[s, HALO + rows:2 * HALO + rows, :] = (
            zeros if next_ref is None else jnp.where(last, zeros, next_ref[...]))
        pad_ref[s, HALO:HALO + rows, :] = glu_ref[s * rows:(s + 1) * rows, :]


def _conv_chunk(pad_ref, s, r0, wdw_ref, bdw_ref, gln_ref, bln_ref, zero_ref):
    blocks = []
    zero = None
    for lb in range(CONV_CH // LANES):
        lanes = slice(lb * LANES, (lb + 1) * LANES)
        win = pad_ref[s, pl.ds(r0, CONV_WIN), lanes]
        acc = jnp.broadcast_to(bdw_ref[:, lanes], (CONV_ROWS, LANES))
        if zero is not None:
            acc = acc + jnp.concatenate([zero] * (CONV_ROWS // SUBLANES), axis=0)
        for phase in range(SUBLANES):
            shifted = win if phase == 0 else pltpu.roll(win, CONV_WIN - phase, 0)
            for a in range((CONV_W + CONV_LEAD) // SUBLANES + 1):
                k = a * SUBLANES + phase - CONV_LEAD
                if 0 <= k < CONV_W:
                    acc = acc + shifted[a * SUBLANES:a * SUBLANES + CONV_ROWS] * wdw_ref[k:k + 1, lanes]
        blocks.append(acc)
        peak = functools.reduce(jnp.maximum, [acc[g * SUBLANES:(g + 1) * SUBLANES]
                                              for g in range(CONV_ROWS // SUBLANES)])
        zero = (peak.astype(jnp.int32) & jnp.full((SUBLANES, LANES), zero_ref[0], jnp.int32)).astype(F32)
    acc = jnp.concatenate(blocks, axis=1)
    mu = jnp.mean(acc, axis=-1, keepdims=True)
    d = acc - mu
    var = jnp.mean(d * d, axis=-1, keepdims=True)
    return _silu(d * lax.rsqrt(var + EPS) * gln_ref[...] + bln_ref[...])


def _out_ffn_kernel(*refs, conv, nseq, halo, tiles_per_seq, final):
    it = iter(refs)
    x_ref, modm_ref, modf_ref, attn_ref = next(it), next(it), next(it), next(it)
    if conv:
        glu0_ref = next(it)
        next0_ref = next(it) if halo else None
        glun_ref = next(it)
        prevn_ref, nextn_ref = (next(it), next(it)) if halo else (None, None)
        conv_w = [next(it) for _ in range(4)]
        zero_ref = next(it)
    wout_ref, gff_ref, win_ref, wff_ref = next(it), next(it), next(it), next(it)
    gfin_ref = next(it) if final else None
    y_ref = next(it)
    i = pl.program_id(0)
    rows = TM // nseq

    if conv:
        pad_ref, g_ref = next(it), next(it)

        def conv_rows(c):
            s, r0 = divmod(c * CONV_ROWS, rows)
            g = _conv_chunk(pad_ref, s, r0, *conv_w, zero_ref)
            g_ref[c * CONV_ROWS:(c + 1) * CONV_ROWS, :] = g.astype(BF16)
            return jnp.max(g).astype(jnp.int32) & zero_ref[0]

        @pl.when(i == 0)
        def _():
            _fill_pad(pad_ref, glu0_ref, None, next0_ref, True, tiles_per_seq == 1, nseq)
            for c in range(TM // CONV_ROWS):
                conv_rows(c)

        out = (jnp.dot(attn_ref[...], wout_ref[:CONV_CH, :], preferred_element_type=F32)
               + jnp.dot(g_ref[...], wout_ref[CONV_CH:, :], preferred_element_type=F32))
        pos = (i + 1) % tiles_per_seq
        _fill_pad(pad_ref, glun_ref, prevn_ref, nextn_ref, pos == 0, pos == tiles_per_seq - 1, nseq)
        between = lambda j: conv_rows(j) if j < TM // CONV_ROWS else None
    else:
        out = jnp.dot(attn_ref[...], wout_ref[...], preferred_element_type=F32)
        between = None
    x = x_ref[...] + modm_ref[2:3, :] * out
    y = _ffn_body(x, modf_ref[...], gff_ref, win_ref, wff_ref, between)
    if final:
        y = _rms(y, gfin_ref[...])
    y_ref[...] = y


def _out_ffn(x, x_off, mod3, layer, row_fn, seq, attn, w_out, ffn_w, glu=None, conv_p=None, g_final=None):
    n = attn.shape[0]
    n_tiles = n // TM
    conv = glu is not None
    final = g_final is not None
    nseq = max(TM // seq, 1)
    tiles_per_seq = max(seq // TM, 1)
    halo = conv and tiles_per_seq > 1
    assert TM // CONV_ROWS <= D_FF // FF_CHUNK
    row = lambda w: pl.BlockSpec((TM, w), lambda i: (i, 0))
    in_specs = [pl.BlockSpec((TM, D_MODEL), lambda i: (i + x_off, 0)), _mod_spec(layer, 1, row_fn),
                _mod_spec(layer, 2, row_fn), row(attn.shape[1])]
    args = [x, mod3, mod3, attn]
    scratch = []
    if conv:
        per = TM // HALO
        last = n // HALO - 1
        nxt = lambda i: jnp.minimum(i + 1, n_tiles - 1)
        in_specs.append(pl.BlockSpec((TM, CONV_CH), lambda i: (0, 0)))
        args.append(glu)
        if halo:
            in_specs.append(pl.BlockSpec((HALO, CONV_CH), lambda i: (per, 0)))
            args.append(glu)
        in_specs.append(pl.BlockSpec((TM, CONV_CH), lambda i: (nxt(i), 0)))
        args.append(glu)
        if halo:
            in_specs += [pl.BlockSpec((HALO, CONV_CH), lambda i: (jnp.maximum(nxt(i) * per - 1, 0), 0)),
                         pl.BlockSpec((HALO, CONV_CH), lambda i: (jnp.minimum((nxt(i) + 1) * per, last), 0))]
            args += [glu, glu]
        in_specs += [_const_spec((CONV_W + 1, CONV_CH)), _const_spec((1, CONV_CH)), _const_spec((1, CONV_CH)),
                     _const_spec((1, CONV_CH)), pl.BlockSpec(memory_space=pltpu.SMEM)]
        args += [conv_p["w_dw"], conv_p["b_dw"], conv_p["g_ln"], conv_p["b_ln"], jnp.zeros((1,), jnp.int32)]
        scratch = [pltpu.VMEM((nseq, TM // nseq + 2 * HALO, CONV_CH), F32), pltpu.VMEM((TM, CONV_CH), BF16)]
    g_ff, w_in, w_ff = ffn_w
    in_specs += [_const_spec(w_out.shape), _ffn_weight_specs(layer)[0], _const_spec(w_in.shape),
                 _const_spec(w_ff.shape)]
    args += [w_out, g_ff.reshape(DEPTH, 1, D_MODEL), w_in, w_ff]
    if final:
        in_specs.append(_const_spec((1, D_MODEL)))
        args.append(g_final.reshape(1, D_MODEL))
    return pl.pallas_call(
        functools.partial(_out_ffn_kernel, conv=conv, nseq=nseq, halo=halo, tiles_per_seq=tiles_per_seq,
                          final=final),
        grid=(n_tiles,),
        in_specs=in_specs,
        out_specs=row(D_MODEL),
        out_shape=jax.ShapeDtypeStruct((n, D_MODEL), F32),
        scratch_shapes=scratch,
        compiler_params=pltpu.CompilerParams(dimension_semantics=("arbitrary",), vmem_limit_bytes=VMEM_LIMIT),
        name="out_ffn_conv" if conv else "out_ffn",
    )(*args)


def _rope_tables(d, offset, period):
    half = d // 2
    quarter = half // 2
    rows = DEC_SEQ // GRID_W
    pos_row = np.repeat(np.arange(rows), GRID_W)
    pos_col = np.tile(np.arange(GRID_W), rows)
    inv = ROPE_BASE ** (-np.arange(0, half, 2, dtype=np.float64) / half)
    ang_r = pos_row.astype(np.float64)[:, None] * inv[None, :]
    ang_c = pos_col.astype(np.float64)[:, None] * inv[None, :]
    zero = np.zeros((DEC_SEQ, quarter))
    cos = np.concatenate([np.cos(ang_r), np.cos(ang_r), np.cos(ang_c), np.cos(ang_c)], axis=1)
    sa = np.concatenate([-np.sin(ang_r), zero, -np.sin(ang_c), zero], axis=1)
    sb = np.concatenate([zero, np.sin(ang_r), zero, np.sin(ang_c)], axis=1)

    def embed(t, fill):
        blk = np.concatenate([np.full((DEC_SEQ, offset), fill), t,
                              np.full((DEC_SEQ, period - offset - d), fill)], axis=1)
        return jnp.asarray(np.tile(blk, (1, LANES // period)), dtype=F32)

    return embed(cos, 1.0), embed(sa, 0.0), embed(sb, 0.0)


def _pad_heads(w, heads, dim):
    k = w.shape[0]
    return jnp.pad(w.reshape(k, heads, dim), ((0, 0), (0, 0), (0, HEAD_PAD - dim))).reshape(k, heads * HEAD_PAD)


def _prep_mla(i, g_mix_l, w_in_a, g_q_lora, w_q_up, g_kv_lora, w_kv_up, w_dw, b_dw, g_conv_ln, b_conv_ln, w_out_a):
    w = w_in_a[i]
    o1 = Q_LORA
    o2 = o1 + KV_LORA
    o3 = o2 + QK_ROPE
    zeros = lambda c: jnp.zeros((D_MODEL, c), F32)
    w1 = jnp.concatenate([w[:, :o2], w[:, o3:], zeros(KR_LANE), w[:, o2:o3],
                          zeros(LANES - KR_LANE - QK_ROPE)], axis=1)
    kvu = w_kv_up[i].reshape(KV_LORA, MLA_HEADS, QK_NOPE + V_HEAD)
    w_kn = _pad_heads(kvu[:, :, :QK_NOPE].reshape(KV_LORA, MLA_HEADS * QK_NOPE), MLA_HEADS, QK_NOPE)
    w_v = _pad_heads(kvu[:, :, QK_NOPE:].reshape(KV_LORA, MLA_HEADS * V_HEAD), MLA_HEADS, V_HEAD)
    return dict(
        g_mix=g_mix_l.reshape(1, D_MODEL),
        w1=w1.astype(BF16),
        g_q=g_q_lora[i].reshape(1, Q_LORA),
        w_qu=_pad_heads(w_q_up[i], MLA_HEADS, QK_NOPE + QK_ROPE).astype(BF16),
        g_kv=g_kv_lora[i].reshape(1, KV_LORA),
        w_kv=jnp.concatenate([w_kn, w_v], axis=1).astype(BF16),
        w_dw=jnp.pad(w_dw[i], ((0, 1), (0, 0))),
        b_dw=b_dw[i].reshape(1, CONV_CH),
        g_ln=g_conv_ln[i].reshape(1, CONV_CH),
        b_ln=b_conv_ln[i].reshape(1, CONV_CH),
        w_out=w_out_a[i].astype(BF16),
    )


def _prep_gqa(i, g_mix_l, w_in_c, g_q_head, g_k_head, w_out_c):
    w = w_in_c[i]
    v_pad = _pad_heads(w[:, GQA_Q + GQA_KV:], GQA_KV_HEADS, GQA_HEAD_DIM)
    return dict(
        g_mix=g_mix_l.reshape(1, D_MODEL),
        w=jnp.concatenate([w, v_pad], axis=1).astype(BF16),
        g_q=jnp.tile(g_q_head[i], GQA_HEADS).reshape(1, GQA_Q),
        g_k=jnp.tile(g_k_head[i], GQA_KV_HEADS).reshape(1, GQA_KV),
        w_out=w_out_c[i].astype(BF16),
    )


SAMPLE_HEADS_PER_STEP = 4
PROMPT_SEQS_PER_STEP = 2


def _mixer_ffn(x_all, x_off, n, mod3, l, row_fn, seq, p, ctx, tables, g_ff2, ffn2_w, g_final):
    sample = ctx is not None
    in_proj, n_core = (_mla_in, 4) if l % 2 == 0 else (_gqa_in, 3)
    outs = in_proj(x_all, x_off, n, mod3, l, row_fn, seq, p, tables["mla" if l % 2 == 0 else "gqa"] if sample else None,
                   None if sample else ffn2_w)
    if not sample:
        outs, ffn2_w = outs[:-2], tuple(outs[-2:])
    ffn_w = (g_ff2,) + tuple(ffn2_w)
    if l % 2 == 0:
        q, kt, v, glu = outs[:4]
        segs = [(kt, v)]
        if sample:
            segs = [_mla_ctx(ctx["mla_ckv"], ctx["mla_krb"], p["w_kv"], ctx["past"])] + segs
            hps, bt = SAMPLE_HEADS_PER_STEP, 1
        else:
            hps, bt = MLA_HEADS, PROMPT_SEQS_PER_STEP
        attn = _attention(q, segs, seq=seq, heads=MLA_HEADS, kv_heads=MLA_HEADS, dq=HEAD_PAD, dk=HEAD_PAD,
                          dv=V_HEAD, bt=bt, heads_per_step=hps)
        return _out_ffn(x_all, x_off, mod3, l, row_fn, seq, attn, p["w_out"], ffn_w, glu=glu, conv_p=p,
                        g_final=g_final), outs[n_core:], ffn2_w
    q, kt, v = outs[:3]
    segs = [(kt, v)]
    if sample:
        segs = [_gqa_ctx(ctx["gqa_k"], ctx["gqa_vpad"], ctx["past"])] + segs
        hps, bt = SAMPLE_HEADS_PER_STEP, 1
    else:
        hps, bt = GQA_HEADS, PROMPT_SEQS_PER_STEP
    attn = _attention(q, segs, seq=seq, heads=GQA_HEADS, kv_heads=GQA_KV_HEADS, dq=GQA_HEAD_DIM,
                      dk=GQA_HEAD_DIM, dv=GQA_HEAD_DIM, bt=bt, heads_per_step=hps)
    return (_out_ffn(x_all, x_off, mod3, l, row_fn, seq, attn, p["w_out"], ffn_w, g_final=g_final), outs[n_core:],
            ffn2_w)


def kernel(x_prompt, x_sample, cache_mla_ckv, cache_mla_krope, cache_gqa_k, cache_gqa_v, c, c_ctx, g_ff1, w_ff1_in, w_ff1_out, g_mix, g_ff2, w_ff2_in, w_ff2_out, w_mod, b_mod, w_in_a, g_q_lora, w_q_up, g_kv_lora, w_kv_up, w_dw, b_dw, g_conv_ln, b_conv_ln, w_out_a, w_in_c, g_q_head, g_k_head, w_out_c, g_final):
    batch, seq, _ = x_prompt.shape
    dec_batch, dec_seq, _ = x_sample.shape
    past = cache_mla_ckv.shape[2]
    assert DEPTH == 2 and dec_seq == DEC_SEQ and 1 + dec_batch <= MOD_ROWS

    cond = jnp.concatenate([c_ctx[None, :], c, jnp.zeros((MOD_ROWS - 1 - dec_batch, D_MODEL), F32)], axis=0)
    mod3 = _modulation(cond, w_mod, b_mod)

    layers = [
        _prep_mla(0, g_mix[0], w_in_a, g_q_lora, w_q_up, g_kv_lora, w_kv_up, w_dw, b_dw, g_conv_ln, b_conv_ln,
                  w_out_a),
        _prep_gqa(0, g_mix[1], w_in_c, g_q_head, g_k_head, w_out_c),
    ]
    tables = dict(mla=_rope_tables(QK_ROPE, KR_LANE, HEAD_PAD), gqa=_rope_tables(GQA_HEAD_DIM, 0, GQA_HEAD_DIM))

    ctx = dict(
        past=past,
        mla_ckv=cache_mla_ckv[:, 0].reshape(dec_batch * past, KV_LORA),
        mla_krb=jnp.pad(cache_mla_krope[:, 0].reshape(dec_batch * past, QK_ROPE),
                        ((0, 0), (KR_LANE, LANES - KR_LANE - QK_ROPE))),
        gqa_k=cache_gqa_k[:, 0].reshape(dec_batch * past, GQA_KV),
        gqa_vpad=jnp.pad(cache_gqa_v[:, 0], ((0, 0), (0, 0), (0, 0), (0, HEAD_PAD - GQA_HEAD_DIM))
                         ).reshape(dec_batch * past, GQA_VPAD),
    )

    n_p, n_s = batch * seq, dec_batch * dec_seq
    nb_p = n_p // TM
    row_p = lambda i: 0
    row_s = lambda i: 1 + (i * TM) // DEC_SEQ
    row_all = lambda i: jnp.where(i < nb_p, 0, 1 + ((i - nb_p) * TM) // DEC_SEQ)

    parts = [x_prompt.reshape(n_p, D_MODEL), x_sample.reshape(n_s, D_MODEL)]
    saved = []
    for l in range(DEPTH):
        gf = g_final if l == DEPTH - 1 else None
        x_all = _ffn(parts, mod3, l, 0, row_all, g_ff1, w_ff1_in, w_ff1_out)
        xp, st, w2 = _mixer_ffn(x_all, 0, n_p, mod3, l, row_p, seq, layers[l], None, tables, g_ff2,
                                (w_ff2_in, w_ff2_out), gf)
        xs, _, _ = _mixer_ffn(x_all, nb_p, n_s, mod3, l, row_s, dec_seq, layers[l], ctx, tables, g_ff2, w2, gf)
        saved.append(st)
        parts = [xp, xs]
    y_prompt, y_sample = parts
    (ckv_new, kr_new), (k_new, v_new) = saved

    return (y_prompt.reshape(batch, seq, D_MODEL),
            y_sample.reshape(dec_batch, dec_seq, D_MODEL),
            ckv_new.reshape(batch, 1, seq, KV_LORA),
            kr_new.reshape(batch, 1, seq, QK_ROPE),
            k_new.reshape(batch, 1, seq, GQA_KV_HEADS, GQA_HEAD_DIM),
            v_new.reshape(batch, 1, seq, GQA_KV_HEADS, GQA_HEAD_DIM))
```

```python
import functools

import jax
import jax.numpy as jnp
import numpy as np
from jax import lax
from jax.experimental import pallas as pl
from jax.experimental.pallas import tpu as pltpu

F32 = jnp.float32
BF16 = jnp.bfloat16

D_MODEL = 1024
DEPTH = 2
DEC_SEQ = 2048
GRID_W = 64
MLA_HEADS = 8
Q_LORA = 384
KV_LORA = 256
QK_NOPE = 64
QK_ROPE = 32
V_HEAD = 64
CONV_CH = 512
CONV_W = 31
CONV_PAD = CONV_W // 2
GQA_HEADS = 16
GQA_KV_HEADS = 4
GQA_HEAD_DIM = 64
D_FF = 2816
MACARON = 0.5
N_MOD = 9
ROPE_BASE = 10000.0
EPS = 1e-6

LANES = 128
HEAD_PAD = 128
MOD_ROWS = 8
DENOM_LANE = 64
LOG2E = 1.4426950408889634
HALO = 16
TM = 512
Q_SUB = 256
FF_CHUNK = 256
CONV_ROWS = 64
INTERLEAVE_SLACK = 1
VMEM_LIMIT = 56 * 1024 * 1024


def _params(n_axes):
    return pltpu.CompilerParams(dimension_semantics=("parallel",) * n_axes, vmem_limit_bytes=VMEM_LIMIT)


def _const_spec(shape):
    return pl.BlockSpec(shape, lambda *_: (0,) * len(shape), pipeline_mode=pl.Buffered(1))


def _rms(x, g):
    return x * lax.rsqrt(jnp.mean(x * x, axis=-1, keepdims=True) + EPS) * g


def _mod_norm(x, g, mod):
    return _rms(x, g) * (1.0 + mod[1:2]) + mod[0:1]


def _silu(x):
    return x * jax.nn.sigmoid(x)


def _rope(x, cos, sa, sb, quarter):
    w = x.shape[-1]
    return x * cos + pltpu.roll(x, w - quarter, 1) * sa + pltpu.roll(x, quarter, 1) * sb


def _with_ones_lane(v):
    lane = lax.broadcasted_iota(jnp.int32, v.shape, 1)
    return jnp.where(lane % HEAD_PAD == DENOM_LANE, 1.0, v)


def _tile_lanes(t, reps):
    return t if reps == 1 else jnp.concatenate([t] * reps, axis=1)


def _mod_kernel(c_ref, w_ref, b_ref, o_ref):
    e = _silu(c_ref[...]).astype(BF16)
    o_ref[...] = jnp.dot(e, w_ref[...].astype(BF16), preferred_element_type=F32) + b_ref[...]


def _modulation(cond, w_mod, b_mod):
    n_out = N_MOD * D_MODEL
    tn = 1536
    out = pl.pallas_call(
        _mod_kernel,
        grid=(DEPTH, n_out // tn),
        in_specs=[
            pl.BlockSpec((MOD_ROWS, D_MODEL), lambda l, j: (0, 0)),
            pl.BlockSpec((None, D_MODEL, tn), lambda l, j: (l, 0, j)),
            pl.BlockSpec((None, 1, tn), lambda l, j: (l, 0, j)),
        ],
        out_specs=pl.BlockSpec((None, MOD_ROWS, tn), lambda l, j: (l, 0, j)),
        out_shape=jax.ShapeDtypeStruct((DEPTH, MOD_ROWS, n_out), F32),
        compiler_params=_params(2),
        name="modulation",
    )(cond, w_mod, b_mod.reshape(DEPTH, 1, n_out))
    return out.reshape(DEPTH * MOD_ROWS * 3, 3, D_MODEL)


def _mod_spec(layer, sub, row_fn):
    return pl.BlockSpec((None, 3, D_MODEL), lambda i, *_: ((layer * MOD_ROWS + row_fn(i)) * 3 + sub, 0, 0))


def _cast_ffn2_slabs(wi_ref, wo_ref, wi_out, wo_out):
    wi_out[...] = wi_ref[...].astype(BF16)
    wo_out[...] = wo_ref[...].astype(BF16)


def _cast_ffn2_specs(layer, steps, w_in, w_out):
    shapes = [w_in.shape[1:], w_out.shape[1:]]
    assert all(k % (steps * 16) == 0 for k, _ in shapes)
    in_specs = [pl.BlockSpec((None, k // steps, n), lambda i: (layer, i, 0)) for k, n in shapes]
    out_specs = [pl.BlockSpec((k // steps, n), lambda i: (i, 0)) for k, n in shapes]
    return in_specs, out_specs, [jax.ShapeDtypeStruct(s, BF16) for s in shapes]


def _ffn_body(x, mod, g_ref, win_ref, wout_ref, between_chunks=None):
    h = _mod_norm(x, g_ref[...], mod).astype(BF16)
    acc = jnp.zeros(x.shape, F32)
    zeros = [None] * INTERLEAVE_SLACK
    for j in range(D_FF // FF_CHUNK):
        lo = j * FF_CHUNK
        a = jnp.dot(h, win_ref[:, lo:lo + FF_CHUNK].astype(BF16), preferred_element_type=F32)
        b = jnp.dot(h, win_ref[:, D_FF + lo:D_FF + lo + FF_CHUNK].astype(BF16), preferred_element_type=F32)
        act = (_silu(a) * b).astype(BF16)
        zero = zeros.pop(0)
        rows = pl.ds(lo, FF_CHUNK) if zero is None else pl.ds(pl.multiple_of(lo + zero, FF_CHUNK), FF_CHUNK)
        acc = acc + jnp.dot(act, wout_ref[rows, :].astype(BF16), preferred_element_type=F32)
        zeros.append(between_chunks(j) if between_chunks is not None else None)
    return x + (MACARON * mod[2:3]) * acc


def _ffn_kernel(*refs, nparts, nblk0):
    x_refs = refs[:nparts]
    mod_ref, g_ref, win_ref, wout_ref, o_ref = refs[nparts:]
    x = x_refs[0][...]
    if nparts == 2:
        x = jnp.where(pl.program_id(0) < nblk0, x, x_refs[1][...])
    o_ref[...] = _ffn_body(x, mod_ref[...], g_ref, win_ref, wout_ref)


def _ffn_weight_specs(layer):
    layer_spec = lambda shape: pl.BlockSpec((None,) + shape, lambda i: (layer, 0, 0), pipeline_mode=pl.Buffered(1))
    return [layer_spec((1, D_MODEL)), layer_spec((D_MODEL, 2 * D_FF)), layer_spec((D_FF, D_MODEL))]


def _ffn(parts, mod3, layer, sub, row_fn, g, w_in, w_out):
    nblks = [p.shape[0] // TM for p in parts]
    if len(parts) == 1:
        in_specs = [pl.BlockSpec((TM, D_MODEL), lambda i: (i, 0))]
    else:
        nb0 = nblks[0]
        in_specs = [pl.BlockSpec((TM, D_MODEL), lambda i: (jnp.minimum(i, nb0 - 1), 0)),
                    pl.BlockSpec((TM, D_MODEL), lambda i: (jnp.maximum(i - nb0, 0), 0))]
    in_specs += [_mod_spec(layer, sub, row_fn)] + _ffn_weight_specs(layer)
    n = sum(nblks) * TM
    return pl.pallas_call(
        functools.partial(_ffn_kernel, nparts=len(parts), nblk0=nblks[0]),
        grid=(n // TM,),
        in_specs=in_specs,
        out_specs=pl.BlockSpec((TM, D_MODEL), lambda i: (i, 0)),
        out_shape=jax.ShapeDtypeStruct((n, D_MODEL), F32),
        compiler_params=_params(1),
        name="ffn",
    )(*parts, mod3, g.reshape(DEPTH, 1, D_MODEL), w_in, w_out)


MLA_W1_COLS = Q_LORA + KV_LORA + 2 * CONV_CH + LANES
MLA_QK = MLA_HEADS * HEAD_PAD
KR_LANE = QK_NOPE


def _mla_kv(ckvn, krb, wkv_ref, kt_ref, v_ref, nseq):
    kv = jnp.dot(ckvn.astype(BF16), wkv_ref[...], preferred_element_type=F32)
    k = kv[:, :MLA_QK] + _tile_lanes(krb, MLA_HEADS)
    v_ref[...] = _with_ones_lane(kv[:, MLA_QK:]).astype(BF16)
    rows = k.shape[0] // nseq
    for s in range(nseq):
        kt_ref[s] = k[s * rows:(s + 1) * rows, :].T.astype(BF16)


def _mla_in_kernel(x_ref, mod_ref, g_ref, w1_ref, gq_ref, wqu_ref, gkv_ref, wkv_ref, *rest, rope, nseq):
    if rope:
        cos_ref, sa_ref, sb_ref, q_ref, kt_ref, v_ref, glu_ref = rest
    else:
        wi_ref, wo_ref, q_ref, kt_ref, v_ref, glu_ref, ckv_ref, kr_ref, wi_out, wo_out = rest
        _cast_ffn2_slabs(wi_ref, wo_ref, wi_out, wo_out)
    h = _mod_norm(x_ref[...], g_ref[...], mod_ref[...]).astype(BF16)
    proj = jnp.dot(h, w1_ref[...], preferred_element_type=F32)
    o1 = Q_LORA
    o2 = o1 + KV_LORA
    o3 = o2 + CONV_CH
    o4 = o3 + CONV_CH
    cq, ckv, ua, ub, krb = proj[:, :o1], proj[:, o1:o2], proj[:, o2:o3], proj[:, o3:o4], proj[:, o4:]
    glu_ref[...] = ua * jax.nn.sigmoid(ub)
    q = jnp.dot(_rms(cq, gq_ref[...]).astype(BF16), wqu_ref[...], preferred_element_type=F32)
    ckvn = _rms(ckv, gkv_ref[...])
    if rope:
        cos, sa, sb = cos_ref[...], sa_ref[...], sb_ref[...]
        quarter = QK_ROPE // 4
        krb = _rope(krb, cos, sa, sb, quarter)
        q = _rope(q, _tile_lanes(cos, MLA_HEADS), _tile_lanes(sa, MLA_HEADS), _tile_lanes(sb, MLA_HEADS), quarter)
    else:
        ckv_ref[...] = ckvn
        kr_ref[...] = krb[:, KR_LANE:KR_LANE + QK_ROPE]
    q_ref[...] = (q * ((QK_NOPE + QK_ROPE) ** -0.5 * LOG2E)).astype(BF16)
    _mla_kv(ckvn, krb, wkv_ref, kt_ref, v_ref, nseq)


def _mla_ctx_kernel(ckv_ref, krb_ref, wkv_ref, kt_ref, v_ref):
    _mla_kv(ckv_ref[...], krb_ref[...], wkv_ref, kt_ref, v_ref, 1)


def _mla_in(x, x_off, n, mod3, layer, row_fn, seq, p, tables, ffn2_f32=None):
    rope = tables is not None
    nseq = max(TM // seq, 1)
    rows = TM // nseq
    batch = n // seq
    tiles_per_seq = max(seq // TM, 1)
    row = lambda w: pl.BlockSpec((TM, w), lambda i: (i, 0))
    in_specs = [pl.BlockSpec((TM, D_MODEL), lambda i: (i + x_off, 0)), _mod_spec(layer, 1, row_fn),
                _const_spec((1, D_MODEL)),
                _const_spec((D_MODEL, MLA_W1_COLS)), _const_spec((1, Q_LORA)), _const_spec((Q_LORA, MLA_QK)),
                _const_spec((1, KV_LORA)), _const_spec((KV_LORA, 2 * MLA_QK))]
    args = [x, mod3, p["g_mix"], p["w1"], p["g_q"], p["w_qu"], p["g_kv"], p["w_kv"]]
    kt_spec = pl.BlockSpec((nseq, MLA_QK, rows), lambda i: (i // tiles_per_seq, 0, i % tiles_per_seq))
    out_specs = [row(MLA_QK), kt_spec, row(MLA_QK), row(CONV_CH)]
    out_shape = [jax.ShapeDtypeStruct((n, MLA_QK), BF16), jax.ShapeDtypeStruct((batch, MLA_QK, seq), BF16),
                 jax.ShapeDtypeStruct((n, MLA_QK), BF16), jax.ShapeDtypeStruct((n, CONV_CH), F32)]
    if rope:
        tab = pl.BlockSpec((TM, LANES), lambda i: (i % tiles_per_seq, 0))
        in_specs += [tab, tab, tab]
        args += list(tables)
    else:
        out_specs += [row(KV_LORA), row(QK_ROPE)]
        out_shape += [jax.ShapeDtypeStruct((n, KV_LORA), F32), jax.ShapeDtypeStruct((n, QK_ROPE), F32)]
        cast_in, cast_out, cast_shape = _cast_ffn2_specs(layer, n // TM, *ffn2_f32)
        in_specs += cast_in
        args += list(ffn2_f32)
        out_specs += cast_out
        out_shape += cast_shape
    return pl.pallas_call(
        functools.partial(_mla_in_kernel, rope=rope, nseq=nseq),
        grid=(n // TM,),
        in_specs=in_specs,
        out_specs=out_specs,
        out_shape=out_shape,
        compiler_params=_params(1),
        name="mla_in_rope" if rope else "mla_in",
    )(*args)


def _mla_ctx(ckv, krb, w_kv, seq):
    n = ckv.shape[0]
    return pl.pallas_call(
        _mla_ctx_kernel,
        grid=(n // seq,),
        in_specs=[pl.BlockSpec((seq, KV_LORA), lambda i: (i, 0)), pl.BlockSpec((seq, LANES), lambda i: (i, 0)),
                  _const_spec((KV_LORA, 2 * MLA_QK))],
        out_specs=[pl.BlockSpec((1, MLA_QK, seq), lambda i: (i, 0, 0)), pl.BlockSpec((seq, MLA_QK), lambda i: (i, 0))],
        out_shape=[jax.ShapeDtypeStruct((n // seq, MLA_QK, seq), BF16), jax.ShapeDtypeStruct((n, MLA_QK), BF16)],
        compiler_params=_params(1),
        name="mla_ctx",
    )(ckv, krb, w_kv)


GQA_Q = GQA_HEADS * GQA_HEAD_DIM
GQA_KV = GQA_KV_HEADS * GQA_HEAD_DIM
GQA_VPAD = GQA_KV_HEADS * HEAD_PAD
GQA_W_COLS = GQA_Q + 2 * GQA_KV + GQA_VPAD


def _head_rms(x, g):
    tm, w = x.shape
    lo = lax.broadcasted_iota(jnp.int32, (tm, LANES), 1) < GQA_HEAD_DIM
    outs = []
    for b in range(w // LANES):
        xb = x[:, b * LANES:(b + 1) * LANES]
        sq = xb * xb
        s_lo = jnp.sum(jnp.where(lo, sq, 0.0), axis=-1, keepdims=True)
        s_hi = jnp.sum(jnp.where(lo, 0.0, sq), axis=-1, keepdims=True)
        ms = jnp.where(lo, s_lo, s_hi) * (1.0 / GQA_HEAD_DIM)
        outs.append(xb * lax.rsqrt(ms + EPS))
    return jnp.concatenate(outs, axis=1) * g


def _gqa_in_kernel(x_ref, mod_ref, g_ref, w_ref, gq_ref, gk_ref, *rest, rope, nseq):
    if rope:
        cos_ref, sa_ref, sb_ref, q_ref, kt_ref, v_ref = rest
    else:
        wi_ref, wo_ref, q_ref, kt_ref, v_ref, kc_ref, vc_ref, wi_out, wo_out = rest
        _cast_ffn2_slabs(wi_ref, wo_ref, wi_out, wo_out)
    h = _mod_norm(x_ref[...], g_ref[...], mod_ref[...]).astype(BF16)
    proj = jnp.dot(h, w_ref[...], preferred_element_type=F32)
    q = _head_rms(proj[:, :GQA_Q], gq_ref[...])
    k = _head_rms(proj[:, GQA_Q:GQA_Q + GQA_KV], gk_ref[...])
    if rope:
        cos, sa, sb = cos_ref[...], sa_ref[...], sb_ref[...]
        quarter = GQA_HEAD_DIM // 4
        q = _rope(q, _tile_lanes(cos, GQA_Q // LANES), _tile_lanes(sa, GQA_Q // LANES),
                  _tile_lanes(sb, GQA_Q // LANES), quarter)
        k = _rope(k, _tile_lanes(cos, GQA_KV // LANES), _tile_lanes(sa, GQA_KV // LANES),
                  _tile_lanes(sb, GQA_KV // LANES), quarter)
    else:
        kc_ref[...] = k
        vc_ref[...] = proj[:, GQA_Q + GQA_KV:GQA_Q + 2 * GQA_KV]
    q_ref[...] = (q * (GQA_HEAD_DIM ** -0.5 * LOG2E)).astype(BF16)
    v_ref[...] = _with_ones_lane(proj[:, GQA_Q + 2 * GQA_KV:]).astype(BF16)
    rows = k.shape[0] // nseq
    for s in range(nseq):
        kt_ref[s] = k[s * rows:(s + 1) * rows, :].T.astype(BF16)


def _gqa_ctx_kernel(k_ref, v_ref, kt_ref, vo_ref):
    kt_ref[0] = k_ref[...].T.astype(BF16)
    vo_ref[...] = _with_ones_lane(v_ref[...]).astype(BF16)


def _gqa_in(x, x_off, n, mod3, layer, row_fn, seq, p, tables, ffn2_f32=None):
    rope = tables is not None
    nseq = max(TM // seq, 1)
    rows = TM // nseq
    batch = n // seq
    tiles_per_seq = max(seq // TM, 1)
    row = lambda w: pl.BlockSpec((TM, w), lambda i: (i, 0))
    in_specs = [pl.BlockSpec((TM, D_MODEL), lambda i: (i + x_off, 0)), _mod_spec(layer, 1, row_fn),
                _const_spec((1, D_MODEL)),
                _const_spec((D_MODEL, GQA_W_COLS)), _const_spec((1, GQA_Q)), _const_spec((1, GQA_KV))]
    args = [x, mod3, p["g_mix"], p["w"], p["g_q"], p["g_k"]]
    kt_spec = pl.BlockSpec((nseq, GQA_KV, rows), lambda i: (i // tiles_per_seq, 0, i % tiles_per_seq))
    out_specs = [row(GQA_Q), kt_spec, row(GQA_VPAD)]
    out_shape = [jax.ShapeDtypeStruct((n, GQA_Q), BF16), jax.ShapeDtypeStruct((batch, GQA_KV, seq), BF16),
                 jax.ShapeDtypeStruct((n, GQA_VPAD), BF16)]
    if rope:
        tab = pl.BlockSpec((TM, LANES), lambda i: (i % tiles_per_seq, 0))
        in_specs += [tab, tab, tab]
        args += list(tables)
    else:
        out_specs += [row(GQA_KV), row(GQA_KV)]
        out_shape += [jax.ShapeDtypeStruct((n, GQA_KV), F32), jax.ShapeDtypeStruct((n, GQA_KV), F32)]
        cast_in, cast_out, cast_shape = _cast_ffn2_specs(layer, n // TM, *ffn2_f32)
        in_specs += cast_in
        args += list(ffn2_f32)
        out_specs += cast_out
        out_shape += cast_shape
    return pl.pallas_call(
        functools.partial(_gqa_in_kernel, rope=rope, nseq=nseq),
        grid=(n // TM,),
        in_specs=in_specs,
        out_specs=out_specs,
        out_shape=out_shape,
        compiler_params=_params(1),
        name="gqa_in_rope" if rope else "gqa_in",
    )(*args)


def _gqa_ctx(k, vpad, seq):
    n = k.shape[0]
    return pl.pallas_call(
        _gqa_ctx_kernel,
        grid=(n // seq,),
        in_specs=[pl.BlockSpec((seq, GQA_KV), lambda i: (i, 0)), pl.BlockSpec((seq, GQA_VPAD), lambda i: (i, 0))],
        out_specs=[pl.BlockSpec((1, GQA_KV, seq), lambda i: (i, 0, 0)), pl.BlockSpec((seq, GQA_VPAD), lambda i: (i, 0))],
        out_shape=[jax.ShapeDtypeStruct((n // seq, GQA_KV, seq), BF16), jax.ShapeDtypeStruct((n, GQA_VPAD), BF16)],
        compiler_params=_params(1),
        name="gqa_ctx",
    )(k, vpad)


def _attn_kernel(*refs, nseg, bt, seq, nq, nkv, dq, dk, dv):
    q_ref = refs[0]
    segs = [(refs[1 + 2 * s], refs[2 + 2 * s]) for s in range(nseg)]
    o_ref = refs[1 + 2 * nseg]
    s_refs = refs[2 + 2 * nseg:4 + 2 * nseg]
    nks = [kt_ref.shape[2] for kt_ref, _ in segs]
    offs = [sum(nks[:s]) for s in range(nseg)]

    def scores(bi, r0, s_ref):
        rows = pl.ds(bi * seq + r0, Q_SUB)
        for j in range(nq):
            kv = j * nkv // nq
            qs = q_ref[rows, j * dq:(j + 1) * dq]
            for (kt_ref, _), off, nk in zip(segs, offs, nks):
                s_ref[j, :, off:off + nk] = jnp.dot(qs, kt_ref[bi, kv * dk:(kv + 1) * dk, :],
                                                    preferred_element_type=F32)

    def softmax_pv(bi, r0, s_ref):
        rows = pl.ds(bi * seq + r0, Q_SUB)
        outs = []
        for j in range(nq):
            kv = j * nkv // nq
            s = s_ref[j]
            p = jnp.exp2(s - jnp.max(s, axis=-1, keepdims=True)).astype(BF16)
            o = None
            for (_, v_ref), off, nk in zip(segs, offs, nks):
                pv = jnp.dot(p[:, off:off + nk], v_ref[bi * nk:(bi + 1) * nk, kv * HEAD_PAD:(kv + 1) * HEAD_PAD],
                             preferred_element_type=F32)
                o = pv if o is None else o + pv
            outs.append(o[:, :dv] / o[:, DENOM_LANE:DENOM_LANE + 1])
        o_ref[rows, :] = jnp.concatenate(outs, axis=1).astype(BF16)

    if seq == Q_SUB:
        assert bt == len(s_refs)
        for bi in range(bt):
            scores(bi, 0, s_refs[bi])
        for bi in range(bt):
            softmax_pv(bi, 0, s_refs[bi])
    else:
        n_sub = seq // Q_SUB
        assert bt == 1
        scores(0, 0, s_refs[0])
        for r in range(n_sub):
            if r + 1 < n_sub:
                scores(0, (r + 1) * Q_SUB, s_refs[(r + 1) % 2])
            softmax_pv(0, r * Q_SUB, s_refs[r % 2])


def _attention(q, segs, *, seq, heads, kv_heads, dq, dk, dv, bt, heads_per_step):
    n = q.shape[0]
    batch = n // seq
    nq = heads_per_step
    steps = heads // nq
    nkv = max(kv_heads // steps, 1)
    q_per_kv_block = steps // (kv_heads // nkv)
    in_specs = [pl.BlockSpec((bt * seq, nq * dq), lambda b, p: (b, p))]
    args = [q]
    for kt, v in segs:
        nk = kt.shape[2]
        in_specs.append(pl.BlockSpec((bt, nkv * dk, nk), lambda b, p: (b, p // q_per_kv_block, 0)))
        in_specs.append(pl.BlockSpec((bt * nk, nkv * HEAD_PAD), lambda b, p: (b, p // q_per_kv_block)))
        args += [kt, v]
    return pl.pallas_call(
        functools.partial(_attn_kernel, nseg=len(segs), bt=bt, seq=seq, nq=nq, nkv=nkv, dq=dq, dk=dk, dv=dv),
        grid=(batch // bt, steps),
        in_specs=in_specs,
        out_specs=pl.BlockSpec((bt * seq, nq * dv), lambda b, p: (b, p)),
        out_shape=jax.ShapeDtypeStruct((n, heads * dv), BF16),
        scratch_shapes=[pltpu.VMEM((nq, Q_SUB, sum(kt.shape[2] for kt, _ in segs)), F32)] * 2,
        compiler_params=_params(2),
        name="attention",
    )(*args)


SUBLANES = 8
CONV_LEAD = HALO - CONV_PAD
CONV_WIN = CONV_ROWS + 2 * HALO


def _fill_pad(pad_ref, glu_ref, prev_ref, next_ref, first, last, nseq):
    rows = TM // nseq
    zeros = jnp.zeros((HALO, CONV_CH), F32)
    for s in range(nseq):
        pad_ref[s, 0:HALO, :] = zeros if prev_ref is None else jnp.where(first, zeros, prev_ref[...])
        pad_ref[s, HALO + rows:2 * HALO + rows, :] = (
            zeros if next_ref is None else jnp.where(last, zeros, next_ref[...]))
        pad_ref[s, HALO:HALO + rows, :] = glu_ref[s * rows:(s + 1) * rows, :]


CONV_LANE_BLOCKS = CONV_CH // LANES


def _conv_block(pad_ref, s, r0, lb, wdw_ref, bdw_ref, zero_ref, prev_zero):
    lanes = slice(lb * LANES, (lb + 1) * LANES)
    win = pad_ref[s, pl.ds(r0, CONV_WIN), lanes]
    acc = jnp.broadcast_to(bdw_ref[:, lanes], (CONV_ROWS, LANES))
    if prev_zero is not None:
        acc = acc + jnp.concatenate([prev_zero] * (CONV_ROWS // SUBLANES), axis=0)
    for phase in range(SUBLANES):
        shifted = win if phase == 0 else pltpu.roll(win, CONV_WIN - phase, 0)
        for a in range((CONV_W + CONV_LEAD) // SUBLANES + 1):
            k = a * SUBLANES + phase - CONV_LEAD
            if 0 <= k < CONV_W:
                acc = acc + shifted[a * SUBLANES:a * SUBLANES + CONV_ROWS] * wdw_ref[k:k + 1, lanes]
    peak = functools.reduce(jnp.maximum, [acc[g * SUBLANES:(g + 1) * SUBLANES]
                                          for g in range(CONV_ROWS // SUBLANES)])
    zero = (peak.astype(jnp.int32) & jnp.full((SUBLANES, LANES), zero_ref[0], jnp.int32)).astype(F32)
    return acc, zero


def _conv_finish(blocks, gln_ref, bln_ref):
    acc = jnp.concatenate(blocks, axis=1)
    mu = jnp.mean(acc, axis=-1, keepdims=True)
    d = acc - mu
    var = jnp.mean(d * d, axis=-1, keepdims=True)
    return _silu(d * lax.rsqrt(var + EPS) * gln_ref[...] + bln_ref[...])


def _out_ffn_kernel(*refs, conv, nseq, halo, tiles_per_seq, final):
    it = iter(refs)
    x_ref, modm_ref, modf_ref, attn_ref = next(it), next(it), next(it), next(it)
    if conv:
        glu0_ref = next(it)
        next0_ref = next(it) if halo else None
        glun_ref = next(it)
        prevn_ref, nextn_ref = (next(it), next(it)) if halo else (None, None)
        conv_w = [next(it) for _ in range(4)]
        zero_ref = next(it)
    wout_ref, gff_ref, win_ref, wff_ref = next(it), next(it), next(it), next(it)
    gfin_ref = next(it) if final else None
    y_ref = next(it)
    i = pl.program_id(0)
    rows = TM // nseq

    if conv:
        pad_ref, g_ref = next(it), next(it)

        wdw_ref, bdw_ref, gln_ref, bln_ref = conv_w
        units = [(c, lb) for c in range(TM // CONV_ROWS) for lb in range(CONV_LANE_BLOCKS)]
        hosts = D_FF // FF_CHUNK - INTERLEAVE_SLACK
        counts = [len(units) // hosts + (j < len(units) % hosts) for j in range(hosts)]
        starts = [sum(counts[:j]) for j in range(hosts + 1)]

        def run_units(todo, state):
            for c, lb in todo:
                s, r0 = divmod(c * CONV_ROWS, rows)
                acc, state[0] = _conv_block(pad_ref, s, r0, lb, wdw_ref, bdw_ref, zero_ref, state[0])
                state[1].append(acc)
                if lb == CONV_LANE_BLOCKS - 1:
                    g = _conv_finish(state[1], gln_ref, bln_ref)
                    g_ref[c * CONV_ROWS:(c + 1) * CONV_ROWS, :] = g.astype(BF16)
                    state[1] = []

        @pl.when(i == 0)
        def _():
            _fill_pad(pad_ref, glu0_ref, None, next0_ref, True, tiles_per_seq == 1, nseq)
            run_units(units, [None, []])

        out = (jnp.dot(attn_ref[...], wout_ref[:CONV_CH, :], preferred_element_type=F32)
               + jnp.dot(g_ref[...], wout_ref[CONV_CH:, :], preferred_element_type=F32))
        pos = (i + 1) % tiles_per_seq
        _fill_pad(pad_ref, glun_ref, prevn_ref, nextn_ref, pos == 0, pos == tiles_per_seq - 1, nseq)
        state = [None, []]

        def between(j):
            if j >= hosts:
                return None
            todo = units[starts[j]:starts[j + 1]]
            run_units(todo, state)
            return jnp.max(state[0]).astype(jnp.int32)
    else:
        out = jnp.dot(attn_ref[...], wout_ref[...], preferred_element_type=F32)
        between = None
    x = x_ref[...] + modm_ref[2:3, :] * out
    y = _ffn_body(x, modf_ref[...], gff_ref, win_ref, wff_ref, between)
    if final:
        y = _rms(y, gfin_ref[...])
    y_ref[...] = y


def _out_ffn(x, x_off, mod3, layer, row_fn, seq, attn, w_out, ffn_w, glu=None, conv_p=None, g_final=None):
    n = attn.shape[0]
    n_tiles = n // TM
    conv = glu is not None
    final = g_final is not None
    nseq = max(TM // seq, 1)
    tiles_per_seq = max(seq // TM, 1)
    halo = conv and tiles_per_seq > 1
    row = lambda w: pl.BlockSpec((TM, w), lambda i: (i, 0))
    in_specs = [pl.BlockSpec((TM, D_MODEL), lambda i: (i + x_off, 0)), _mod_spec(layer, 1, row_fn),
                _mod_spec(layer, 2, row_fn), row(attn.shape[1])]
    args = [x, mod3, mod3, attn]
    scratch = []
    if conv:
        per = TM // HALO
        last = n // HALO - 1
        nxt = lambda i: jnp.minimum(i + 1, n_tiles - 1)
        in_specs.append(pl.BlockSpec((TM, CONV_CH), lambda i: (0, 0)))
        args.append(glu)
        if halo:
            in_specs.append(pl.BlockSpec((HALO, CONV_CH), lambda i: (per, 0)))
            args.append(glu)
        in_specs.append(pl.BlockSpec((TM, CONV_CH), lambda i: (nxt(i), 0)))
        args.append(glu)
        if halo:
            in_specs += [pl.BlockSpec((HALO, CONV_CH), lambda i: (jnp.maximum(nxt(i) * per - 1, 0), 0)),
                         pl.BlockSpec((HALO, CONV_CH), lambda i: (jnp.minimum((nxt(i) + 1) * per, last), 0))]
            args += [glu, glu]
        in_specs += [_const_spec((CONV_W + 1, CONV_CH)), _const_spec((1, CONV_CH)), _const_spec((1, CONV_CH)),
                     _const_spec((1, CONV_CH)), pl.BlockSpec(memory_space=pltpu.SMEM)]
        args += [conv_p["w_dw"], conv_p["b_dw"], conv_p["g_ln"], conv_p["b_ln"], jnp.zeros((1,), jnp.int32)]
        scratch = [pltpu.VMEM((nseq, TM // nseq + 2 * HALO, CONV_CH), F32), pltpu.VMEM((TM, CONV_CH), BF16)]
    g_ff, w_in, w_ff = ffn_w
    in_specs += [_const_spec(w_out.shape), _ffn_weight_specs(layer)[0], _const_spec(w_in.shape),
                 _const_spec(w_ff.shape)]
    args += [w_out, g_ff.reshape(DEPTH, 1, D_MODEL), w_in, w_ff]
    if final:
        in_specs.append(_const_spec((1, D_MODEL)))
        args.append(g_final.reshape(1, D_MODEL))
    return pl.pallas_call(
        functools.partial(_out_ffn_kernel, conv=conv, nseq=nseq, halo=halo, tiles_per_seq=tiles_per_seq,
                          final=final),
        grid=(n_tiles,),
        in_specs=in_specs,
        out_specs=row(D_MODEL),
        out_shape=jax.ShapeDtypeStruct((n, D_MODEL), F32),
        scratch_shapes=scratch,
        compiler_params=pltpu.CompilerParams(dimension_semantics=("arbitrary",), vmem_limit_bytes=VMEM_LIMIT),
        name="out_ffn_conv" if conv else "out_ffn",
    )(*args)


def _rope_tables(d, offset, period):
    half = d // 2
    quarter = half // 2
    rows = DEC_SEQ // GRID_W
    pos_row = np.repeat(np.arange(rows), GRID_W)
    pos_col = np.tile(np.arange(GRID_W), rows)
    inv = ROPE_BASE ** (-np.arange(0, half, 2, dtype=np.float64) / half)
    ang_r = pos_row.astype(np.float64)[:, None] * inv[None, :]
    ang_c = pos_col.astype(np.float64)[:, None] * inv[None, :]
    zero = np.zeros((DEC_SEQ, quarter))
    cos = np.concatenate([np.cos(ang_r), np.cos(ang_r), np.cos(ang_c), np.cos(ang_c)], axis=1)
    sa = np.concatenate([-np.sin(ang_r), zero, -np.sin(ang_c), zero], axis=1)
    sb = np.concatenate([zero, np.sin(ang_r), zero, np.sin(ang_c)], axis=1)

    def embed(t, fill):
        blk = np.concatenate([np.full((DEC_SEQ, offset), fill), t,
                              np.full((DEC_SEQ, period - offset - d), fill)], axis=1)
        return jnp.asarray(np.tile(blk, (1, LANES // period)), dtype=F32)

    return embed(cos, 1.0), embed(sa, 0.0), embed(sb, 0.0)


def _pad_heads(w, heads, dim):
    k = w.shape[0]
    return jnp.pad(w.reshape(k, heads, dim), ((0, 0), (0, 0), (0, HEAD_PAD - dim))).reshape(k, heads * HEAD_PAD)


def _prep_mla(i, g_mix_l, w_in_a, g_q_lora, w_q_up, g_kv_lora, w_kv_up, w_dw, b_dw, g_conv_ln, b_conv_ln, w_out_a):
    w = w_in_a[i]
    o1 = Q_LORA
    o2 = o1 + KV_LORA
    o3 = o2 + QK_ROPE
    zeros = lambda c: jnp.zeros((D_MODEL, c), F32)
    w1 = jnp.concatenate([w[:, :o2], w[:, o3:], zeros(KR_LANE), w[:, o2:o3],
                          zeros(LANES - KR_LANE - QK_ROPE)], axis=1)
    kvu = w_kv_up[i].reshape(KV_LORA, MLA_HEADS, QK_NOPE + V_HEAD)
    w_kn = _pad_heads(kvu[:, :, :QK_NOPE].reshape(KV_LORA, MLA_HEADS * QK_NOPE), MLA_HEADS, QK_NOPE)
    w_v = _pad_heads(kvu[:, :, QK_NOPE:].reshape(KV_LORA, MLA_HEADS * V_HEAD), MLA_HEADS, V_HEAD)
    return dict(
        g_mix=g_mix_l.reshape(1, D_MODEL),
        w1=w1.astype(BF16),
        g_q=g_q_lora[i].reshape(1, Q_LORA),
        w_qu=_pad_heads(w_q_up[i], MLA_HEADS, QK_NOPE + QK_ROPE).astype(BF16),
        g_kv=g_kv_lora[i].reshape(1, KV_LORA),
        w_kv=jnp.concatenate([w_kn, w_v], axis=1).astype(BF16),
        w_dw=jnp.pad(w_dw[i], ((0, 1), (0, 0))),
        b_dw=b_dw[i].reshape(1, CONV_CH),
        g_ln=g_conv_ln[i].reshape(1, CONV_CH),
        b_ln=b_conv_ln[i].reshape(1, CONV_CH),
        w_out=w_out_a[i].astype(BF16),
    )


def _prep_gqa(i, g_mix_l, w_in_c, g_q_head, g_k_head, w_out_c):
    w = w_in_c[i]
    v_pad = _pad_heads(w[:, GQA_Q + GQA_KV:], GQA_KV_HEADS, GQA_HEAD_DIM)
    return dict(
        g_mix=g_mix_l.reshape(1, D_MODEL),
        w=jnp.concatenate([w, v_pad], axis=1).astype(BF16),
        g_q=jnp.tile(g_q_head[i], GQA_HEADS).reshape(1, GQA_Q),
        g_k=jnp.tile(g_k_head[i], GQA_KV_HEADS).reshape(1, GQA_KV),
        w_out=w_out_c[i].astype(BF16),
    )


SAMPLE_HEADS_PER_STEP = 4
PROMPT_SEQS_PER_STEP = 2


def _mixer_ffn(x_all, x_off, n, mod3, l, row_fn, seq, p, ctx, tables, g_ff2, ffn2_w, g_final):
    sample = ctx is not None
    in_proj, n_core = (_mla_in, 4) if l % 2 == 0 else (_gqa_in, 3)
    outs = in_proj(x_all, x_off, n, mod3, l, row_fn, seq, p, tables["mla" if l % 2 == 0 else "gqa"] if sample else None,
                   None if sample else ffn2_w)
    if not sample:
        outs, ffn2_w = outs[:-2], tuple(outs[-2:])
    ffn_w = (g_ff2,) + tuple(ffn2_w)
    if l % 2 == 0:
        q, kt, v, glu = outs[:4]
        segs = [(kt, v)]
        if sample:
            segs = [_mla_ctx(ctx["mla_ckv"], ctx["mla_krb"], p["w_kv"], ctx["past"])] + segs
            hps, bt = SAMPLE_HEADS_PER_STEP, 1
        else:
            hps, bt = MLA_HEADS, PROMPT_SEQS_PER_STEP
        attn = _attention(q, segs, seq=seq, heads=MLA_HEADS, kv_heads=MLA_HEADS, dq=HEAD_PAD, dk=HEAD_PAD,
                          dv=V_HEAD, bt=bt, heads_per_step=hps)
        return _out_ffn(x_all, x_off, mod3, l, row_fn, seq, attn, p["w_out"], ffn_w, glu=glu, conv_p=p,
                        g_final=g_final), outs[n_core:], ffn2_w
    q, kt, v = outs[:3]
    segs = [(kt, v)]
    if sample:
        segs = [_gqa_ctx(ctx["gqa_k"], ctx["gqa_vpad"], ctx["past"])] + segs
        hps, bt = SAMPLE_HEADS_PER_STEP, 1
    else:
        hps, bt = GQA_HEADS, PROMPT_SEQS_PER_STEP
    attn = _attention(q, segs, seq=seq, heads=GQA_HEADS, kv_heads=GQA_KV_HEADS, dq=GQA_HEAD_DIM,
                      dk=GQA_HEAD_DIM, dv=GQA_HEAD_DIM, bt=bt, heads_per_step=hps)
    return (_out_ffn(x_all, x_off, mod3, l, row_fn, seq, attn, p["w_out"], ffn_w, g_final=g_final), outs[n_core:],
            ffn2_w)


def kernel(x_prompt, x_sample, cache_mla_ckv, cache_mla_krope, cache_gqa_k, cache_gqa_v, c, c_ctx, g_ff1, w_ff1_in, w_ff1_out, g_mix, g_ff2, w_ff2_in, w_ff2_out, w_mod, b_mod, w_in_a, g_q_lora, w_q_up, g_kv_lora, w_kv_up, w_dw, b_dw, g_conv_ln, b_conv_ln, w_out_a, w_in_c, g_q_head, g_k_head, w_out_c, g_final):
    batch, seq, _ = x_prompt.shape
    dec_batch, dec_seq, _ = x_sample.shape
    past = cache_mla_ckv.shape[2]
    assert DEPTH == 2 and dec_seq == DEC_SEQ and 1 + dec_batch <= MOD_ROWS

    cond = jnp.concatenate([c_ctx[None, :], c, jnp.zeros((MOD_ROWS - 1 - dec_batch, D_MODEL), F32)], axis=0)
    mod3 = _modulation(cond, w_mod, b_mod)

    layers = [
        _prep_mla(0, g_mix[0], w_in_a, g_q_lora, w_q_up, g_kv_lora, w_kv_up, w_dw, b_dw, g_conv_ln, b_conv_ln,
                  w_out_a),
        _prep_gqa(0, g_mix[1], w_in_c, g_q_head, g_k_head, w_out_c),
    ]
    tables = dict(mla=_rope_tables(QK_ROPE, KR_LANE, HEAD_PAD), gqa=_rope_tables(GQA_HEAD_DIM, 0, GQA_HEAD_DIM))

    ctx = dict(
        past=past,
        mla_ckv=cache_mla_ckv[:, 0].reshape(dec_batch * past, KV_LORA),
        mla_krb=jnp.pad(cache_mla_krope[:, 0].reshape(dec_batch * past, QK_ROPE),
                        ((0, 0), (KR_LANE, LANES - KR_LANE - QK_ROPE))),
        gqa_k=cache_gqa_k[:, 0].reshape(dec_batch * past, GQA_KV),
        gqa_vpad=jnp.pad(cache_gqa_v[:, 0], ((0, 0), (0, 0), (0, 0), (0, HEAD_PAD - GQA_HEAD_DIM))
                         ).reshape(dec_batch * past, GQA_VPAD),
    )

    n_p, n_s = batch * seq, dec_batch * dec_seq
    nb_p = n_p // TM
    row_p = lambda i: 0
    row_s = lambda i: 1 + (i * TM) // DEC_SEQ
    row_all = lambda i: jnp.where(i < nb_p, 0, 1 + ((i - nb_p) * TM) // DEC_SEQ)

    parts = [x_prompt.reshape(n_p, D_MODEL), x_sample.reshape(n_s, D_MODEL)]
    saved = []
    for l in range(DEPTH):
        gf = g_final if l == DEPTH - 1 else None
        x_all = _ffn(parts, mod3, l, 0, row_all, g_ff1, w_ff1_in, w_ff1_out)
        xp, st, w2 = _mixer_ffn(x_all, 0, n_p, mod3, l, row_p, seq, layers[l], None, tables, g_ff2,
                                (w_ff2_in, w_ff2_out), gf)
        xs, _, _ = _mixer_ffn(x_all, nb_p, n_s, mod3, l, row_s, dec_seq, layers[l], ctx, tables, g_ff2, w2, gf)
        saved.append(st)
        parts = [xp, xs]
    y_prompt, y_sample = parts
    (ckv_new, kr_new), (k_new, v_new) = saved

    return (y_prompt.reshape(batch, seq, D_MODEL),
            y_sample.reshape(dec_batch, dec_seq, D_MODEL),
            ckv_new.reshape(batch, 1, seq, KV_LORA),
            kr_new.reshape(batch, 1, seq, QK_ROPE),
            k_new.reshape(batch, 1, seq, GQA_KV_HEADS, GQA_HEAD_DIM),
            v_new.reshape(batch, 1, seq, GQA_KV_HEADS, GQA_HEAD_DIM))
```

```python
import functools

import jax
import jax.numpy as jnp
import numpy as np
from jax import lax
from jax.experimental import pallas as pl
from jax.experimental.pallas import tpu as pltpu

F32 = jnp.float32
BF16 = jnp.bfloat16

D_MODEL = 1024
DEPTH = 2
DEC_SEQ = 2048
GRID_W = 64
MLA_HEADS = 8
Q_LORA = 384
KV_LORA = 256
QK_NOPE = 64
QK_ROPE = 32
V_HEAD = 64
CONV_CH = 512
CONV_W = 31
CONV_PAD = CONV_W // 2
GQA_HEADS = 16
GQA_KV_HEADS = 4
GQA_HEAD_DIM = 64
D_FF = 2816
MACARON = 0.5
N_MOD = 9
ROPE_BASE = 10000.0
EPS = 1e-6

LANES = 128
HEAD_PAD = 128
MOD_ROWS = 8
DENOM_LANE = 64
LOG2E = 1.4426950408889634
HALO = 16
TM = 512
Q_SUB = 256
FF_CHUNK = 256
CONV_ROWS = 64
INTERLEAVE_SLACK = 1
VMEM_LIMIT = 60 * 1024 * 1024


def _params(n_axes):
    return pltpu.CompilerParams(dimension_semantics=("parallel",) * n_axes, vmem_limit_bytes=VMEM_LIMIT)


def _const_spec(shape):
    return pl.BlockSpec(shape, lambda *_: (0,) * len(shape), pipeline_mode=pl.Buffered(1))


def _rms(x, g):
    return x * lax.rsqrt(jnp.mean(x * x, axis=-1, keepdims=True) + EPS) * g


def _mod_norm(x, g, mod):
    return _rms(x, g) * (1.0 + mod[1:2]) + mod[0:1]


def _silu(x):
    return x * jax.nn.sigmoid(x)


def _rope(x, cos, sa, sb, quarter):
    w = x.shape[-1]
    return x * cos + pltpu.roll(x, w - quarter, 1) * sa + pltpu.roll(x, quarter, 1) * sb


def _with_ones_lane(v):
    lane = lax.broadcasted_iota(jnp.int32, v.shape, 1)
    return jnp.where(lane % HEAD_PAD == DENOM_LANE, 1.0, v)


def _tile_lanes(t, reps):
    return t if reps == 1 else jnp.concatenate([t] * reps, axis=1)


def _mod_kernel(c_ref, w_ref, b_ref, o_ref):
    e = _silu(c_ref[...]).astype(BF16)
    o_ref[...] = jnp.dot(e, w_ref[...].astype(BF16), preferred_element_type=F32) + b_ref[...]


def _modulation(cond, w_mod, b_mod):
    n_out = N_MOD * D_MODEL
    tn = 1536
    out = pl.pallas_call(
        _mod_kernel,
        grid=(DEPTH, n_out // tn),
        in_specs=[
            pl.BlockSpec((MOD_ROWS, D_MODEL), lambda l, j: (0, 0)),
            pl.BlockSpec((None, D_MODEL, tn), lambda l, j: (l, 0, j)),
            pl.BlockSpec((None, 1, tn), lambda l, j: (l, 0, j)),
        ],
        out_specs=pl.BlockSpec((None, MOD_ROWS, tn), lambda l, j: (l, 0, j)),
        out_shape=jax.ShapeDtypeStruct((DEPTH, MOD_ROWS, n_out), F32),
        compiler_params=_params(2),
        name="modulation",
    )(cond, w_mod, b_mod.reshape(DEPTH, 1, n_out))
    return out.reshape(DEPTH * MOD_ROWS * 3, 3, D_MODEL)


def _mod_spec(layer, sub, row_fn):
    return pl.BlockSpec((None, 3, D_MODEL), lambda i, *_: ((layer * MOD_ROWS + row_fn(i)) * 3 + sub, 0, 0))


def _cast_ffn2_slabs(wi_ref, wo_ref, wi_out, wo_out):
    wi_out[...] = wi_ref[...].astype(BF16)
    wo_out[...] = wo_ref[...].astype(BF16)


def _cast_ffn2_specs(layer, steps, w_in, w_out):
    slabs = steps // 2
    shapes = [w_in.shape[1:], w_out.shape[1:]]
    assert steps % 2 == 0 and all(k % (slabs * 16) == 0 for k, _ in shapes)
    in_specs = [pl.BlockSpec((None, k // slabs, n), lambda i: (layer, i // 2, 0)) for k, n in shapes]
    out_specs = [pl.BlockSpec((k // slabs, n), lambda i: (i // 2, 0)) for k, n in shapes]
    return in_specs, out_specs, [jax.ShapeDtypeStruct(s, BF16) for s in shapes]


def _ffn_body(x, mod, g_ref, win_ref, wout_ref, between_chunks=None):
    h = _mod_norm(x, g_ref[...], mod).astype(BF16)
    acc = jnp.zeros(x.shape, F32)
    zeros = [None] * INTERLEAVE_SLACK
    for j in range(D_FF // FF_CHUNK):
        lo = j * FF_CHUNK
        a = jnp.dot(h, win_ref[:, lo:lo + FF_CHUNK].astype(BF16), preferred_element_type=F32)
        b = jnp.dot(h, win_ref[:, D_FF + lo:D_FF + lo + FF_CHUNK].astype(BF16), preferred_element_type=F32)
        act = (_silu(a) * b).astype(BF16)
        zero = zeros.pop(0)
        rows = pl.ds(lo, FF_CHUNK) if zero is None else pl.ds(pl.multiple_of(lo + zero, FF_CHUNK), FF_CHUNK)
        acc = acc + jnp.dot(act, wout_ref[rows, :].astype(BF16), preferred_element_type=F32)
        zeros.append(between_chunks(j) if between_chunks is not None else None)
    return x + (MACARON * mod[2:3]) * acc


def _ffn_kernel(*refs, nparts, nblk0):
    x_refs = refs[:nparts]
    mod_ref, g_ref, win_ref, wout_ref, wi_ref, wo_ref, o_ref, wi_out, wo_out = refs[nparts:]
    _cast_ffn2_slabs(wi_ref, wo_ref, wi_out, wo_out)
    x = x_refs[0][...]
    if nparts == 2:
        x = jnp.where(pl.program_id(0) < nblk0, x, x_refs[1][...])
    o_ref[...] = _ffn_body(x, mod_ref[...], g_ref, win_ref, wout_ref)


def _ffn_weight_specs(layer):
    layer_spec = lambda shape: pl.BlockSpec((None,) + shape, lambda i: (layer, 0, 0), pipeline_mode=pl.Buffered(1))
    return [layer_spec((1, D_MODEL)), layer_spec((D_MODEL, 2 * D_FF)), layer_spec((D_FF, D_MODEL))]


def _ffn(parts, mod3, layer, sub, row_fn, g, w_in, w_out, ffn2_f32):
    nblks = [p.shape[0] // TM for p in parts]
    if len(parts) == 1:
        in_specs = [pl.BlockSpec((TM, D_MODEL), lambda i: (i, 0))]
    else:
        nb0 = nblks[0]
        in_specs = [pl.BlockSpec((TM, D_MODEL), lambda i: (jnp.minimum(i, nb0 - 1), 0)),
                    pl.BlockSpec((TM, D_MODEL), lambda i: (jnp.maximum(i - nb0, 0), 0))]
    n = sum(nblks) * TM
    cast_in, cast_out, cast_shape = _cast_ffn2_specs(layer, n // TM, *ffn2_f32)
    in_specs += [_mod_spec(layer, sub, row_fn)] + _ffn_weight_specs(layer) + cast_in
    y, w2_in, w2_out = pl.pallas_call(
        functools.partial(_ffn_kernel, nparts=len(parts), nblk0=nblks[0]),
        grid=(n // TM,),
        in_specs=in_specs,
        out_specs=[pl.BlockSpec((TM, D_MODEL), lambda i: (i, 0))] + cast_out,
        out_shape=[jax.ShapeDtypeStruct((n, D_MODEL), F32)] + cast_shape,
        compiler_params=pltpu.CompilerParams(dimension_semantics=("arbitrary",), vmem_limit_bytes=VMEM_LIMIT),
        name="ffn",
    )(*parts, mod3, g.reshape(DEPTH, 1, D_MODEL), w_in, w_out, *ffn2_f32)
    return y, (w2_in, w2_out)


MLA_W1_COLS = Q_LORA + KV_LORA + 2 * CONV_CH + LANES
MLA_QK = MLA_HEADS * HEAD_PAD
KR_LANE = QK_NOPE


def _mla_kv(ckvn, krb, wkv_ref, kt_ref, v_ref, nseq):
    kv = jnp.dot(ckvn.astype(BF16), wkv_ref[...], preferred_element_type=F32)
    k = kv[:, :MLA_QK] + _tile_lanes(krb, MLA_HEADS)
    v_ref[...] = _with_ones_lane(kv[:, MLA_QK:]).astype(BF16)
    rows = k.shape[0] // nseq
    for s in range(nseq):
        kt_ref[s] = k[s * rows:(s + 1) * rows, :].T.astype(BF16)


def _mla_in_kernel(x_ref, mod_ref, g_ref, w1_ref, gq_ref, wqu_ref, gkv_ref, wkv_ref, *rest, rope, nseq):
    if rope:
        cos_ref, sa_ref, sb_ref, q_ref, kt_ref, v_ref, glu_ref = rest
    else:
        q_ref, kt_ref, v_ref, glu_ref, ckv_ref, kr_ref = rest
    h = _mod_norm(x_ref[...], g_ref[...], mod_ref[...]).astype(BF16)
    proj = jnp.dot(h, w1_ref[...], preferred_element_type=F32)
    o1 = Q_LORA
    o2 = o1 + KV_LORA
    o3 = o2 + CONV_CH
    o4 = o3 + CONV_CH
    cq, ckv, ua, ub, krb = proj[:, :o1], proj[:, o1:o2], proj[:, o2:o3], proj[:, o3:o4], proj[:, o4:]
    glu_ref[...] = ua * jax.nn.sigmoid(ub)
    q = jnp.dot(_rms(cq, gq_ref[...]).astype(BF16), wqu_ref[...], preferred_element_type=F32)
    ckvn = _rms(ckv, gkv_ref[...])
    if rope:
        cos, sa, sb = cos_ref[...], sa_ref[...], sb_ref[...]
        quarter = QK_ROPE // 4
        krb = _rope(krb, cos, sa, sb, quarter)
        q = _rope(q, _tile_lanes(cos, MLA_HEADS), _tile_lanes(sa, MLA_HEADS), _tile_lanes(sb, MLA_HEADS), quarter)
    else:
        ckv_ref[...] = ckvn
        kr_ref[...] = krb[:, KR_LANE:KR_LANE + QK_ROPE]
    q_ref[...] = (q * ((QK_NOPE + QK_ROPE) ** -0.5 * LOG2E)).astype(BF16)
    _mla_kv(ckvn, krb, wkv_ref, kt_ref, v_ref, nseq)


def _mla_ctx_kernel(ckv_ref, krb_ref, wkv_ref, kt_ref, v_ref):
    _mla_kv(ckv_ref[...], krb_ref[...], wkv_ref, kt_ref, v_ref, 1)


def _mla_in(x, x_off, n, mod3, layer, row_fn, seq, p, tables):
    rope = tables is not None
    nseq = max(TM // seq, 1)
    rows = TM // nseq
    batch = n // seq
    tiles_per_seq = max(seq // TM, 1)
    row = lambda w: pl.BlockSpec((TM, w), lambda i: (i, 0))
    in_specs = [pl.BlockSpec((TM, D_MODEL), lambda i: (i + x_off, 0)), _mod_spec(layer, 1, row_fn),
                _const_spec((1, D_MODEL)),
                _const_spec((D_MODEL, MLA_W1_COLS)), _const_spec((1, Q_LORA)), _const_spec((Q_LORA, MLA_QK)),
                _const_spec((1, KV_LORA)), _const_spec((KV_LORA, 2 * MLA_QK))]
    args = [x, mod3, p["g_mix"], p["w1"], p["g_q"], p["w_qu"], p["g_kv"], p["w_kv"]]
    kt_spec = pl.BlockSpec((nseq, MLA_QK, rows), lambda i: (i // tiles_per_seq, 0, i % tiles_per_seq))
    out_specs = [row(MLA_QK), kt_spec, row(MLA_QK), row(CONV_CH)]
    out_shape = [jax.ShapeDtypeStruct((n, MLA_QK), BF16), jax.ShapeDtypeStruct((batch, MLA_QK, seq), BF16),
                 jax.ShapeDtypeStruct((n, MLA_QK), BF16), jax.ShapeDtypeStruct((n, CONV_CH), F32)]
    if rope:
        tab = pl.BlockSpec((TM, LANES), lambda i: (i % tiles_per_seq, 0))
        in_specs += [tab, tab, tab]
        args += list(tables)
    else:
        out_specs += [row(KV_LORA), row(QK_ROPE)]
        out_shape += [jax.ShapeDtypeStruct((n, KV_LORA), F32), jax.ShapeDtypeStruct((n, QK_ROPE), F32)]
    return pl.pallas_call(
        functools.partial(_mla_in_kernel, rope=rope, nseq=nseq),
        grid=(n // TM,),
        in_specs=in_specs,
        out_specs=out_specs,
        out_shape=out_shape,
        compiler_params=_params(1),
        name="mla_in_rope" if rope else "mla_in",
    )(*args)


def _mla_ctx(ckv, krb, w_kv, seq):
    n = ckv.shape[0]
    return pl.pallas_call(
        _mla_ctx_kernel,
        grid=(n // seq,),
        in_specs=[pl.BlockSpec((seq, KV_LORA), lambda i: (i, 0)), pl.BlockSpec((seq, LANES), lambda i: (i, 0)),
                  _const_spec((KV_LORA, 2 * MLA_QK))],
        out_specs=[pl.BlockSpec((1, MLA_QK, seq), lambda i: (i, 0, 0)), pl.BlockSpec((seq, MLA_QK), lambda i: (i, 0))],
        out_shape=[jax.ShapeDtypeStruct((n // seq, MLA_QK, seq), BF16), jax.ShapeDtypeStruct((n, MLA_QK), BF16)],
        compiler_params=_params(1),
        name="mla_ctx",
    )(ckv, krb, w_kv)


GQA_Q = GQA_HEADS * GQA_HEAD_DIM
GQA_KV = GQA_KV_HEADS * GQA_HEAD_DIM
GQA_VPAD = GQA_KV_HEADS * HEAD_PAD
GQA_W_COLS = GQA_Q + 2 * GQA_KV + GQA_VPAD


def _head_rms(x, g):
    tm, w = x.shape
    lo = lax.broadcasted_iota(jnp.int32, (tm, LANES), 1) < GQA_HEAD_DIM
    outs = []
    for b in range(w // LANES):
        xb = x[:, b * LANES:(b + 1) * LANES]
        sq = xb * xb
        s_lo = jnp.sum(jnp.where(lo, sq, 0.0), axis=-1, keepdims=True)
        s_hi = jnp.sum(jnp.where(lo, 0.0, sq), axis=-1, keepdims=True)
        ms = jnp.where(lo, s_lo, s_hi) * (1.0 / GQA_HEAD_DIM)
        outs.append(xb * lax.rsqrt(ms + EPS))
    return jnp.concatenate(outs, axis=1) * g


def _gqa_in_kernel(x_ref, mod_ref, g_ref, w_ref, gq_ref, gk_ref, *rest, rope, nseq):
    if rope:
        cos_ref, sa_ref, sb_ref, q_ref, kt_ref, v_ref = rest
    else:
        q_ref, kt_ref, v_ref, kc_ref, vc_ref = rest
    h = _mod_norm(x_ref[...], g_ref[...], mod_ref[...]).astype(BF16)
    proj = jnp.dot(h, w_ref[...], preferred_element_type=F32)
    q = _head_rms(proj[:, :GQA_Q], gq_ref[...])
    k = _head_rms(proj[:, GQA_Q:GQA_Q + GQA_KV], gk_ref[...])
    if rope:
        cos, sa, sb = cos_ref[...], sa_ref[...], sb_ref[...]
        quarter = GQA_HEAD_DIM // 4
        q = _rope(q, _tile_lanes(cos, GQA_Q // LANES), _tile_lanes(sa, GQA_Q // LANES),
                  _tile_lanes(sb, GQA_Q // LANES), quarter)
        k = _rope(k, _tile_lanes(cos, GQA_KV // LANES), _tile_lanes(sa, GQA_KV // LANES),
                  _tile_lanes(sb, GQA_KV // LANES), quarter)
    else:
        kc_ref[...] = k
        vc_ref[...] = proj[:, GQA_Q + GQA_KV:GQA_Q + 2 * GQA_KV]
    q_ref[...] = (q * (GQA_HEAD_DIM ** -0.5 * LOG2E)).astype(BF16)
    v_ref[...] = _with_ones_lane(proj[:, GQA_Q + 2 * GQA_KV:]).astype(BF16)
    rows = k.shape[0] // nseq
    for s in range(nseq):
        kt_ref[s] = k[s * rows:(s + 1) * rows, :].T.astype(BF16)


def _gqa_ctx_kernel(k_ref, v_ref, kt_ref, vo_ref):
    kt_ref[0] = k_ref[...].T.astype(BF16)
    vo_ref[...] = _with_ones_lane(v_ref[...]).astype(BF16)


def _gqa_in(x, x_off, n, mod3, layer, row_fn, seq, p, tables):
    rope = tables is not None
    nseq = max(TM // seq, 1)
    rows = TM // nseq
    batch = n // seq
    tiles_per_seq = max(seq // TM, 1)
    row = lambda w: pl.BlockSpec((TM, w), lambda i: (i, 0))
    in_specs = [pl.BlockSpec((TM, D_MODEL), lambda i: (i + x_off, 0)), _mod_spec(layer, 1, row_fn),
                _const_spec((1, D_MODEL)),
                _const_spec((D_MODEL, GQA_W_COLS)), _const_spec((1, GQA_Q)), _const_spec((1, GQA_KV))]
    args = [x, mod3, p["g_mix"], p["w"], p["g_q"], p["g_k"]]
    kt_spec = pl.BlockSpec((nseq, GQA_KV, rows), lambda i: (i // tiles_per_seq, 0, i % tiles_per_seq))
    out_specs = [row(GQA_Q), kt_spec, row(GQA_VPAD)]
    out_shape = [jax.ShapeDtypeStruct((n, GQA_Q), BF16), jax.ShapeDtypeStruct((batch, GQA_KV, seq), BF16),
                 jax.ShapeDtypeStruct((n, GQA_VPAD), BF16)]
    if rope:
        tab = pl.BlockSpec((TM, LANES), lambda i: (i % tiles_per_seq, 0))
        in_specs += [tab, tab, tab]
        args += list(tables)
    else:
        out_specs += [row(GQA_KV), row(GQA_KV)]
        out_shape += [jax.ShapeDtypeStruct((n, GQA_KV), F32), jax.ShapeDtypeStruct((n, GQA_KV), F32)]
    return pl.pallas_call(
        functools.partial(_gqa_in_kernel, rope=rope, nseq=nseq),
        grid=(n // TM,),
        in_specs=in_specs,
        out_specs=out_specs,
        out_shape=out_shape,
        compiler_params=_params(1),
        name="gqa_in_rope" if rope else "gqa_in",
    )(*args)


def _gqa_ctx(k, vpad, seq):
    n = k.shape[0]
    return pl.pallas_call(
        _gqa_ctx_kernel,
        grid=(n // seq,),
        in_specs=[pl.BlockSpec((seq, GQA_KV), lambda i: (i, 0)), pl.BlockSpec((seq, GQA_VPAD), lambda i: (i, 0))],
        out_specs=[pl.BlockSpec((1, GQA_KV, seq), lambda i: (i, 0, 0)), pl.BlockSpec((seq, GQA_VPAD), lambda i: (i, 0))],
        out_shape=[jax.ShapeDtypeStruct((n // seq, GQA_KV, seq), BF16), jax.ShapeDtypeStruct((n, GQA_VPAD), BF16)],
        compiler_params=_params(1),
        name="gqa_ctx",
    )(k, vpad)


def _attn_kernel(*refs, nseg, bt, seq, nq, nkv, dq, dk, dv):
    q_ref = refs[0]
    segs = [(refs[1 + 2 * s], refs[2 + 2 * s]) for s in range(nseg)]
    o_ref = refs[1 + 2 * nseg]
    s_refs = refs[2 + 2 * nseg:4 + 2 * nseg]
    nks = [kt_ref.shape[2] for kt_ref, _ in segs]
    offs = [sum(nks[:s]) for s in range(nseg)]

    def scores(bi, r0, s_ref):
        rows = pl.ds(bi * seq + r0, Q_SUB)
        for j in range(nq):
            kv = j * nkv // nq
            qs = q_ref[rows, j * dq:(j + 1) * dq]
            for (kt_ref, _), off, nk in zip(segs, offs, nks):
                s_ref[j, :, off:off + nk] = jnp.dot(qs, kt_ref[bi, kv * dk:(kv + 1) * dk, :],
                                                    preferred_element_type=F32)

    def softmax_pv(bi, r0, s_ref):
        rows = pl.ds(bi * seq + r0, Q_SUB)
        outs = []
        for j in range(nq):
            kv = j * nkv // nq
            s = s_ref[j]
            p = jnp.exp2(s - jnp.max(s, axis=-1, keepdims=True)).astype(BF16)
            o = None
            for (_, v_ref), off, nk in zip(segs, offs, nks):
                pv = jnp.dot(p[:, off:off + nk], v_ref[bi * nk:(bi + 1) * nk, kv * HEAD_PAD:(kv + 1) * HEAD_PAD],
                             preferred_element_type=F32)
                o = pv if o is None else o + pv
            outs.append(o[:, :dv] / o[:, DENOM_LANE:DENOM_LANE + 1])
        o_ref[rows, :] = jnp.concatenate(outs, axis=1).astype(BF16)

    if seq == Q_SUB:
        assert bt == len(s_refs)
        for bi in range(bt):
            scores(bi, 0, s_refs[bi])
        for bi in range(bt):
            softmax_pv(bi, 0, s_refs[bi])
    else:
        n_sub = seq // Q_SUB
        assert bt == 1
        scores(0, 0, s_refs[0])
        for r in range(n_sub):
            if r + 1 < n_sub:
                scores(0, (r + 1) * Q_SUB, s_refs[(r + 1) % 2])
            softmax_pv(0, r * Q_SUB, s_refs[r % 2])


def _attention(q, segs, *, seq, heads, kv_heads, dq, dk, dv, bt, heads_per_step):
    n = q.shape[0]
    batch = n // seq
    nq = heads_per_step
    steps = heads // nq
    nkv = max(kv_heads // steps, 1)
    q_per_kv_block = steps // (kv_heads // nkv)
    in_specs = [pl.BlockSpec((bt * seq, nq * dq), lambda b, p: (b, p))]
    args = [q]
    for kt, v in segs:
        nk = kt.shape[2]
        in_specs.append(pl.BlockSpec((bt, nkv * dk, nk), lambda b, p: (b, p // q_per_kv_block, 0)))
        in_specs.append(pl.BlockSpec((bt * nk, nkv * HEAD_PAD), lambda b, p: (b, p // q_per_kv_block)))
        args += [kt, v]
    return pl.pallas_call(
        functools.partial(_attn_kernel, nseg=len(segs), bt=bt, seq=seq, nq=nq, nkv=nkv, dq=dq, dk=dk, dv=dv),
        grid=(batch // bt, steps),
        in_specs=in_specs,
        out_specs=pl.BlockSpec((bt * seq, nq * dv), lambda b, p: (b, p)),
        out_shape=jax.ShapeDtypeStruct((n, heads * dv), BF16),
        scratch_shapes=[pltpu.VMEM((nq, Q_SUB, sum(kt.shape[2] for kt, _ in segs)), F32)] * 2,
        compiler_params=_params(2),
        name="attention",
    )(*args)


SUBLANES = 8
CONV_LEAD = HALO - CONV_PAD
CONV_WIN = CONV_ROWS + 2 * HALO


def _fill_pad(pad_ref, glu_ref, prev_ref, next_ref, first, last, nseq):
    rows = TM // nseq
    zeros = jnp.zeros((HALO, CONV_CH), F32)
    for s in range(nseq):
        pad_ref[s, 0:HALO, :] = zeros if prev_ref is None else jnp.where(first, zeros, prev_ref[...])
        pad_ref[s, HALO + rows:2 * HALO + rows, :] = (
            zeros if next_ref is None else jnp.where(last, zeros, next_ref[...]))
        pad_ref[s, HALO:HALO + rows, :] = glu_ref[s * rows:(s + 1) * rows, :]


CONV_LANE_BLOCKS = CONV_CH // LANES


def _conv_block(pad_ref, s, r0, lb, wdw_ref, bdw_ref, zero_ref, prev_zero):
    lanes = slice(lb * LANES, (lb + 1) * LANES)
    win = pad_ref[s, pl.ds(r0, CONV_WIN), lanes]
    acc = jnp.broadcast_to(bdw_ref[:, lanes], (CONV_ROWS, LANES))
    if prev_zero is not None:
        acc = acc + jnp.concatenate([prev_zero] * (CONV_ROWS // SUBLANES), axis=0)
    for phase in range(SUBLANES):
        shifted = win if phase == 0 else pltpu.roll(win, CONV_WIN - phase, 0)
        for a in range((CONV_W + CONV_LEAD) // SUBLANES + 1):
            k = a * SUBLANES + phase - CONV_LEAD
            if 0 <= k < CONV_W:
                acc = acc + shifted[a * SUBLANES:a * SUBLANES + CONV_ROWS] * wdw_ref[k:k + 1, lanes]
    peak = functools.reduce(jnp.maximum, [acc[g * SUBLANES:(g + 1) * SUBLANES]
                                          for g in range(CONV_ROWS // SUBLANES)])
    zero = (peak.astype(jnp.int32) & jnp.full((SUBLANES, LANES), zero_ref[0], jnp.int32)).astype(F32)
    return acc, zero


def _conv_finish(blocks, gln_ref, bln_ref):
    acc = jnp.concatenate(blocks, axis=1)
    mu = jnp.mean(acc, axis=-1, keepdims=True)
    d = acc - mu
    var = jnp.mean(d * d, axis=-1, keepdims=True)
    return _silu(d * lax.rsqrt(var + EPS) * gln_ref[...] + bln_ref[...])


def _out_ffn_kernel(*refs, conv, nseq, halo, tiles_per_seq, final):
    it = iter(refs)
    x_ref, modm_ref, modf_ref, attn_ref = next(it), next(it), next(it), next(it)
    if conv:
        glu0_ref = next(it)
        next0_ref = next(it) if halo else None
        glun_ref = next(it)
        prevn_ref, nextn_ref = (next(it), next(it)) if halo else (None, None)
        conv_w = [next(it) for _ in range(4)]
        zero_ref = next(it)
    wout_ref, gff_ref, win_ref, wff_ref = next(it), next(it), next(it), next(it)
    gfin_ref = next(it) if final else None
    y_ref = next(it)
    i = pl.program_id(0)
    rows = TM // nseq

    if conv:
        pad_ref, g_ref = next(it), next(it)

        wdw_ref, bdw_ref, gln_ref, bln_ref = conv_w
        units = [(c, lb) for c in range(TM // CONV_ROWS) for lb in range(CONV_LANE_BLOCKS)]
        hosts = D_FF // FF_CHUNK - INTERLEAVE_SLACK
        counts = [len(units) // hosts + (j < len(units) % hosts) for j in range(hosts)]
        starts = [sum(counts[:j]) for j in range(hosts + 1)]

        def run_units(todo, state):
            for c, lb in todo:
                s, r0 = divmod(c * CONV_ROWS, rows)
                acc, state[0] = _conv_block(pad_ref, s, r0, lb, wdw_ref, bdw_ref, zero_ref, state[0])
                state[1].append(acc)
                if lb == CONV_LANE_BLOCKS - 1:
                    g = _conv_finish(state[1], gln_ref, bln_ref)
                    g_ref[c * CONV_ROWS:(c + 1) * CONV_ROWS, :] = g.astype(BF16)
                    state[1] = []

        @pl.when(i == 0)
        def _():
            _fill_pad(pad_ref, glu0_ref, None, next0_ref, True, tiles_per_seq == 1, nseq)
            run_units(units, [None, []])

        out = (jnp.dot(attn_ref[...], wout_ref[:CONV_CH, :], preferred_element_type=F32)
               + jnp.dot(g_ref[...], wout_ref[CONV_CH:, :], preferred_element_type=F32))
        pos = (i + 1) % tiles_per_seq
        _fill_pad(pad_ref, glun_ref, prevn_ref, nextn_ref, pos == 0, pos == tiles_per_seq - 1, nseq)
        state = [None, []]

        def between(j):
            if j >= hosts:
                return None
            todo = units[starts[j]:starts[j + 1]]
            run_units(todo, state)
            return jnp.max(state[0]).astype(jnp.int32)
    else:
        out = jnp.dot(attn_ref[...], wout_ref[...], preferred_element_type=F32)
        between = None
    x = x_ref[...] + modm_ref[2:3, :] * out
    y = _ffn_body(x, modf_ref[...], gff_ref, win_ref, wff_ref, between)
    if final:
        y = _rms(y, gfin_ref[...])
    y_ref[...] = y


def _out_ffn(x, x_off, mod3, layer, row_fn, seq, attn, w_out, ffn_w, glu=None, conv_p=None, g_final=None):
    n = attn.shape[0]
    n_tiles = n // TM
    conv = glu is not None
    final = g_final is not None
    nseq = max(TM // seq, 1)
    tiles_per_seq = max(seq // TM, 1)
    halo = conv and tiles_per_seq > 1
    row = lambda w: pl.BlockSpec((TM, w), lambda i: (i, 0))
    in_specs = [pl.BlockSpec((TM, D_MODEL), lambda i: (i + x_off, 0)), _mod_spec(layer, 1, row_fn),
                _mod_spec(layer, 2, row_fn), row(attn.shape[1])]
    args = [x, mod3, mod3, attn]
    scratch = []
    if conv:
        per = TM // HALO
        last = n // HALO - 1
        nxt = lambda i: jnp.minimum(i + 1, n_tiles - 1)
        in_specs.append(pl.BlockSpec((TM, CONV_CH), lambda i: (0, 0)))
        args.append(glu)
        if halo:
            in_specs.append(pl.BlockSpec((HALO, CONV_CH), lambda i: (per, 0)))
            args.append(glu)
        in_specs.append(pl.BlockSpec((TM, CONV_CH), lambda i: (nxt(i), 0)))
        args.append(glu)
        if halo:
            in_specs += [pl.BlockSpec((HALO, CONV_CH), lambda i: (jnp.maximum(nxt(i) * per - 1, 0), 0)),
                         pl.BlockSpec((HALO, CONV_CH), lambda i: (jnp.minimum((nxt(i) + 1) * per, last), 0))]
            args += [glu, glu]
        in_specs += [_const_spec((CONV_W + 1, CONV_CH)), _const_spec((1, CONV_CH)), _const_spec((1, CONV_CH)),
                     _const_spec((1, CONV_CH)), pl.BlockSpec(memory_space=pltpu.SMEM)]
        args += [conv_p["w_dw"], conv_p["b_dw"], conv_p["g_ln"], conv_p["b_ln"], jnp.zeros((1,), jnp.int32)]
        scratch = [pltpu.VMEM((nseq, TM // nseq + 2 * HALO, CONV_CH), F32), pltpu.VMEM((TM, CONV_CH), BF16)]
    g_ff, w_in, w_ff = ffn_w
    in_specs += [_const_spec(w_out.shape), _ffn_weight_specs(layer)[0], _const_spec(w_in.shape),
                 _const_spec(w_ff.shape)]
    args += [w_out, g_ff.reshape(DEPTH, 1, D_MODEL), w_in, w_ff]
    if final:
        in_specs.append(_const_spec((1, D_MODEL)))
        args.append(g_final.reshape(1, D_MODEL))
    return pl.pallas_call(
        functools.partial(_out_ffn_kernel, conv=conv, nseq=nseq, halo=halo, tiles_per_seq=tiles_per_seq,
                          final=final),
        grid=(n_tiles,),
        in_specs=in_specs,
        out_specs=row(D_MODEL),
        out_shape=jax.ShapeDtypeStruct((n, D_MODEL), F32),
        scratch_shapes=scratch,
        compiler_params=pltpu.CompilerParams(dimension_semantics=("arbitrary",), vmem_limit_bytes=VMEM_LIMIT),
        name="out_ffn_conv" if conv else "out_ffn",
    )(*args)


def _rope_tables(d, offset, period):
    half = d // 2
    quarter = half // 2
    rows = DEC_SEQ // GRID_W
    pos_row = np.repeat(np.arange(rows), GRID_W)
    pos_col = np.tile(np.arange(GRID_W), rows)
    inv = ROPE_BASE ** (-np.arange(0, half, 2, dtype=np.float64) / half)
    ang_r = pos_row.astype(np.float64)[:, None] * inv[None, :]
    ang_c = pos_col.astype(np.float64)[:, None] * inv[None, :]
    zero = np.zeros((DEC_SEQ, quarter))
    cos = np.concatenate([np.cos(ang_r), np.cos(ang_r), np.cos(ang_c), np.cos(ang_c)], axis=1)
    sa = np.concatenate([-np.sin(ang_r), zero, -np.sin(ang_c), zero], axis=1)
    sb = np.concatenate([zero, np.sin(ang_r), zero, np.sin(ang_c)], axis=1)

    def embed(t, fill):
        blk = np.concatenate([np.full((DEC_SEQ, offset), fill), t,
                              np.full((DEC_SEQ, period - offset - d), fill)], axis=1)
        return jnp.asarray(np.tile(blk, (1, LANES // period)), dtype=F32)

    return embed(cos, 1.0), embed(sa, 0.0), embed(sb, 0.0)


def _pad_heads(w, heads, dim):
    k = w.shape[0]
    return jnp.pad(w.reshape(k, heads, dim), ((0, 0), (0, 0), (0, HEAD_PAD - dim))).reshape(k, heads * HEAD_PAD)


def _prep_mla(i, g_mix_l, w_in_a, g_q_lora, w_q_up, g_kv_lora, w_kv_up, w_dw, b_dw, g_conv_ln, b_conv_ln, w_out_a):
    w = w_in_a[i]
    o1 = Q_LORA
    o2 = o1 + KV_LORA
    o3 = o2 + QK_ROPE
    zeros = lambda c: jnp.zeros((D_MODEL, c), F32)
    w1 = jnp.concatenate([w[:, :o2], w[:, o3:], zeros(KR_LANE), w[:, o2:o3],
                          zeros(LANES - KR_LANE - QK_ROPE)], axis=1)
    kvu = w_kv_up[i].reshape(KV_LORA, MLA_HEADS, QK_NOPE + V_HEAD)
    w_kn = _pad_heads(kvu[:, :, :QK_NOPE].reshape(KV_LORA, MLA_HEADS * QK_NOPE), MLA_HEADS, QK_NOPE)
    w_v = _pad_heads(kvu[:, :, QK_NOPE:].reshape(KV_LORA, MLA_HEADS * V_HEAD), MLA_HEADS, V_HEAD)
    return dict(
        g_mix=g_mix_l.reshape(1, D_MODEL),
        w1=w1.astype(BF16),
        g_q=g_q_lora[i].reshape(1, Q_LORA),
        w_qu=_pad_heads(w_q_up[i], MLA_HEADS, QK_NOPE + QK_ROPE).astype(BF16),
        g_kv=g_kv_lora[i].reshape(1, KV_LORA),
        w_kv=jnp.concatenate([w_kn, w_v], axis=1).astype(BF16),
        w_dw=jnp.pad(w_dw[i], ((0, 1), (0, 0))),
        b_dw=b_dw[i].reshape(1, CONV_CH),
        g_ln=g_conv_ln[i].reshape(1, CONV_CH),
        b_ln=b_conv_ln[i].reshape(1, CONV_CH),
        w_out=w_out_a[i].astype(BF16),
    )


def _prep_gqa(i, g_mix_l, w_in_c, g_q_head, g_k_head, w_out_c):
    w = w_in_c[i]
    v_pad = _pad_heads(w[:, GQA_Q + GQA_KV:], GQA_KV_HEADS, GQA_HEAD_DIM)
    return dict(
        g_mix=g_mix_l.reshape(1, D_MODEL),
        w=jnp.concatenate([w, v_pad], axis=1).astype(BF16),
        g_q=jnp.tile(g_q_head[i], GQA_HEADS).reshape(1, GQA_Q),
        g_k=jnp.tile(g_k_head[i], GQA_KV_HEADS).reshape(1, GQA_KV),
        w_out=w_out_c[i].astype(BF16),
    )


SAMPLE_HEADS_PER_STEP = 4
PROMPT_SEQS_PER_STEP = 2


def _mixer_ffn(x_all, x_off, n, mod3, l, row_fn, seq, p, ctx, tables, g_ff2, ffn2_w, g_final):
    sample = ctx is not None
    in_proj, n_core = (_mla_in, 4) if l % 2 == 0 else (_gqa_in, 3)
    outs = in_proj(x_all, x_off, n, mod3, l, row_fn, seq, p, tables["mla" if l % 2 == 0 else "gqa"] if sample else None)
    ffn_w = (g_ff2,) + tuple(ffn2_w)
    if l % 2 == 0:
        q, kt, v, glu = outs[:4]
        segs = [(kt, v)]
        if sample:
            segs = [_mla_ctx(ctx["mla_ckv"], ctx["mla_krb"], p["w_kv"], ctx["past"])] + segs
            hps, bt = SAMPLE_HEADS_PER_STEP, 1
        else:
            hps, bt = MLA_HEADS, PROMPT_SEQS_PER_STEP
        attn = _attention(q, segs, seq=seq, heads=MLA_HEADS, kv_heads=MLA_HEADS, dq=HEAD_PAD, dk=HEAD_PAD,
                          dv=V_HEAD, bt=bt, heads_per_step=hps)
        return _out_ffn(x_all, x_off, mod3, l, row_fn, seq, attn, p["w_out"], ffn_w, glu=glu, conv_p=p,
                        g_final=g_final), outs[n_core:]
    q, kt, v = outs[:3]
    segs = [(kt, v)]
    if sample:
        segs = [_gqa_ctx(ctx["gqa_k"], ctx["gqa_vpad"], ctx["past"])] + segs
        hps, bt = SAMPLE_HEADS_PER_STEP, 1
    else:
        hps, bt = GQA_HEADS, PROMPT_SEQS_PER_STEP
    attn = _attention(q, segs, seq=seq, heads=GQA_HEADS, kv_heads=GQA_KV_HEADS, dq=GQA_HEAD_DIM,
                      dk=GQA_HEAD_DIM, dv=GQA_HEAD_DIM, bt=bt, heads_per_step=hps)
    return _out_ffn(x_all, x_off, mod3, l, row_fn, seq, attn, p["w_out"], ffn_w, g_final=g_final), outs[n_core:]


def kernel(x_prompt, x_sample, cache_mla_ckv, cache_mla_krope, cache_gqa_k, cache_gqa_v, c, c_ctx, g_ff1, w_ff1_in, w_ff1_out, g_mix, g_ff2, w_ff2_in, w_ff2_out, w_mod, b_mod, w_in_a, g_q_lora, w_q_up, g_kv_lora, w_kv_up, w_dw, b_dw, g_conv_ln, b_conv_ln, w_out_a, w_in_c, g_q_head, g_k_head, w_out_c, g_final):
    batch, seq, _ = x_prompt.shape
    dec_batch, dec_seq, _ = x_sample.shape
    past = cache_mla_ckv.shape[2]
    assert DEPTH == 2 and dec_seq == DEC_SEQ and 1 + dec_batch <= MOD_ROWS

    cond = jnp.concatenate([c_ctx[None, :], c, jnp.zeros((MOD_ROWS - 1 - dec_batch, D_MODEL), F32)], axis=0)
    mod3 = _modulation(cond, w_mod, b_mod)

    layers = [
        _prep_mla(0, g_mix[0], w_in_a, g_q_lora, w_q_up, g_kv_lora, w_kv_up, w_dw, b_dw, g_conv_ln, b_conv_ln,
                  w_out_a),
        _prep_gqa(0, g_mix[1], w_in_c, g_q_head, g_k_head, w_out_c),
    ]
    tables = dict(mla=_rope_tables(QK_ROPE, KR_LANE, HEAD_PAD), gqa=_rope_tables(GQA_HEAD_DIM, 0, GQA_HEAD_DIM))

    ctx = dict(
        past=past,
        mla_ckv=cache_mla_ckv[:, 0].reshape(dec_batch * past, KV_LORA),
        mla_krb=jnp.pad(cache_mla_krope[:, 0].reshape(dec_batch * past, QK_ROPE),
                        ((0, 0), (KR_LANE, LANES - KR_LANE - QK_ROPE))),
        gqa_k=cache_gqa_k[:, 0].reshape(dec_batch * past, GQA_KV),
        gqa_vpad=jnp.pad(cache_gqa_v[:, 0], ((0, 0), (0, 0), (0, 0), (0, HEAD_PAD - GQA_HEAD_DIM))
                         ).reshape(dec_batch * past, GQA_VPAD),
    )

    n_p, n_s = batch * seq, dec_batch * dec_seq
    nb_p = n_p // TM
    row_p = lambda i: 0
    row_s = lambda i: 1 + (i * TM) // DEC_SEQ
    row_all = lambda i: jnp.where(i < nb_p, 0, 1 + ((i - nb_p) * TM) // DEC_SEQ)

    parts = [x_prompt.reshape(n_p, D_MODEL), x_sample.reshape(n_s, D_MODEL)]
    saved = []
    for l in range(DEPTH):
        gf = g_final if l == DEPTH - 1 else None
        x_all, w2 = _ffn(parts, mod3, l, 0, row_all, g_ff1, w_ff1_in, w_ff1_out, (w_ff2_in, w_ff2_out))
        xp, st = _mixer_ffn(x_all, 0, n_p, mod3, l, row_p, seq, layers[l], None, tables, g_ff2, w2, gf)
        xs, _ = _mixer_ffn(x_all, nb_p, n_s, mod3, l, row_s, dec_seq, layers[l], ctx, tables, g_ff2, w2, gf)
        saved.append(st)
        parts = [xp, xs]
    y_prompt, y_sample = parts
    (ckv_new, kr_new), (k_new, v_new) = saved

    return (y_prompt.reshape(batch, seq, D_MODEL),
            y_sample.reshape(dec_batch, dec_seq, D_MODEL),
            ckv_new.reshape(batch, 1, seq, KV_LORA),
            kr_new.reshape(batch, 1, seq, QK_ROPE),
            k_new.reshape(batch, 1, seq, GQA_KV_HEADS, GQA_HEAD_DIM),
            v_new.reshape(batch, 1, seq, GQA_KV_HEADS, GQA_HEAD_DIM))
```

```python
import functools

import jax
import jax.numpy as jnp
import numpy as np
from jax import lax
from jax.experimental import pallas as pl
from jax.experimental.pallas import tpu as pltpu

F32 = jnp.float32
BF16 = jnp.bfloat16

D_MODEL = 1024
DEPTH = 2
DEC_SEQ = 2048
GRID_W = 64
MLA_HEADS = 8
Q_LORA = 384
KV_LORA = 256
QK_NOPE = 64
QK_ROPE = 32
V_HEAD = 64
CONV_CH = 512
CONV_W = 31
CONV_PAD = CONV_W // 2
GQA_HEADS = 16
GQA_KV_HEADS = 4
GQA_HEAD_DIM = 64
D_FF = 2816
MACARON = 0.5
N_MOD = 9
ROPE_BASE = 10000.0
EPS = 1e-6

LANES = 128
HEAD_PAD = 128
MOD_ROWS = 8
DENOM_LANE = 64
LOG2E = 1.4426950408889634
HALO = 16
TM = 512
Q_SUB = 256
FF_CHUNK = 256
CONV_ROWS = 64
INTERLEAVE_SLACK = 1
VMEM_LIMIT = 60 * 1024 * 1024


def _params(n_axes):
    return pltpu.CompilerParams(dimension_semantics=("parallel",) * n_axes, vmem_limit_bytes=VMEM_LIMIT)


def _const_spec(shape):
    return pl.BlockSpec(shape, lambda *_: (0,) * len(shape), pipeline_mode=pl.Buffered(1))


def _rms(x, g):
    return x * lax.rsqrt(jnp.mean(x * x, axis=-1, keepdims=True) + EPS) * g


def _mod_norm(x, g, mod):
    return _rms(x, g) * (1.0 + mod[1:2]) + mod[0:1]


def _silu(x):
    return x * jax.nn.sigmoid(x)


def _rope(x, cos, sa, sb, quarter):
    w = x.shape[-1]
    return x * cos + pltpu.roll(x, w - quarter, 1) * sa + pltpu.roll(x, quarter, 1) * sb


def _with_ones_lane(v):
    lane = lax.broadcasted_iota(jnp.int32, v.shape, 1)
    return jnp.where(lane % HEAD_PAD == DENOM_LANE, 1.0, v)


def _tile_lanes(t, reps):
    return t if reps == 1 else jnp.concatenate([t] * reps, axis=1)


def _mod_kernel(c_ref, w_ref, b_ref, o_ref):
    e = _silu(c_ref[...]).astype(BF16)
    o_ref[...] = jnp.dot(e, w_ref[...].astype(BF16), preferred_element_type=F32) + b_ref[...]


def _modulation(cond, w_mod, b_mod):
    n_out = N_MOD * D_MODEL
    tn = 1536
    out = pl.pallas_call(
        _mod_kernel,
        grid=(DEPTH, n_out // tn),
        in_specs=[
            pl.BlockSpec((MOD_ROWS, D_MODEL), lambda l, j: (0, 0)),
            pl.BlockSpec((None, D_MODEL, tn), lambda l, j: (l, 0, j)),
            pl.BlockSpec((None, 1, tn), lambda l, j: (l, 0, j)),
        ],
        out_specs=pl.BlockSpec((None, MOD_ROWS, tn), lambda l, j: (l, 0, j)),
        out_shape=jax.ShapeDtypeStruct((DEPTH, MOD_ROWS, n_out), F32),
        compiler_params=_params(2),
        name="modulation",
    )(cond, w_mod, b_mod.reshape(DEPTH, 1, n_out))
    return out.reshape(DEPTH * MOD_ROWS * 3, 3, D_MODEL)


def _mod_spec(layer, sub, row_fn):
    return pl.BlockSpec((None, 3, D_MODEL), lambda i, *_: ((layer * MOD_ROWS + row_fn(i)) * 3 + sub, 0, 0))


def _cast_ffn2_slabs(wi_ref, wo_ref, wi_out, wo_out):
    wi_out[...] = wi_ref[...].astype(BF16)
    wo_out[...] = wo_ref[...].astype(BF16)


def _cast_ffn2_specs(layer, steps, w_in, w_out):
    shapes = [w_in.shape[1:], w_out.shape[1:]]
    assert all(k % (steps * 16) == 0 for k, _ in shapes)
    in_specs = [pl.BlockSpec((None, k // steps, n), lambda i: (layer, i, 0)) for k, n in shapes]
    out_specs = [pl.BlockSpec((k // steps, n), lambda i: (i, 0)) for k, n in shapes]
    return in_specs, out_specs, [jax.ShapeDtypeStruct(s, BF16) for s in shapes]


def _ffn_body(x, mod, g_ref, win_ref, wout_ref, between_chunks=None, before_chunk=None):
    h = _mod_norm(x, g_ref[...], mod).astype(BF16)
    acc = jnp.zeros(x.shape, F32)
    zeros = [None] * INTERLEAVE_SLACK
    for j in range(D_FF // FF_CHUNK):
        lo = j * FF_CHUNK
        if before_chunk is not None:
            before_chunk(j)
        a = jnp.dot(h, win_ref[:, lo:lo + FF_CHUNK].astype(BF16), preferred_element_type=F32)
        b = jnp.dot(h, win_ref[:, D_FF + lo:D_FF + lo + FF_CHUNK].astype(BF16), preferred_element_type=F32)
        act = (_silu(a) * b).astype(BF16)
        zero = zeros.pop(0)
        rows = pl.ds(lo, FF_CHUNK) if zero is None else pl.ds(pl.multiple_of(lo + zero, FF_CHUNK), FF_CHUNK)
        acc = acc + jnp.dot(act, wout_ref[rows, :].astype(BF16), preferred_element_type=F32)
        zeros.append(between_chunks(j) if between_chunks is not None else None)
    return x + (MACARON * mod[2:3]) * acc


def _ffn_kernel(*refs, nparts, nblk0, layer):
    x_refs = refs[:nparts]
    mod_ref, g_ref, win_hbm, wout_hbm, o_ref, win_ref, wout_ref, sems = refs[nparts:]
    i = pl.program_id(0)
    x = x_refs[0][...]
    if nparts == 2:
        x = jnp.where(i < nblk0, x, x_refs[1][...])

    def chunk_copies(j):
        lo = j * FF_CHUNK
        cols = [pl.ds(lo, FF_CHUNK), pl.ds(D_FF + lo, FF_CHUNK)]
        return ([pltpu.make_async_copy(win_hbm.at[layer, :, c], win_ref.at[:, c], sems.at[j, k])
                 for k, c in enumerate(cols)]
                + [pltpu.make_async_copy(wout_hbm.at[layer, cols[0], :], wout_ref.at[cols[0], :], sems.at[j, 2])])

    def wait_chunk(j):
        for copy in chunk_copies(j):
            copy.wait()

    @pl.when(i == 0)
    def _():
        for j in range(D_FF // FF_CHUNK):
            for copy in chunk_copies(j):
                copy.start()
        o_ref[...] = _ffn_body(x, mod_ref[...], g_ref, win_ref, wout_ref, before_chunk=wait_chunk)

    @pl.when(i > 0)
    def _():
        o_ref[...] = _ffn_body(x, mod_ref[...], g_ref, win_ref, wout_ref)


def _ffn_weight_specs(layer):
    layer_spec = lambda shape: pl.BlockSpec((None,) + shape, lambda i: (layer, 0, 0), pipeline_mode=pl.Buffered(1))
    return [layer_spec((1, D_MODEL)), layer_spec((D_MODEL, 2 * D_FF)), layer_spec((D_FF, D_MODEL))]


def _ffn(parts, mod3, layer, sub, row_fn, g, w_in, w_out):
    nblks = [p.shape[0] // TM for p in parts]
    if len(parts) == 1:
        in_specs = [pl.BlockSpec((TM, D_MODEL), lambda i: (i, 0))]
    else:
        nb0 = nblks[0]
        in_specs = [pl.BlockSpec((TM, D_MODEL), lambda i: (jnp.minimum(i, nb0 - 1), 0)),
                    pl.BlockSpec((TM, D_MODEL), lambda i: (jnp.maximum(i - nb0, 0), 0))]
    in_specs += [_mod_spec(layer, sub, row_fn), _ffn_weight_specs(layer)[0],
                 pl.BlockSpec(memory_space=pl.ANY), pl.BlockSpec(memory_space=pl.ANY)]
    n = sum(nblks) * TM
    return pl.pallas_call(
        functools.partial(_ffn_kernel, nparts=len(parts), nblk0=nblks[0], layer=layer),
        grid=(n // TM,),
        in_specs=in_specs,
        out_specs=pl.BlockSpec((TM, D_MODEL), lambda i: (i, 0)),
        out_shape=jax.ShapeDtypeStruct((n, D_MODEL), F32),
        scratch_shapes=[pltpu.VMEM(w_in.shape[1:], w_in.dtype), pltpu.VMEM(w_out.shape[1:], w_out.dtype),
                        pltpu.SemaphoreType.DMA((D_FF // FF_CHUNK, 3))],
        compiler_params=pltpu.CompilerParams(dimension_semantics=("arbitrary",), vmem_limit_bytes=VMEM_LIMIT),
        name="ffn",
    )(*parts, mod3, g.reshape(DEPTH, 1, D_MODEL), w_in, w_out)


MLA_W1_COLS = Q_LORA + KV_LORA + 2 * CONV_CH + LANES
MLA_QK = MLA_HEADS * HEAD_PAD
KR_LANE = QK_NOPE


def _mla_kv(ckvn, krb, wkv_ref, kt_ref, v_ref, nseq):
    kv = jnp.dot(ckvn.astype(BF16), wkv_ref[...], preferred_element_type=F32)
    k = kv[:, :MLA_QK] + _tile_lanes(krb, MLA_HEADS)
    v_ref[...] = _with_ones_lane(kv[:, MLA_QK:]).astype(BF16)
    rows = k.shape[0] // nseq
    for s in range(nseq):
        kt_ref[s] = k[s * rows:(s + 1) * rows, :].T.astype(BF16)


def _mla_in_kernel(x_ref, mod_ref, g_ref, w1_ref, gq_ref, wqu_ref, gkv_ref, wkv_ref, *rest, rope, nseq):
    if rope:
        cos_ref, sa_ref, sb_ref, q_ref, kt_ref, v_ref, glu_ref = rest
    else:
        wi_ref, wo_ref, q_ref, kt_ref, v_ref, glu_ref, ckv_ref, kr_ref, wi_out, wo_out = rest
        _cast_ffn2_slabs(wi_ref, wo_ref, wi_out, wo_out)
    h = _mod_norm(x_ref[...], g_ref[...], mod_ref[...]).astype(BF16)
    proj = jnp.dot(h, w1_ref[...], preferred_element_type=F32)
    o1 = Q_LORA
    o2 = o1 + KV_LORA
    o3 = o2 + CONV_CH
    o4 = o3 + CONV_CH
    cq, ckv, ua, ub, krb = proj[:, :o1], proj[:, o1:o2], proj[:, o2:o3], proj[:, o3:o4], proj[:, o4:]
    glu_ref[...] = ua * jax.nn.sigmoid(ub)
    q = jnp.dot(_rms(cq, gq_ref[...]).astype(BF16), wqu_ref[...], preferred_element_type=F32)
    ckvn = _rms(ckv, gkv_ref[...])
    if rope:
        cos, sa, sb = cos_ref[...], sa_ref[...], sb_ref[...]
        quarter = QK_ROPE // 4
        krb = _rope(krb, cos, sa, sb, quarter)
        q = _rope(q, _tile_lanes(cos, MLA_HEADS), _tile_lanes(sa, MLA_HEADS), _tile_lanes(sb, MLA_HEADS), quarter)
    else:
        ckv_ref[...] = ckvn
        kr_ref[...] = krb[:, KR_LANE:KR_LANE + QK_ROPE]
    q_ref[...] = (q * ((QK_NOPE + QK_ROPE) ** -0.5 * LOG2E)).astype(BF16)
    _mla_kv(ckvn, krb, wkv_ref, kt_ref, v_ref, nseq)


def _mla_ctx_kernel(ckv_ref, krb_ref, wkv_ref, kt_ref, v_ref):
    _mla_kv(ckv_ref[...], krb_ref[...], wkv_ref, kt_ref, v_ref, 1)


def _mla_in(x, x_off, n, mod3, layer, row_fn, seq, p, tables, ffn2_f32=None):
    rope = tables is not None
    nseq = max(TM // seq, 1)
    rows = TM // nseq
    batch = n // seq
    tiles_per_seq = max(seq // TM, 1)
    row = lambda w: pl.BlockSpec((TM, w), lambda i: (i, 0))
    in_specs = [pl.BlockSpec((TM, D_MODEL), lambda i: (i + x_off, 0)), _mod_spec(layer, 1, row_fn),
                _const_spec((1, D_MODEL)),
                _const_spec((D_MODEL, MLA_W1_COLS)), _const_spec((1, Q_LORA)), _const_spec((Q_LORA, MLA_QK)),
                _const_spec((1, KV_LORA)), _const_spec((KV_LORA, 2 * MLA_QK))]
    args = [x, mod3, p["g_mix"], p["w1"], p["g_q"], p["w_qu"], p["g_kv"], p["w_kv"]]
    kt_spec = pl.BlockSpec((nseq, MLA_QK, rows), lambda i: (i // tiles_per_seq, 0, i % tiles_per_seq))
    out_specs = [row(MLA_QK), kt_spec, row(MLA_QK), row(CONV_CH)]
    out_shape = [jax.ShapeDtypeStruct((n, MLA_QK), BF16), jax.ShapeDtypeStruct((batch, MLA_QK, seq), BF16),
                 jax.ShapeDtypeStruct((n, MLA_QK), BF16), jax.ShapeDtypeStruct((n, CONV_CH), F32)]
    if rope:
        tab = pl.BlockSpec((TM, LANES), lambda i: (i % tiles_per_seq, 0))
        in_specs += [tab, tab, tab]
        args += list(tables)
    else:
        out_specs += [row(KV_LORA), row(QK_ROPE)]
        out_shape += [jax.ShapeDtypeStruct((n, KV_LORA), F32), jax.ShapeDtypeStruct((n, QK_ROPE), F32)]
        cast_in, cast_out, cast_shape = _cast_ffn2_specs(layer, n // TM, *ffn2_f32)
        in_specs += cast_in
        args += list(ffn2_f32)
        out_specs += cast_out
        out_shape += cast_shape
    return pl.pallas_call(
        functools.partial(_mla_in_kernel, rope=rope, nseq=nseq),
        grid=(n // TM,),
        in_specs=in_specs,
        out_specs=out_specs,
        out_shape=out_shape,
        compiler_params=_params(1),
        name="mla_in_rope" if rope else "mla_in",
    )(*args)


def _mla_ctx(ckv, krb, w_kv, seq):
    n = ckv.shape[0]
    return pl.pallas_call(
        _mla_ctx_kernel,
        grid=(n // seq,),
        in_specs=[pl.BlockSpec((seq, KV_LORA), lambda i: (i, 0)), pl.BlockSpec((seq, LANES), lambda i: (i, 0)),
                  _const_spec((KV_LORA, 2 * MLA_QK))],
        out_specs=[pl.BlockSpec((1, MLA_QK, seq), lambda i: (i, 0, 0)), pl.BlockSpec((seq, MLA_QK), lambda i: (i, 0))],
        out_shape=[jax.ShapeDtypeStruct((n // seq, MLA_QK, seq), BF16), jax.ShapeDtypeStruct((n, MLA_QK), BF16)],
        compiler_params=_params(1),
        name="mla_ctx",
    )(ckv, krb, w_kv)


GQA_Q = GQA_HEADS * GQA_HEAD_DIM
GQA_KV = GQA_KV_HEADS * GQA_HEAD_DIM
GQA_VPAD = GQA_KV_HEADS * HEAD_PAD
GQA_W_COLS = GQA_Q + 2 * GQA_KV + GQA_VPAD


def _head_rms(x, g):
    tm, w = x.shape
    lo = lax.broadcasted_iota(jnp.int32, (tm, LANES), 1) < GQA_HEAD_DIM
    outs = []
    for b in range(w // LANES):
        xb = x[:, b * LANES:(b + 1) * LANES]
        sq = xb * xb
        s_lo = jnp.sum(jnp.where(lo, sq, 0.0), axis=-1, keepdims=True)
        s_hi = jnp.sum(jnp.where(lo, 0.0, sq), axis=-1, keepdims=True)
        ms = jnp.where(lo, s_lo, s_hi) * (1.0 / GQA_HEAD_DIM)
        outs.append(xb * lax.rsqrt(ms + EPS))
    return jnp.concatenate(outs, axis=1) * g


def _gqa_in_kernel(x_ref, mod_ref, g_ref, w_ref, gq_ref, gk_ref, *rest, rope, nseq):
    if rope:
        cos_ref, sa_ref, sb_ref, q_ref, kt_ref, v_ref = rest
    else:
        wi_ref, wo_ref, q_ref, kt_ref, v_ref, kc_ref, vc_ref, wi_out, wo_out = rest
        _cast_ffn2_slabs(wi_ref, wo_ref, wi_out, wo_out)
    h = _mod_norm(x_ref[...], g_ref[...], mod_ref[...]).astype(BF16)
    proj = jnp.dot(h, w_ref[...], preferred_element_type=F32)
    q = _head_rms(proj[:, :GQA_Q], gq_ref[...])
    k = _head_rms(proj[:, GQA_Q:GQA_Q + GQA_KV], gk_ref[...])
    if rope:
        cos, sa, sb = cos_ref[...], sa_ref[...], sb_ref[...]
        quarter = GQA_HEAD_DIM // 4
        q = _rope(q, _tile_lanes(cos, GQA_Q // LANES), _tile_lanes(sa, GQA_Q // LANES),
                  _tile_lanes(sb, GQA_Q // LANES), quarter)
        k = _rope(k, _tile_lanes(cos, GQA_KV // LANES), _tile_lanes(sa, GQA_KV // LANES),
                  _tile_lanes(sb, GQA_KV // LANES), quarter)
    else:
        kc_ref[...] = k
        vc_ref[...] = proj[:, GQA_Q + GQA_KV:GQA_Q + 2 * GQA_KV]
    q_ref[...] = (q * (GQA_HEAD_DIM ** -0.5 * LOG2E)).astype(BF16)
    v_ref[...] = _with_ones_lane(proj[:, GQA_Q + 2 * GQA_KV:]).astype(BF16)
    rows = k.shape[0] // nseq
    for s in range(nseq):
        kt_ref[s] = k[s * rows:(s + 1) * rows, :].T.astype(BF16)


def _gqa_ctx_kernel(k_ref, v_ref, kt_ref, vo_ref):
    kt_ref[0] = k_ref[...].T.astype(BF16)
    vo_ref[...] = _with_ones_lane(v_ref[...]).astype(BF16)


def _gqa_in(x, x_off, n, mod3, layer, row_fn, seq, p, tables, ffn2_f32=None):
    rope = tables is not None
    nseq = max(TM // seq, 1)
    rows = TM // nseq
    batch = n // seq
    tiles_per_seq = max(seq // TM, 1)
    row = lambda w: pl.BlockSpec((TM, w), lambda i: (i, 0))
    in_specs = [pl.BlockSpec((TM, D_MODEL), lambda i: (i + x_off, 0)), _mod_spec(layer, 1, row_fn),
                _const_spec((1, D_MODEL)),
                _const_spec((D_MODEL, GQA_W_COLS)), _const_spec((1, GQA_Q)), _const_spec((1, GQA_KV))]
    args = [x, mod3, p["g_mix"], p["w"], p["g_q"], p["g_k"]]
    kt_spec = pl.BlockSpec((nseq, GQA_KV, rows), lambda i: (i // tiles_per_seq, 0, i % tiles_per_seq))
    out_specs = [row(GQA_Q), kt_spec, row(GQA_VPAD)]
    out_shape = [jax.ShapeDtypeStruct((n, GQA_Q), BF16), jax.ShapeDtypeStruct((batch, GQA_KV, seq), BF16),
                 jax.ShapeDtypeStruct((n, GQA_VPAD), BF16)]
    if rope:
        tab = pl.BlockSpec((TM, LANES), lambda i: (i % tiles_per_seq, 0))
        in_specs += [tab, tab, tab]
        args += list(tables)
    else:
        out_specs += [row(GQA_KV), row(GQA_KV)]
        out_shape += [jax.ShapeDtypeStruct((n, GQA_KV), F32), jax.ShapeDtypeStruct((n, GQA_KV), F32)]
        cast_in, cast_out, cast_shape = _cast_ffn2_specs(layer, n // TM, *ffn2_f32)
        in_specs += cast_in
        args += list(ffn2_f32)
        out_specs += cast_out
        out_shape += cast_shape
    return pl.pallas_call(
        functools.partial(_gqa_in_kernel, rope=rope, nseq=nseq),
        grid=(n // TM,),
        in_specs=in_specs,
        out_specs=out_specs,
        out_shape=out_shape,
        compiler_params=_params(1),
        name="gqa_in_rope" if rope else "gqa_in",
    )(*args)


def _gqa_ctx(k, vpad, seq):
    n = k.shape[0]
    return pl.pallas_call(
        _gqa_ctx_kernel,
        grid=(n // seq,),
        in_specs=[pl.BlockSpec((seq, GQA_KV), lambda i: (i, 0)), pl.BlockSpec((seq, GQA_VPAD), lambda i: (i, 0))],
        out_specs=[pl.BlockSpec((1, GQA_KV, seq), lambda i: (i, 0, 0)), pl.BlockSpec((seq, GQA_VPAD), lambda i: (i, 0))],
        out_shape=[jax.ShapeDtypeStruct((n // seq, GQA_KV, seq), BF16), jax.ShapeDtypeStruct((n, GQA_VPAD), BF16)],
        compiler_params=_params(1),
        name="gqa_ctx",
    )(k, vpad)


def _attn_kernel(*refs, nseg, bt, seq, nq, nkv, dq, dk, dv):
    q_ref = refs[0]
    segs = [(refs[1 + 2 * s], refs[2 + 2 * s]) for s in range(nseg)]
    o_ref = refs[1 + 2 * nseg]
    s_refs = refs[2 + 2 * nseg:4 + 2 * nseg]
    nks = [kt_ref.shape[2] for kt_ref, _ in segs]
    offs = [sum(nks[:s]) for s in range(nseg)]

    def scores(bi, r0, s_ref):
        rows = pl.ds(bi * seq + r0, Q_SUB)
        for j in range(nq):
            kv = j * nkv // nq
            qs = q_ref[rows, j * dq:(j + 1) * dq]
            for (kt_ref, _), off, nk in zip(segs, offs, nks):
                s_ref[j, :, off:off + nk] = jnp.dot(qs, kt_ref[bi, kv * dk:(kv + 1) * dk, :],
                                                    preferred_element_type=F32)

    def softmax_pv(bi, r0, s_ref):
        rows = pl.ds(bi * seq + r0, Q_SUB)
        outs = []
        for j in range(nq):
            kv = j * nkv // nq
            s = s_ref[j]
            p = jnp.exp2(s - jnp.max(s, axis=-1, keepdims=True)).astype(BF16)
            o = None
            for (_, v_ref), off, nk in zip(segs, offs, nks):
                pv = jnp.dot(p[:, off:off + nk], v_ref[bi * nk:(bi + 1) * nk, kv * HEAD_PAD:(kv + 1) * HEAD_PAD],
                             preferred_element_type=F32)
                o = pv if o is None else o + pv
            outs.append(o[:, :dv] / o[:, DENOM_LANE:DENOM_LANE + 1])
        o_ref[rows, :] = jnp.concatenate(outs, axis=1).astype(BF16)

    if seq == Q_SUB:
        assert bt == len(s_refs)
        for bi in range(bt):
            scores(bi, 0, s_refs[bi])
        for bi in range(bt):
            softmax_pv(bi, 0, s_refs[bi])
    else:
        n_sub = seq // Q_SUB
        assert bt == 1
        scores(0, 0, s_refs[0])
        for r in range(n_sub):
            if r + 1 < n_sub:
                scores(0, (r + 1) * Q_SUB, s_refs[(r + 1) % 2])
            softmax_pv(0, r * Q_SUB, s_refs[r % 2])


def _attention(q, segs, *, seq, heads, kv_heads, dq, dk, dv, bt, heads_per_step):
    n = q.shape[0]
    batch = n // seq
    nq = heads_per_step
    steps = heads // nq
    nkv = max(kv_heads // steps, 1)
    q_per_kv_block = steps // (kv_heads // nkv)
    in_specs = [pl.BlockSpec((bt * seq, nq * dq), lambda b, p: (b, p))]
    args = [q]
    for kt, v in segs:
        nk = kt.shape[2]
        in_specs.append(pl.BlockSpec((bt, nkv * dk, nk), lambda b, p: (b, p // q_per_kv_block, 0)))
        in_specs.append(pl.BlockSpec((bt * nk, nkv * HEAD_PAD), lambda b, p: (b, p // q_per_kv_block)))
        args += [kt, v]
    return pl.pallas_call(
        functools.partial(_attn_kernel, nseg=len(segs), bt=bt, seq=seq, nq=nq, nkv=nkv, dq=dq, dk=dk, dv=dv),
        grid=(batch // bt, steps),
        in_specs=in_specs,
        out_specs=pl.BlockSpec((bt * seq, nq * dv), lambda b, p: (b, p)),
        out_shape=jax.ShapeDtypeStruct((n, heads * dv), BF16),
        scratch_shapes=[pltpu.VMEM((nq, Q_SUB, sum(kt.shape[2] for kt, _ in segs)), F32)] * 2,
        compiler_params=_params(2),
        name="attention",
    )(*args)


SUBLANES = 8
CONV_LEAD = HALO - CONV_PAD
CONV_WIN = CONV_ROWS + 2 * HALO


def _fill_pad(pad_ref, glu_ref, prev_ref, next_ref, first, last, nseq):
    rows = TM // nseq
    zeros = jnp.zeros((HALO, CONV_CH), F32)
    for s in range(nseq):
        pad_ref[s, 0:HALO, :] = zeros if prev_ref is None else jnp.where(first, zeros, prev_ref[...])
        pad_ref[s, HALO + rows:2 * HALO + rows, :] = (
            zeros if next_ref is None else jnp.where(last, zeros, next_ref[...]))
        pad_ref[s, HALO:HALO + rows, :] = glu_ref[s * rows:(s + 1) * rows, :]


CONV_LANE_BLOCKS = CONV_CH // LANES


def _conv_block(pad_ref, s, r0, lb, wdw_ref, bdw_ref, zero_ref, prev_zero):
    lanes = slice(lb * LANES, (lb + 1) * LANES)
    win = pad_ref[s, pl.ds(r0, CONV_WIN), lanes]
    acc = jnp.broadcast_to(bdw_ref[:, lanes], (CONV_ROWS, LANES))
    if prev_zero is not None:
        acc = acc + jnp.concatenate([prev_zero] * (CONV_ROWS // SUBLANES), axis=0)
    for phase in range(SUBLANES):
        shifted = win if phase == 0 else pltpu.roll(win, CONV_WIN - phase, 0)
        for a in range((CONV_W + CONV_LEAD) // SUBLANES + 1):
            k = a * SUBLANES + phase - CONV_LEAD
            if 0 <= k < CONV_W:
                acc = acc + shifted[a * SUBLANES:a * SUBLANES + CONV_ROWS] * wdw_ref[k:k + 1, lanes]
    peak = functools.reduce(jnp.maximum, [acc[g * SUBLANES:(g + 1) * SUBLANES]
                                          for g in range(CONV_ROWS // SUBLANES)])
    zero = (peak.astype(jnp.int32) & jnp.full((SUBLANES, LANES), zero_ref[0], jnp.int32)).astype(F32)
    return acc, zero


def _conv_finish(blocks, gln_ref, bln_ref):
    acc = jnp.concatenate(blocks, axis=1)
    mu = jnp.mean(acc, axis=-1, keepdims=True)
    d = acc - mu
    var = jnp.mean(d * d, axis=-1, keepdims=True)
    return _silu(d * lax.rsqrt(var + EPS) * gln_ref[...] + bln_ref[...])


def _out_ffn_kernel(*refs, conv, nseq, halo, tiles_per_seq, final):
    it = iter(refs)
    x_ref, modm_ref, modf_ref, attn_ref = next(it), next(it), next(it), next(it)
    if conv:
        glu0_ref = next(it)
        next0_ref = next(it) if halo else None
        glun_ref = next(it)
        prevn_ref, nextn_ref = (next(it), next(it)) if halo else (None, None)
        conv_w = [next(it) for _ in range(4)]
        zero_ref = next(it)
    wout_ref, gff_ref, win_ref, wff_ref = next(it), next(it), next(it), next(it)
    gfin_ref = next(it) if final else None
    y_ref = next(it)
    i = pl.program_id(0)
    rows = TM // nseq

    if conv:
        pad_ref, g_ref = next(it), next(it)

        wdw_ref, bdw_ref, gln_ref, bln_ref = conv_w
        units = [(c, lb) for c in range(TM // CONV_ROWS) for lb in range(CONV_LANE_BLOCKS)]
        hosts = D_FF // FF_CHUNK - INTERLEAVE_SLACK
        counts = [len(units) // hosts + (j < len(units) % hosts) for j in range(hosts)]
        starts = [sum(counts[:j]) for j in range(hosts + 1)]

        def run_units(todo, state):
            for c, lb in todo:
                s, r0 = divmod(c * CONV_ROWS, rows)
                acc, state[0] = _conv_block(pad_ref, s, r0, lb, wdw_ref, bdw_ref, zero_ref, state[0])
                state[1].append(acc)
                if lb == CONV_LANE_BLOCKS - 1:
                    g = _conv_finish(state[1], gln_ref, bln_ref)
                    g_ref[c * CONV_ROWS:(c + 1) * CONV_ROWS, :] = g.astype(BF16)
                    state[1] = []

        @pl.when(i == 0)
        def _():
            _fill_pad(pad_ref, glu0_ref, None, next0_ref, True, tiles_per_seq == 1, nseq)
            run_units(units, [None, []])

        out = (jnp.dot(attn_ref[...], wout_ref[:CONV_CH, :], preferred_element_type=F32)
               + jnp.dot(g_ref[...], wout_ref[CONV_CH:, :], preferred_element_type=F32))
        pos = (i + 1) % tiles_per_seq
        _fill_pad(pad_ref, glun_ref, prevn_ref, nextn_ref, pos == 0, pos == tiles_per_seq - 1, nseq)
        state = [None, []]

        def between(j):
            if j >= hosts:
                return None
            todo = units[starts[j]:starts[j + 1]]
            run_units(todo, state)
            return jnp.max(state[0]).astype(jnp.int32)
    else:
        out = jnp.dot(attn_ref[...], wout_ref[...], preferred_element_type=F32)
        between = None
    x = x_ref[...] + modm_ref[2:3, :] * out
    y = _ffn_body(x, modf_ref[...], gff_ref, win_ref, wff_ref, between)
    if final:
        y = _rms(y, gfin_ref[...])
    y_ref[...] = y


def _out_ffn(x, x_off, mod3, layer, row_fn, seq, attn, w_out, ffn_w, glu=None, conv_p=None, g_final=None):
    n = attn.shape[0]
    n_tiles = n // TM
    conv = glu is not None
    final = g_final is not None
    nseq = max(TM // seq, 1)
    tiles_per_seq = max(seq // TM, 1)
    halo = conv and tiles_per_seq > 1
    row = lambda w: pl.BlockSpec((TM, w), lambda i: (i, 0))
    in_specs = [pl.BlockSpec((TM, D_MODEL), lambda i: (i + x_off, 0)), _mod_spec(layer, 1, row_fn),
                _mod_spec(layer, 2, row_fn), row(attn.shape[1])]
    args = [x, mod3, mod3, attn]
    scratch = []
    if conv:
        per = TM // HALO
        last = n // HALO - 1
        nxt = lambda i: jnp.minimum(i + 1, n_tiles - 1)
        in_specs.append(pl.BlockSpec((TM, CONV_CH), lambda i: (0, 0)))
        args.append(glu)
        if halo:
            in_specs.append(pl.BlockSpec((HALO, CONV_CH), lambda i: (per, 0)))
            args.append(glu)
        in_specs.append(pl.BlockSpec((TM, CONV_CH), lambda i: (nxt(i), 0)))
        args.append(glu)
        if halo:
            in_specs += [pl.BlockSpec((HALO, CONV_CH), lambda i: (jnp.maximum(nxt(i) * per - 1, 0), 0)),
                         pl.BlockSpec((HALO, CONV_CH), lambda i: (jnp.minimum((nxt(i) + 1) * per, last), 0))]
            args += [glu, glu]
        in_specs += [_const_spec((CONV_W + 1, CONV_CH)), _const_spec((1, CONV_CH)), _const_spec((1, CONV_CH)),
                     _const_spec((1, CONV_CH)), pl.BlockSpec(memory_space=pltpu.SMEM)]
        args += [conv_p["w_dw"], conv_p["b_dw"], conv_p["g_ln"], conv_p["b_ln"], jnp.zeros((1,), jnp.int32)]
        scratch = [pltpu.VMEM((nseq, TM // nseq + 2 * HALO, CONV_CH), F32), pltpu.VMEM((TM, CONV_CH), BF16)]
    g_ff, w_in, w_ff = ffn_w
    in_specs += [_const_spec(w_out.shape), _ffn_weight_specs(layer)[0], _const_spec(w_in.shape),
                 _const_spec(w_ff.shape)]
    args += [w_out, g_ff.reshape(DEPTH, 1, D_MODEL), w_in, w_ff]
    if final:
        in_specs.append(_const_spec((1, D_MODEL)))
        args.append(g_final.reshape(1, D_MODEL))
    return pl.pallas_call(
        functools.partial(_out_ffn_kernel, conv=conv, nseq=nseq, halo=halo, tiles_per_seq=tiles_per_seq,
                          final=final),
        grid=(n_tiles,),
        in_specs=in_specs,
        out_specs=row(D_MODEL),
        out_shape=jax.ShapeDtypeStruct((n, D_MODEL), F32),
        scratch_shapes=scratch,
        compiler_params=pltpu.CompilerParams(dimension_semantics=("arbitrary",), vmem_limit_bytes=VMEM_LIMIT),
        name="out_ffn_conv" if conv else "out_ffn",
    )(*args)


def _rope_tables(d, offset, period):
    half = d // 2
    quarter = half // 2
    rows = DEC_SEQ // GRID_W
    pos_row = np.repeat(np.arange(rows), GRID_W)
    pos_col = np.tile(np.arange(GRID_W), rows)
    inv = ROPE_BASE ** (-np.arange(0, half, 2, dtype=np.float64) / half)
    ang_r = pos_row.astype(np.float64)[:, None] * inv[None, :]
    ang_c = pos_col.astype(np.float64)[:, None] * inv[None, :]
    zero = np.zeros((DEC_SEQ, quarter))
    cos = np.concatenate([np.cos(ang_r), np.cos(ang_r), np.cos(ang_c), np.cos(ang_c)], axis=1)
    sa = np.concatenate([-np.sin(ang_r), zero, -np.sin(ang_c), zero], axis=1)
    sb = np.concatenate([zero, np.sin(ang_r), zero, np.sin(ang_c)], axis=1)

    def embed(t, fill):
        blk = np.concatenate([np.full((DEC_SEQ, offset), fill), t,
                              np.full((DEC_SEQ, period - offset - d), fill)], axis=1)
        return jnp.asarray(np.tile(blk, (1, LANES // period)), dtype=F32)

    return embed(cos, 1.0), embed(sa, 0.0), embed(sb, 0.0)


def _pad_heads(w, heads, dim):
    k = w.shape[0]
    return jnp.pad(w.reshape(k, heads, dim), ((0, 0), (0, 0), (0, HEAD_PAD - dim))).reshape(k, heads * HEAD_PAD)


def _prep_mla(i, g_mix_l, w_in_a, g_q_lora, w_q_up, g_kv_lora, w_kv_up, w_dw, b_dw, g_conv_ln, b_conv_ln, w_out_a):
    w = w_in_a[i]
    o1 = Q_LORA
    o2 = o1 + KV_LORA
    o3 = o2 + QK_ROPE
    zeros = lambda c: jnp.zeros((D_MODEL, c), F32)
    w1 = jnp.concatenate([w[:, :o2], w[:, o3:], zeros(KR_LANE), w[:, o2:o3],
                          zeros(LANES - KR_LANE - QK_ROPE)], axis=1)
    kvu = w_kv_up[i].reshape(KV_LORA, MLA_HEADS, QK_NOPE + V_HEAD)
    w_kn = _pad_heads(kvu[:, :, :QK_NOPE].reshape(KV_LORA, MLA_HEADS * QK_NOPE), MLA_HEADS, QK_NOPE)
    w_v = _pad_heads(kvu[:, :, QK_NOPE:].reshape(KV_LORA, MLA_HEADS * V_HEAD), MLA_HEADS, V_HEAD)
    return dict(
        g_mix=g_mix_l.reshape(1, D_MODEL),
        w1=w1.astype(BF16),
        g_q=g_q_lora[i].reshape(1, Q_LORA),
        w_qu=_pad_heads(w_q_up[i], MLA_HEADS, QK_NOPE + QK_ROPE).astype(BF16),
        g_kv=g_kv_lora[i].reshape(1, KV_LORA),
        w_kv=jnp.concatenate([w_kn, w_v], axis=1).astype(BF16),
        w_dw=jnp.pad(w_dw[i], ((0, 1), (0, 0))),
        b_dw=b_dw[i].reshape(1, CONV_CH),
        g_ln=g_conv_ln[i].reshape(1, CONV_CH),
        b_ln=b_conv_ln[i].reshape(1, CONV_CH),
        w_out=w_out_a[i].astype(BF16),
    )


def _prep_gqa(i, g_mix_l, w_in_c, g_q_head, g_k_head, w_out_c):
    w = w_in_c[i]
    v_pad = _pad_heads(w[:, GQA_Q + GQA_KV:], GQA_KV_HEADS, GQA_HEAD_DIM)
    return dict(
        g_mix=g_mix_l.reshape(1, D_MODEL),
        w=jnp.concatenate([w, v_pad], axis=1).astype(BF16),
        g_q=jnp.tile(g_q_head[i], GQA_HEADS).reshape(1, GQA_Q),
        g_k=jnp.tile(g_k_head[i], GQA_KV_HEADS).reshape(1, GQA_KV),
        w_out=w_out_c[i].astype(BF16),
    )


SAMPLE_HEADS_PER_STEP = 4
PROMPT_SEQS_PER_STEP = 2


def _mixer_ffn(x_all, x_off, n, mod3, l, row_fn, seq, p, ctx, tables, g_ff2, ffn2_w, g_final):
    sample = ctx is not None
    in_proj, n_core = (_mla_in, 4) if l % 2 == 0 else (_gqa_in, 3)
    outs = in_proj(x_all, x_off, n, mod3, l, row_fn, seq, p, tables["mla" if l % 2 == 0 else "gqa"] if sample else None,
                   None if sample else ffn2_w)
    if not sample:
        outs, ffn2_w = outs[:-2], tuple(outs[-2:])
    ffn_w = (g_ff2,) + tuple(ffn2_w)
    if l % 2 == 0:
        q, kt, v, glu = outs[:4]
        segs = [(kt, v)]
        if sample:
            segs = [_mla_ctx(ctx["mla_ckv"], ctx["mla_krb"], p["w_kv"], ctx["past"])] + segs
            hps, bt = SAMPLE_HEADS_PER_STEP, 1
        else:
            hps, bt = MLA_HEADS, PROMPT_SEQS_PER_STEP
        attn = _attention(q, segs, seq=seq, heads=MLA_HEADS, kv_heads=MLA_HEADS, dq=HEAD_PAD, dk=HEAD_PAD,
                          dv=V_HEAD, bt=bt, heads_per_step=hps)
        return _out_ffn(x_all, x_off, mod3, l, row_fn, seq, attn, p["w_out"], ffn_w, glu=glu, conv_p=p,
                        g_final=g_final), outs[n_core:], ffn2_w
    q, kt, v = outs[:3]
    segs = [(kt, v)]
    if sample:
        segs = [_gqa_ctx(ctx["gqa_k"], ctx["gqa_vpad"], ctx["past"])] + segs
        hps, bt = SAMPLE_HEADS_PER_STEP, 1
    else:
        hps, bt = GQA_HEADS, PROMPT_SEQS_PER_STEP
    attn = _attention(q, segs, seq=seq, heads=GQA_HEADS, kv_heads=GQA_KV_HEADS, dq=GQA_HEAD_DIM,
                      dk=GQA_HEAD_DIM, dv=GQA_HEAD_DIM, bt=bt, heads_per_step=hps)
    return (_out_ffn(x_all, x_off, mod3, l, row_fn, seq, attn, p["w_out"], ffn_w, g_final=g_final), outs[n_core:],
            ffn2_w)


def kernel(x_prompt, x_sample, cache_mla_ckv, cache_mla_krope, cache_gqa_k, cache_gqa_v, c, c_ctx, g_ff1, w_ff1_in, w_ff1_out, g_mix, g_ff2, w_ff2_in, w_ff2_out, w_mod, b_mod, w_in_a, g_q_lora, w_q_up, g_kv_lora, w_kv_up, w_dw, b_dw, g_conv_ln, b_conv_ln, w_out_a, w_in_c, g_q_head, g_k_head, w_out_c, g_final):
    batch, seq, _ = x_prompt.shape
    dec_batch, dec_seq, _ = x_sample.shape
    past = cache_mla_ckv.shape[2]
    assert DEPTH == 2 and dec_seq == DEC_SEQ and 1 + dec_batch <= MOD_ROWS

    cond = jnp.concatenate([c_ctx[None, :], c, jnp.zeros((MOD_ROWS - 1 - dec_batch, D_MODEL), F32)], axis=0)
    mod3 = _modulation(cond, w_mod, b_mod)

    layers = [
        _prep_mla(0, g_mix[0], w_in_a, g_q_lora, w_q_up, g_kv_lora, w_kv_up, w_dw, b_dw, g_conv_ln, b_conv_ln,
                  w_out_a),
        _prep_gqa(0, g_mix[1], w_in_c, g_q_head, g_k_head, w_out_c),
    ]
    tables = dict(mla=_rope_tables(QK_ROPE, KR_LANE, HEAD_PAD), gqa=_rope_tables(GQA_HEAD_DIM, 0, GQA_HEAD_DIM))

    ctx = dict(
        past=past,
        mla_ckv=cache_mla_ckv[:, 0].reshape(dec_batch * past, KV_LORA),
        mla_krb=jnp.pad(cache_mla_krope[:, 0].reshape(dec_batch * past, QK_ROPE),
                        ((0, 0), (KR_LANE, LANES - KR_LANE - QK_ROPE))),
        gqa_k=cache_gqa_k[:, 0].reshape(dec_batch * past, GQA_KV),
        gqa_vpad=jnp.pad(cache_gqa_v[:, 0], ((0, 0), (0, 0), (0, 0), (0, HEAD_PAD - GQA_HEAD_DIM))
                         ).reshape(dec_batch * past, GQA_VPAD),
    )

    n_p, n_s = batch * seq, dec_batch * dec_seq
    nb_p = n_p // TM
    row_p = lambda i: 0
    row_s = lambda i: 1 + (i * TM) // DEC_SEQ
    row_all = lambda i: jnp.where(i < nb_p, 0, 1 + ((i - nb_p) * TM) // DEC_SEQ)

    parts = [x_prompt.reshape(n_p, D_MODEL), x_sample.reshape(n_s, D_MODEL)]
    saved = []
    for l in range(DEPTH):
        gf = g_final if l == DEPTH - 1 else None
        x_all = _ffn(parts, mod3, l, 0, row_all, g_ff1, w_ff1_in, w_ff1_out)
        xp, st, w2 = _mixer_ffn(x_all, 0, n_p, mod3, l, row_p, seq, layers[l], None, tables, g_ff2,
                                (w_ff2_in, w_ff2_out), gf)
        xs, _, _ = _mixer_ffn(x_all, nb_p, n_s, mod3, l, row_s, dec_seq, layers[l], ctx, tables, g_ff2, w2, gf)
        saved.append(st)
        parts = [xp, xs]
    y_prompt, y_sample = parts
    (ckv_new, kr_new), (k_new, v_new) = saved

    return (y_prompt.reshape(batch, seq, D_MODEL),
            y_sample.reshape(dec_batch, dec_seq, D_MODEL),
            ckv_new.reshape(batch, 1, seq, KV_LORA),
            kr_new.reshape(batch, 1, seq, QK_ROPE),
            k_new.reshape(batch, 1, seq, GQA_KV_HEADS, GQA_HEAD_DIM),
            v_new.reshape(batch, 1, seq, GQA_KV_HEADS, GQA_HEAD_DIM))
```

```python
import functools

import jax
import jax.numpy as jnp
import numpy as np
from jax import lax
from jax.experimental import pallas as pl
from jax.experimental.pallas import tpu as pltpu

F32 = jnp.float32
BF16 = jnp.bfloat16

D_MODEL = 1024
DEPTH = 2
DEC_SEQ = 2048
GRID_W = 64
MLA_HEADS = 8
Q_LORA = 384
KV_LORA = 256
QK_NOPE = 64
QK_ROPE = 32
V_HEAD = 64
CONV_CH = 512
CONV_W = 31
CONV_PAD = CONV_W // 2
GQA_HEADS = 16
GQA_KV_HEADS = 4
GQA_HEAD_DIM = 64
D_FF = 2816
MACARON = 0.5
N_MOD = 9
ROPE_BASE = 10000.0
EPS = 1e-6

LANES = 128
HEAD_PAD = 128
MOD_ROWS = 8
DENOM_LANE = 64
LOG2E = 1.4426950408889634
HALO = 16
TM = 512
Q_SUB = 256
FF_CHUNK = 256
CONV_ROWS = 64
INTERLEAVE_SLACK = 1
VMEM_LIMIT = 56 * 1024 * 1024


def _params(n_axes):
    return pltpu.CompilerParams(dimension_semantics=("parallel",) * n_axes, vmem_limit_bytes=VMEM_LIMIT)


def _const_spec(shape):
    return pl.BlockSpec(shape, lambda *_: (0,) * len(shape), pipeline_mode=pl.Buffered(1))


def _rms(x, g):
    return x * lax.rsqrt(jnp.mean(x * x, axis=-1, keepdims=True) + EPS) * g


def _mod_norm(x, g, mod):
    return _rms(x, g) * (1.0 + mod[1:2]) + mod[0:1]


def _silu(x):
    return x * jax.nn.sigmoid(x)


def _rope(x, cos, sa, sb, quarter):
    w = x.shape[-1]
    return x * cos + pltpu.roll(x, w - quarter, 1) * sa + pltpu.roll(x, quarter, 1) * sb


def _with_ones_lane(v):
    lane = lax.broadcasted_iota(jnp.int32, v.shape, 1)
    return jnp.where(lane % HEAD_PAD == DENOM_LANE, 1.0, v)


def _tile_lanes(t, reps):
    return t if reps == 1 else jnp.concatenate([t] * reps, axis=1)


def _mod_kernel(c_ref, w_ref, b_ref, o_ref):
    e = _silu(c_ref[...]).astype(BF16)
    o_ref[...] = jnp.dot(e, w_ref[...].astype(BF16), preferred_element_type=F32) + b_ref[...]


def _modulation(cond, w_mod, b_mod):
    n_out = N_MOD * D_MODEL
    tn = 1536
    out = pl.pallas_call(
        _mod_kernel,
        grid=(DEPTH, n_out // tn),
        in_specs=[
            pl.BlockSpec((MOD_ROWS, D_MODEL), lambda l, j: (0, 0)),
            pl.BlockSpec((None, D_MODEL, tn), lambda l, j: (l, 0, j)),
            pl.BlockSpec((None, 1, tn), lambda l, j: (l, 0, j)),
        ],
        out_specs=pl.BlockSpec((None, MOD_ROWS, tn), lambda l, j: (l, 0, j)),
        out_shape=jax.ShapeDtypeStruct((DEPTH, MOD_ROWS, n_out), F32),
        compiler_params=_params(2),
        name="modulation",
    )(cond, w_mod, b_mod.reshape(DEPTH, 1, n_out))
    return out.reshape(DEPTH * MOD_ROWS * 3, 3, D_MODEL)


def _mod_spec(layer, sub, row_fn):
    return pl.BlockSpec((None, 3, D_MODEL), lambda i, *_: ((layer * MOD_ROWS + row_fn(i)) * 3 + sub, 0, 0))


def _cast_ffn2_slabs(wi_ref, wo_ref, wi_out, wo_out):
    wi_out[...] = wi_ref[...].astype(BF16)
    wo_out[...] = wo_ref[...].astype(BF16)


def _cast_ffn2_specs(layer, steps, w_in, w_out):
    shapes = [w_in.shape[1:], w_out.shape[1:]]
    assert all(k % (steps * 16) == 0 for k, _ in shapes)
    in_specs = [pl.BlockSpec((None, k // steps, n), lambda i: (layer, i, 0)) for k, n in shapes]
    out_specs = [pl.BlockSpec((k // steps, n), lambda i: (i, 0)) for k, n in shapes]
    return in_specs, out_specs, [jax.ShapeDtypeStruct(s, BF16) for s in shapes]


def _ffn_body(x, mod, g_ref, win_ref, wout_ref, between_chunks=None):
    h = _mod_norm(x, g_ref[...], mod).astype(BF16)
    acc = jnp.zeros(x.shape, F32)
    zeros = [None] * INTERLEAVE_SLACK
    for j in range(D_FF // FF_CHUNK):
        lo = j * FF_CHUNK
        a = jnp.dot(h, win_ref[:, lo:lo + FF_CHUNK].astype(BF16), preferred_element_type=F32)
        b = jnp.dot(h, win_ref[:, D_FF + lo:D_FF + lo + FF_CHUNK].astype(BF16), preferred_element_type=F32)
        act = (_silu(a) * b).astype(BF16)
        zero = zeros.pop(0)
        rows = pl.ds(lo, FF_CHUNK) if zero is None else pl.ds(pl.multiple_of(lo + zero, FF_CHUNK), FF_CHUNK)
        acc = acc + jnp.dot(act, wout_ref[rows, :].astype(BF16), preferred_element_type=F32)
        zeros.append(between_chunks(j) if between_chunks is not None else None)
    return x + (MACARON * mod[2:3]) * acc


def _ffn_kernel(*refs, nparts, nblk0):
    x_refs = refs[:nparts]
    mod_ref, g_ref, win_ref, wout_ref, o_ref = refs[nparts:]
    x = x_refs[0][...]
    if nparts == 2:
        x = jnp.where(pl.program_id(0) < nblk0, x, x_refs[1][...])
    o_ref[...] = _ffn_body(x, mod_ref[...], g_ref, win_ref, wout_ref)


def _ffn_weight_specs(layer):
    layer_spec = lambda shape: pl.BlockSpec((None,) + shape, lambda i: (layer, 0, 0), pipeline_mode=pl.Buffered(1))
    return [layer_spec((1, D_MODEL)), layer_spec((D_MODEL, 2 * D_FF)), layer_spec((D_FF, D_MODEL))]


def _ffn(parts, mod3, layer, sub, row_fn, g, w_in, w_out):
    nblks = [p.shape[0] // TM for p in parts]
    if len(parts) == 1:
        in_specs = [pl.BlockSpec((TM, D_MODEL), lambda i: (i, 0))]
    else:
        nb0 = nblks[0]
        in_specs = [pl.BlockSpec((TM, D_MODEL), lambda i: (jnp.minimum(i, nb0 - 1), 0)),
                    pl.BlockSpec((TM, D_MODEL), lambda i: (jnp.maximum(i - nb0, 0), 0))]
    in_specs += [_mod_spec(layer, sub, row_fn)] + _ffn_weight_specs(layer)
    n = sum(nblks) * TM
    return pl.pallas_call(
        functools.partial(_ffn_kernel, nparts=len(parts), nblk0=nblks[0]),
        grid=(n // TM,),
        in_specs=in_specs,
        out_specs=pl.BlockSpec((TM, D_MODEL), lambda i: (i, 0)),
        out_shape=jax.ShapeDtypeStruct((n, D_MODEL), F32),
        compiler_params=_params(1),
        name="ffn",
    )(*parts, mod3, g.reshape(DEPTH, 1, D_MODEL), w_in, w_out)


MLA_W1_COLS = Q_LORA + KV_LORA + 2 * CONV_CH + LANES
MLA_QK = MLA_HEADS * HEAD_PAD
KR_LANE = QK_NOPE


def _mla_kv(ckvn, krb, wkv_ref, kt_ref, v_ref, nseq):
    kv = jnp.dot(ckvn.astype(BF16), wkv_ref[...], preferred_element_type=F32)
    k = kv[:, :MLA_QK] + _tile_lanes(krb, MLA_HEADS)
    v_ref[...] = _with_ones_lane(kv[:, MLA_QK:]).astype(BF16)
    rows = k.shape[0] // nseq
    for s in range(nseq):
        kt_ref[s] = k[s * rows:(s + 1) * rows, :].T.astype(BF16)


def _mla_in_kernel(x_ref, mod_ref, g_ref, w1_ref, gq_ref, wqu_ref, gkv_ref, wkv_ref, *rest, rope, nseq):
    if rope:
        cos_ref, sa_ref, sb_ref, q_ref, kt_ref, v_ref, glu_ref = rest
    else:
        wi_ref, wo_ref, q_ref, kt_ref, v_ref, glu_ref, ckv_ref, kr_ref, wi_out, wo_out = rest
        _cast_ffn2_slabs(wi_ref, wo_ref, wi_out, wo_out)
    h = _mod_norm(x_ref[...], g_ref[...], mod_ref[...]).astype(BF16)
    proj = jnp.dot(h, w1_ref[...], preferred_element_type=F32)
    o1 = Q_LORA
    o2 = o1 + KV_LORA
    o3 = o2 + CONV_CH
    o4 = o3 + CONV_CH
    cq, ckv, ua, ub, krb = proj[:, :o1], proj[:, o1:o2], proj[:, o2:o3], proj[:, o3:o4], proj[:, o4:]
    glu_ref[...] = ua * jax.nn.sigmoid(ub)
    q = jnp.dot(_rms(cq, gq_ref[...]).astype(BF16), wqu_ref[...], preferred_element_type=F32)
    ckvn = _rms(ckv, gkv_ref[...])
    if rope:
        cos, sa, sb = cos_ref[...], sa_ref[...], sb_ref[...]
        quarter = QK_ROPE // 4
        krb = _rope(krb, cos, sa, sb, quarter)
        q = _rope(q, _tile_lanes(cos, MLA_HEADS), _tile_lanes(sa, MLA_HEADS), _tile_lanes(sb, MLA_HEADS), quarter)
    else:
        ckv_ref[...] = ckvn
        kr_ref[...] = krb[:, KR_LANE:KR_LANE + QK_ROPE]
    q_ref[...] = (q * ((QK_NOPE + QK_ROPE) ** -0.5 * LOG2E)).astype(BF16)
    _mla_kv(ckvn, krb, wkv_ref, kt_ref, v_ref, nseq)


def _mla_ctx_kernel(ckv_ref, krb_ref, wkv_ref, kt_ref, v_ref):
    _mla_kv(ckv_ref[...], krb_ref[...], wkv_ref, kt_ref, v_ref, 1)


def _mla_in(x, x_off, n, mod3, layer, row_fn, seq, p, tables, ffn2_f32=None):
    rope = tables is not None
    nseq = max(TM // seq, 1)
    rows = TM // nseq
    batch = n // seq
    tiles_per_seq = max(seq // TM, 1)
    row = lambda w: pl.BlockSpec((TM, w), lambda i: (i, 0))
    in_specs = [pl.BlockSpec((TM, D_MODEL), lambda i: (i + x_off, 0)), _mod_spec(layer, 1, row_fn),
                _const_spec((1, D_MODEL)),
                _const_spec((D_MODEL, MLA_W1_COLS)), _const_spec((1, Q_LORA)), _const_spec((Q_LORA, MLA_QK)),
                _const_spec((1, KV_LORA)), _const_spec((KV_LORA, 2 * MLA_QK))]
    args = [x, mod3, p["g_mix"], p["w1"], p["g_q"], p["w_qu"], p["g_kv"], p["w_kv"]]
    kt_spec = pl.BlockSpec((nseq, MLA_QK, rows), lambda i: (i // tiles_per_seq, 0, i % tiles_per_seq))
    out_specs = [row(MLA_QK), kt_spec, row(MLA_QK), row(CONV_CH)]
    out_shape = [jax.ShapeDtypeStruct((n, MLA_QK), BF16), jax.ShapeDtypeStruct((batch, MLA_QK, seq), BF16),
                 jax.ShapeDtypeStruct((n, MLA_QK), BF16), jax.ShapeDtypeStruct((n, CONV_CH), F32)]
    if rope:
        tab = pl.BlockSpec((TM, LANES), lambda i: (i % tiles_per_seq, 0))
        in_specs += [tab, tab, tab]
        args += list(tables)
    else:
        out_specs += [row(KV_LORA), row(QK_ROPE)]
        out_shape += [jax.ShapeDtypeStruct((n, KV_LORA), F32), jax.ShapeDtypeStruct((n, QK_ROPE), F32)]
        cast_in, cast_out, cast_shape = _cast_ffn2_specs(layer, n // TM, *ffn2_f32)
        in_specs += cast_in
        args += list(ffn2_f32)
        out_specs += cast_out
        out_shape += cast_shape
    return pl.pallas_call(
        functools.partial(_mla_in_kernel, rope=rope, nseq=nseq),
        grid=(n // TM,),
        in_specs=in_specs,
        out_specs=out_specs,
        out_shape=out_shape,
        compiler_params=_params(1),
        name="mla_in_rope" if rope else "mla_in",
    )(*args)


def _mla_ctx(ckv, krb, w_kv, seq):
    n = ckv.shape[0]
    return pl.pallas_call(
        _mla_ctx_kernel,
        grid=(n // seq,),
        in_specs=[pl.BlockSpec((seq, KV_LORA), lambda i: (i, 0)), pl.BlockSpec((seq, LANES), lambda i: (i, 0)),
                  _const_spec((KV_LORA, 2 * MLA_QK))],
        out_specs=[pl.BlockSpec((1, MLA_QK, seq), lambda i: (i, 0, 0)), pl.BlockSpec((seq, MLA_QK), lambda i: (i, 0))],
        out_shape=[jax.ShapeDtypeStruct((n // seq, MLA_QK, seq), BF16), jax.ShapeDtypeStruct((n, MLA_QK), BF16)],
        compiler_params=_params(1),
        name="mla_ctx",
    )(ckv, krb, w_kv)


GQA_Q = GQA_HEADS * GQA_HEAD_DIM
GQA_KV = GQA_KV_HEADS * GQA_HEAD_DIM
GQA_VPAD = GQA_KV_HEADS * HEAD_PAD
GQA_W_COLS = GQA_Q + 2 * GQA_KV + GQA_VPAD


def _head_rms(x, g):
    tm, w = x.shape
    lo = lax.broadcasted_iota(jnp.int32, (tm, LANES), 1) < GQA_HEAD_DIM
    outs = []
    for b in range(w // LANES):
        xb = x[:, b * LANES:(b + 1) * LANES]
        sq = xb * xb
        s_lo = jnp.sum(jnp.where(lo, sq, 0.0), axis=-1, keepdims=True)
        s_hi = jnp.sum(jnp.where(lo, 0.0, sq), axis=-1, keepdims=True)
        ms = jnp.where(lo, s_lo, s_hi) * (1.0 / GQA_HEAD_DIM)
        outs.append(xb * lax.rsqrt(ms + EPS))
    return jnp.concatenate(outs, axis=1) * g


def _gqa_in_kernel(x_ref, mod_ref, g_ref, w_ref, gq_ref, gk_ref, *rest, rope, nseq):
    if rope:
        cos_ref, sa_ref, sb_ref, q_ref, kt_ref, v_ref = rest
    else:
        wi_ref, wo_ref, q_ref, kt_ref, v_ref, kc_ref, vc_ref, wi_out, wo_out = rest
        _cast_ffn2_slabs(wi_ref, wo_ref, wi_out, wo_out)
    h = _mod_norm(x_ref[...], g_ref[...], mod_ref[...]).astype(BF16)
    proj = jnp.dot(h, w_ref[...], preferred_element_type=F32)
    q = _head_rms(proj[:, :GQA_Q], gq_ref[...])
    k = _head_rms(proj[:, GQA_Q:GQA_Q + GQA_KV], gk_ref[...])
    if rope:
        cos, sa, sb = cos_ref[...], sa_ref[...], sb_ref[...]
        quarter = GQA_HEAD_DIM // 4
        q = _rope(q, _tile_lanes(cos, GQA_Q // LANES), _tile_lanes(sa, GQA_Q // LANES),
                  _tile_lanes(sb, GQA_Q // LANES), quarter)
        k = _rope(k, _tile_lanes(cos, GQA_KV // LANES), _tile_lanes(sa, GQA_KV // LANES),
                  _tile_lanes(sb, GQA_KV // LANES), quarter)
    else:
        kc_ref[...] = k
        vc_ref[...] = proj[:, GQA_Q + GQA_KV:GQA_Q + 2 * GQA_KV]
    q_ref[...] = (q * (GQA_HEAD_DIM ** -0.5 * LOG2E)).astype(BF16)
    v_ref[...] = _with_ones_lane(proj[:, GQA_Q + 2 * GQA_KV:]).astype(BF16)
    rows = k.shape[0] // nseq
    for s in range(nseq):
        kt_ref[s] = k[s * rows:(s + 1) * rows, :].T.astype(BF16)


def _gqa_ctx_kernel(k_ref, v_ref, kt_ref, vo_ref):
    kt_ref[0] = k_ref[...].T.astype(BF16)
    vo_ref[...] = _with_ones_lane(v_ref[...]).astype(BF16)


def _gqa_in(x, x_off, n, mod3, layer, row_fn, seq, p, tables, ffn2_f32=None):
    rope = tables is not None
    nseq = max(TM // seq, 1)
    rows = TM // nseq
    batch = n // seq
    tiles_per_seq = max(seq // TM, 1)
    row = lambda w: pl.BlockSpec((TM, w), lambda i: (i, 0))
    in_specs = [pl.BlockSpec((TM, D_MODEL), lambda i: (i + x_off, 0)), _mod_spec(layer, 1, row_fn),
                _const_spec((1, D_MODEL)),
                _const_spec((D_MODEL, GQA_W_COLS)), _const_spec((1, GQA_Q)), _const_spec((1, GQA_KV))]
    args = [x, mod3, p["g_mix"], p["w"], p["g_q"], p["g_k"]]
    kt_spec = pl.BlockSpec((nseq, GQA_KV, rows), lambda i: (i // tiles_per_seq, 0, i % tiles_per_seq))
    out_specs = [row(GQA_Q), kt_spec, row(GQA_VPAD)]
    out_shape = [jax.ShapeDtypeStruct((n, GQA_Q), BF16), jax.ShapeDtypeStruct((batch, GQA_KV, seq), BF16),
                 jax.ShapeDtypeStruct((n, GQA_VPAD), BF16)]
    if rope:
        tab = pl.BlockSpec((TM, LANES), lambda i: (i % tiles_per_seq, 0))
        in_specs += [tab, tab, tab]
        args += list(tables)
    else:
        out_specs += [row(GQA_KV), row(GQA_KV)]
        out_shape += [jax.ShapeDtypeStruct((n, GQA_KV), F32), jax.ShapeDtypeStruct((n, GQA_KV), F32)]
        cast_in, cast_out, cast_shape = _cast_ffn2_specs(layer, n // TM, *ffn2_f32)
        in_specs += cast_in
        args += list(ffn2_f32)
        out_specs += cast_out
        out_shape += cast_shape
    return pl.pallas_call(
        functools.partial(_gqa_in_kernel, rope=rope, nseq=nseq),
        grid=(n // TM,),
        in_specs=in_specs,
        out_specs=out_specs,
        out_shape=out_shape,
        compiler_params=_params(1),
        name="gqa_in_rope" if rope else "gqa_in",
    )(*args)


def _gqa_ctx(k, vpad, seq):
    n = k.shape[0]
    return pl.pallas_call(
        _gqa_ctx_kernel,
        grid=(n // seq,),
        in_specs=[pl.BlockSpec((seq, GQA_KV), lambda i: (i, 0)), pl.BlockSpec((seq, GQA_VPAD), lambda i: (i, 0))],
        out_specs=[pl.BlockSpec((1, GQA_KV, seq), lambda i: (i, 0, 0)), pl.BlockSpec((seq, GQA_VPAD), lambda i: (i, 0))],
        out_shape=[jax.ShapeDtypeStruct((n // seq, GQA_KV, seq), BF16), jax.ShapeDtypeStruct((n, GQA_VPAD), BF16)],
        compiler_params=_params(1),
        name="gqa_ctx",
    )(k, vpad)


def _attn_kernel(*refs, nseg, bt, seq, nq, nkv, dq, dk, dv):
    q_ref = refs[0]
    segs = [(refs[1 + 2 * s], refs[2 + 2 * s]) for s in range(nseg)]
    o_ref = refs[1 + 2 * nseg]
    s_refs = refs[2 + 2 * nseg:]
    nks = [kt_ref.shape[2] for kt_ref, _ in segs]
    offs = [sum(nks[:s]) for s in range(nseg)]

    def scores(bi, r0, s_ref):
        rows = pl.ds(bi * seq + r0, Q_SUB)
        for j in range(nq):
            kv = j * nkv // nq
            qs = q_ref[rows, j * dq:(j + 1) * dq]
            for (kt_ref, _), off, nk in zip(segs, offs, nks):
                s_ref[j, :, off:off + nk] = jnp.dot(qs, kt_ref[bi, kv * dk:(kv + 1) * dk, :],
                                                    preferred_element_type=F32)

    def softmax_pv(bi, r0, s_ref):
        rows = pl.ds(bi * seq + r0, Q_SUB)
        outs = []
        for j in range(nq):
            kv = j * nkv // nq
            s = s_ref[j]
            p = jnp.exp2(s - jnp.max(s, axis=-1, keepdims=True)).astype(BF16)
            o = None
            for (_, v_ref), off, nk in zip(segs, offs, nks):
                pv = jnp.dot(p[:, off:off + nk], v_ref[bi * nk:(bi + 1) * nk, kv * HEAD_PAD:(kv + 1) * HEAD_PAD],
                             preferred_element_type=F32)
                o = pv if o is None else o + pv
            outs.append(o[:, :dv] / o[:, DENOM_LANE:DENOM_LANE + 1])
        o_ref[rows, :] = jnp.concatenate(outs, axis=1).astype(BF16)

    if seq == Q_SUB:
        assert bt == len(s_refs)
        for bi in range(bt):
            scores(bi, 0, s_refs[bi])
        for bi in range(bt):
            softmax_pv(bi, 0, s_refs[bi])
    else:
        n_sub = seq // Q_SUB
        assert bt == 1
        scores(0, 0, s_refs[0])
        for r in range(n_sub):
            if r + 1 < n_sub:
                scores(0, (r + 1) * Q_SUB, s_refs[(r + 1) % 2])
            softmax_pv(0, r * Q_SUB, s_refs[r % 2])


def _attention(q, segs, *, seq, heads, kv_heads, dq, dk, dv, bt, heads_per_step):
    n = q.shape[0]
    batch = n // seq
    nq = heads_per_step
    steps = heads // nq
    nkv = max(kv_heads // steps, 1)
    q_per_kv_block = steps // (kv_heads // nkv)
    in_specs = [pl.BlockSpec((bt * seq, nq * dq), lambda b, p: (b, p))]
    args = [q]
    for kt, v in segs:
        nk = kt.shape[2]
        in_specs.append(pl.BlockSpec((bt, nkv * dk, nk), lambda b, p: (b, p // q_per_kv_block, 0)))
        in_specs.append(pl.BlockSpec((bt * nk, nkv * HEAD_PAD), lambda b, p: (b, p // q_per_kv_block)))
        args += [kt, v]
    return pl.pallas_call(
        functools.partial(_attn_kernel, nseg=len(segs), bt=bt, seq=seq, nq=nq, nkv=nkv, dq=dq, dk=dk, dv=dv),
        grid=(batch // bt, steps),
        in_specs=in_specs,
        out_specs=pl.BlockSpec((bt * seq, nq * dv), lambda b, p: (b, p)),
        out_shape=jax.ShapeDtypeStruct((n, heads * dv), BF16),
        scratch_shapes=[pltpu.VMEM((nq, Q_SUB, sum(kt.shape[2] for kt, _ in segs)), F32)] * max(bt, 2),
        compiler_params=_params(2),
        name="attention",
    )(*args)


SUBLANES = 8
CONV_LEAD = HALO - CONV_PAD
CONV_WIN = CONV_ROWS + 2 * HALO


def _fill_pad(pad_ref, glu_ref, prev_ref, next_ref, first, last, nseq):
    rows = TM // nseq
    zeros = jnp.zeros((HALO, CONV_CH), F32)
    for s in range(nseq):
        pad_ref[s, 0:HALO, :] = zeros if prev_ref is None else jnp.where(first, zeros, prev_ref[...])
        pad_ref[s, HALO + rows:2 * HALO + rows, :] = (
            zeros if next_ref is None else jnp.where(last, zeros, next_ref[...]))
        pad_ref[s, HALO:HALO + rows, :] = glu_ref[s * rows:(s + 1) * rows, :]


CONV_LANE_BLOCKS = CONV_CH // LANES


def _conv_block(pad_ref, s, r0, lb, wdw_ref, bdw_ref, zero_ref, prev_zero):
    lanes = slice(lb * LANES, (lb + 1) * LANES)
    win = pad_ref[s, pl.ds(r0, CONV_WIN), lanes]
    acc = jnp.broadcast_to(bdw_ref[:, lanes], (CONV_ROWS, LANES))
    if prev_zero is not None:
        acc = acc + jnp.concatenate([prev_zero] * (CONV_ROWS // SUBLANES), axis=0)
    for phase in range(SUBLANES):
        shifted = win if phase == 0 else pltpu.roll(win, CONV_WIN - phase, 0)
        for a in range((CONV_W + CONV_LEAD) // SUBLANES + 1):
            k = a * SUBLANES + phase - CONV_LEAD
            if 0 <= k < CONV_W:
                acc = acc + shifted[a * SUBLANES:a * SUBLANES + CONV_ROWS] * wdw_ref[k:k + 1, lanes]
    peak = functools.reduce(jnp.maximum, [acc[g * SUBLANES:(g + 1) * SUBLANES]
                                          for g in range(CONV_ROWS // SUBLANES)])
    zero = (peak.astype(jnp.int32) & jnp.full((SUBLANES, LANES), zero_ref[0], jnp.int32)).astype(F32)
    return acc, zero


def _conv_finish(blocks, gln_ref, bln_ref):
    acc = jnp.concatenate(blocks, axis=1)
    mu = jnp.mean(acc, axis=-1, keepdims=True)
    d = acc - mu
    var = jnp.mean(d * d, axis=-1, keepdims=True)
    return _silu(d * lax.rsqrt(var + EPS) * gln_ref[...] + bln_ref[...])


def _out_ffn_kernel(*refs, conv, nseq, halo, tiles_per_seq, final):
    it = iter(refs)
    x_ref, modm_ref, modf_ref, attn_ref = next(it), next(it), next(it), next(it)
    if conv:
        glu0_ref = next(it)
        next0_ref = next(it) if halo else None
        glun_ref = next(it)
        prevn_ref, nextn_ref = (next(it), next(it)) if halo else (None, None)
        conv_w = [next(it) for _ in range(4)]
        zero_ref = next(it)
    wout_ref, gff_ref, win_ref, wff_ref = next(it), next(it), next(it), next(it)
    gfin_ref = next(it) if final else None
    y_ref = next(it)
    i = pl.program_id(0)
    rows = TM // nseq

    if conv:
        pad_ref, g_ref = next(it), next(it)

        wdw_ref, bdw_ref, gln_ref, bln_ref = conv_w
        units = [(c, lb) for c in range(TM // CONV_ROWS) for lb in range(CONV_LANE_BLOCKS)]
        hosts = D_FF // FF_CHUNK - INTERLEAVE_SLACK
        counts = [len(units) // hosts + (j < len(units) % hosts) for j in range(hosts)]
        starts = [sum(counts[:j]) for j in range(hosts + 1)]

        def run_units(todo, state):
            for c, lb in todo:
                s, r0 = divmod(c * CONV_ROWS, rows)
                acc, state[0] = _conv_block(pad_ref, s, r0, lb, wdw_ref, bdw_ref, zero_ref, state[0])
                state[1].append(acc)
                if lb == CONV_LANE_BLOCKS - 1:
                    g = _conv_finish(state[1], gln_ref, bln_ref)
                    g_ref[c * CONV_ROWS:(c + 1) * CONV_ROWS, :] = g.astype(BF16)
                    state[1] = []

        @pl.when(i == 0)
        def _():
            _fill_pad(pad_ref, glu0_ref, None, next0_ref, True, tiles_per_seq == 1, nseq)
            run_units(units, [None, []])

        out = (jnp.dot(attn_ref[...], wout_ref[:CONV_CH, :], preferred_element_type=F32)
               + jnp.dot(g_ref[...], wout_ref[CONV_CH:, :], preferred_element_type=F32))
        pos = (i + 1) % tiles_per_seq
        _fill_pad(pad_ref, glun_ref, prevn_ref, nextn_ref, pos == 0, pos == tiles_per_seq - 1, nseq)
        state = [None, []]

        def between(j):
            if j >= hosts:
                return None
            todo = units[starts[j]:starts[j + 1]]
            run_units(todo, state)
            return jnp.max(state[0]).astype(jnp.int32)
    else:
        out = jnp.dot(attn_ref[...], wout_ref[...], preferred_element_type=F32)
        between = None
    x = x_ref[...] + modm_ref[2:3, :] * out
    y = _ffn_body(x, modf_ref[...], gff_ref, win_ref, wff_ref, between)
    if final:
        y = _rms(y, gfin_ref[...])
    y_ref[...] = y


def _out_ffn(x, x_off, mod3, layer, row_fn, seq, attn, w_out, ffn_w, glu=None, conv_p=None, g_final=None):
    n = attn.shape[0]
    n_tiles = n // TM
    conv = glu is not None
    final = g_final is not None
    nseq = max(TM // seq, 1)
    tiles_per_seq = max(seq // TM, 1)
    halo = conv and tiles_per_seq > 1
    row = lambda w: pl.BlockSpec((TM, w), lambda i: (i, 0))
    in_specs = [pl.BlockSpec((TM, D_MODEL), lambda i: (i + x_off, 0)), _mod_spec(layer, 1, row_fn),
                _mod_spec(layer, 2, row_fn), row(attn.shape[1])]
    args = [x, mod3, mod3, attn]
    scratch = []
    if conv:
        per = TM // HALO
        last = n // HALO - 1
        nxt = lambda i: jnp.minimum(i + 1, n_tiles - 1)
        in_specs.append(pl.BlockSpec((TM, CONV_CH), lambda i: (0, 0)))
        args.append(glu)
        if halo:
            in_specs.append(pl.BlockSpec((HALO, CONV_CH), lambda i: (per, 0)))
            args.append(glu)
        in_specs.append(pl.BlockSpec((TM, CONV_CH), lambda i: (nxt(i), 0)))
        args.append(glu)
        if halo:
            in_specs += [pl.BlockSpec((HALO, CONV_CH), lambda i: (jnp.maximum(nxt(i) * per - 1, 0), 0)),
                         pl.BlockSpec((HALO, CONV_CH), lambda i: (jnp.minimum((nxt(i) + 1) * per, last), 0))]
            args += [glu, glu]
        in_specs += [_const_spec((CONV_W + 1, CONV_CH)), _const_spec((1, CONV_CH)), _const_spec((1, CONV_CH)),
                     _const_spec((1, CONV_CH)), pl.BlockSpec(memory_space=pltpu.SMEM)]
        args += [conv_p["w_dw"], conv_p["b_dw"], conv_p["g_ln"], conv_p["b_ln"], jnp.zeros((1,), jnp.int32)]
        scratch = [pltpu.VMEM((nseq, TM // nseq + 2 * HALO, CONV_CH), F32), pltpu.VMEM((TM, CONV_CH), BF16)]
    g_ff, w_in, w_ff = ffn_w
    in_specs += [_const_spec(w_out.shape), _ffn_weight_specs(layer)[0], _const_spec(w_in.shape),
                 _const_spec(w_ff.shape)]
    args += [w_out, g_ff.reshape(DEPTH, 1, D_MODEL), w_in, w_ff]
    if final:
        in_specs.append(_const_spec((1, D_MODEL)))
        args.append(g_final.reshape(1, D_MODEL))
    return pl.pallas_call(
        functools.partial(_out_ffn_kernel, conv=conv, nseq=nseq, halo=halo, tiles_per_seq=tiles_per_seq,
                          final=final),
        grid=(n_tiles,),
        in_specs=in_specs,
        out_specs=row(D_MODEL),
        out_shape=jax.ShapeDtypeStruct((n, D_MODEL), F32),
        scratch_shapes=scratch,
        compiler_params=pltpu.CompilerParams(dimension_semantics=("arbitrary",), vmem_limit_bytes=VMEM_LIMIT),
        name="out_ffn_conv" if conv else "out_ffn",
    )(*args)


def _rope_tables(d, offset, period):
    half = d // 2
    quarter = half // 2
    rows = DEC_SEQ // GRID_W
    pos_row = np.repeat(np.arange(rows), GRID_W)
    pos_col = np.tile(np.arange(GRID_W), rows)
    inv = ROPE_BASE ** (-np.arange(0, half, 2, dtype=np.float64) / half)
    ang_r = pos_row.astype(np.float64)[:, None] * inv[None, :]
    ang_c = pos_col.astype(np.float64)[:, None] * inv[None, :]
    zero = np.zeros((DEC_SEQ, quarter))
    cos = np.concatenate([np.cos(ang_r), np.cos(ang_r), np.cos(ang_c), np.cos(ang_c)], axis=1)
    sa = np.concatenate([-np.sin(ang_r), zero, -np.sin(ang_c), zero], axis=1)
    sb = np.concatenate([zero, np.sin(ang_r), zero, np.sin(ang_c)], axis=1)

    def embed(t, fill):
        blk = np.concatenate([np.full((DEC_SEQ, offset), fill), t,
                              np.full((DEC_SEQ, period - offset - d), fill)], axis=1)
        return jnp.asarray(np.tile(blk, (1, LANES // period)), dtype=F32)

    return embed(cos, 1.0), embed(sa, 0.0), embed(sb, 0.0)


def _pad_heads(w, heads, dim):
    k = w.shape[0]
    return jnp.pad(w.reshape(k, heads, dim), ((0, 0), (0, 0), (0, HEAD_PAD - dim))).reshape(k, heads * HEAD_PAD)


def _prep_mla(i, g_mix_l, w_in_a, g_q_lora, w_q_up, g_kv_lora, w_kv_up, w_dw, b_dw, g_conv_ln, b_conv_ln, w_out_a):
    w = w_in_a[i]
    o1 = Q_LORA
    o2 = o1 + KV_LORA
    o3 = o2 + QK_ROPE
    zeros = lambda c: jnp.zeros((D_MODEL, c), F32)
    w1 = jnp.concatenate([w[:, :o2], w[:, o3:], zeros(KR_LANE), w[:, o2:o3],
                          zeros(LANES - KR_LANE - QK_ROPE)], axis=1)
    kvu = w_kv_up[i].reshape(KV_LORA, MLA_HEADS, QK_NOPE + V_HEAD)
    w_kn = _pad_heads(kvu[:, :, :QK_NOPE].reshape(KV_LORA, MLA_HEADS * QK_NOPE), MLA_HEADS, QK_NOPE)
    w_v = _pad_heads(kvu[:, :, QK_NOPE:].reshape(KV_LORA, MLA_HEADS * V_HEAD), MLA_HEADS, V_HEAD)
    return dict(
        g_mix=g_mix_l.reshape(1, D_MODEL),
        w1=w1.astype(BF16),
        g_q=g_q_lora[i].reshape(1, Q_LORA),
        w_qu=_pad_heads(w_q_up[i], MLA_HEADS, QK_NOPE + QK_ROPE).astype(BF16),
        g_kv=g_kv_lora[i].reshape(1, KV_LORA),
        w_kv=jnp.concatenate([w_kn, w_v], axis=1).astype(BF16),
        w_dw=jnp.pad(w_dw[i], ((0, 1), (0, 0))),
        b_dw=b_dw[i].reshape(1, CONV_CH),
        g_ln=g_conv_ln[i].reshape(1, CONV_CH),
        b_ln=b_conv_ln[i].reshape(1, CONV_CH),
        w_out=w_out_a[i].astype(BF16),
    )


def _prep_gqa(i, g_mix_l, w_in_c, g_q_head, g_k_head, w_out_c):
    w = w_in_c[i]
    v_pad = _pad_heads(w[:, GQA_Q + GQA_KV:], GQA_KV_HEADS, GQA_HEAD_DIM)
    return dict(
        g_mix=g_mix_l.reshape(1, D_MODEL),
        w=jnp.concatenate([w, v_pad], axis=1).astype(BF16),
        g_q=jnp.tile(g_q_head[i], GQA_HEADS).reshape(1, GQA_Q),
        g_k=jnp.tile(g_k_head[i], GQA_KV_HEADS).reshape(1, GQA_KV),
        w_out=w_out_c[i].astype(BF16),
    )


SAMPLE_HEADS_PER_STEP = 4
PROMPT_SEQS_PER_STEP = 4


def _mixer_ffn(x_all, x_off, n, mod3, l, row_fn, seq, p, ctx, tables, g_ff2, ffn2_w, g_final):
    sample = ctx is not None
    in_proj, n_core = (_mla_in, 4) if l % 2 == 0 else (_gqa_in, 3)
    outs = in_proj(x_all, x_off, n, mod3, l, row_fn, seq, p, tables["mla" if l % 2 == 0 else "gqa"] if sample else None,
                   None if sample else ffn2_w)
    if not sample:
        outs, ffn2_w = outs[:-2], tuple(outs[-2:])
    ffn_w = (g_ff2,) + tuple(ffn2_w)
    if l % 2 == 0:
        q, kt, v, glu = outs[:4]
        segs = [(kt, v)]
        if sample:
            segs = [_mla_ctx(ctx["mla_ckv"], ctx["mla_krb"], p["w_kv"], ctx["past"])] + segs
            hps, bt = SAMPLE_HEADS_PER_STEP, 1
        else:
            hps, bt = MLA_HEADS, PROMPT_SEQS_PER_STEP
        attn = _attention(q, segs, seq=seq, heads=MLA_HEADS, kv_heads=MLA_HEADS, dq=HEAD_PAD, dk=HEAD_PAD,
                          dv=V_HEAD, bt=bt, heads_per_step=hps)
        return _out_ffn(x_all, x_off, mod3, l, row_fn, seq, attn, p["w_out"], ffn_w, glu=glu, conv_p=p,
                        g_final=g_final), outs[n_core:], ffn2_w
    q, kt, v = outs[:3]
    segs = [(kt, v)]
    if sample:
        segs = [_gqa_ctx(ctx["gqa_k"], ctx["gqa_vpad"], ctx["past"])] + segs
        hps, bt = SAMPLE_HEADS_PER_STEP, 1
    else:
        hps, bt = GQA_HEADS, PROMPT_SEQS_PER_STEP
    attn = _attention(q, segs, seq=seq, heads=GQA_HEADS, kv_heads=GQA_KV_HEADS, dq=GQA_HEAD_DIM,
                      dk=GQA_HEAD_DIM, dv=GQA_HEAD_DIM, bt=bt, heads_per_step=hps)
    return (_out_ffn(x_all, x_off, mod3, l, row_fn, seq, attn, p["w_out"], ffn_w, g_final=g_final), outs[n_core:],
            ffn2_w)


def kernel(x_prompt, x_sample, cache_mla_ckv, cache_mla_krope, cache_gqa_k, cache_gqa_v, c, c_ctx, g_ff1, w_ff1_in, w_ff1_out, g_mix, g_ff2, w_ff2_in, w_ff2_out, w_mod, b_mod, w_in_a, g_q_lora, w_q_up, g_kv_lora, w_kv_up, w_dw, b_dw, g_conv_ln, b_conv_ln, w_out_a, w_in_c, g_q_head, g_k_head, w_out_c, g_final):
    batch, seq, _ = x_prompt.shape
    dec_batch, dec_seq, _ = x_sample.shape
    past = cache_mla_ckv.shape[2]
    assert DEPTH == 2 and dec_seq == DEC_SEQ and 1 + dec_batch <= MOD_ROWS

    cond = jnp.concatenate([c_ctx[None, :], c, jnp.zeros((MOD_ROWS - 1 - dec_batch, D_MODEL), F32)], axis=0)
    mod3 = _modulation(cond, w_mod, b_mod)

    layers = [
        _prep_mla(0, g_mix[0], w_in_a, g_q_lora, w_q_up, g_kv_lora, w_kv_up, w_dw, b_dw, g_conv_ln, b_conv_ln,
                  w_out_a),
        _prep_gqa(0, g_mix[1], w_in_c, g_q_head, g_k_head, w_out_c),
    ]
    tables = dict(mla=_rope_tables(QK_ROPE, KR_LANE, HEAD_PAD), gqa=_rope_tables(GQA_HEAD_DIM, 0, GQA_HEAD_DIM))

    ctx = dict(
        past=past,
        mla_ckv=cache_mla_ckv[:, 0].reshape(dec_batch * past, KV_LORA),
        mla_krb=jnp.pad(cache_mla_krope[:, 0].reshape(dec_batch * past, QK_ROPE),
                        ((0, 0), (KR_LANE, LANES - KR_LANE - QK_ROPE))),
        gqa_k=cache_gqa_k[:, 0].reshape(dec_batch * past, GQA_KV),
        gqa_vpad=jnp.pad(cache_gqa_v[:, 0], ((0, 0), (0, 0), (0, 0), (0, HEAD_PAD - GQA_HEAD_DIM))
                         ).reshape(dec_batch * past, GQA_VPAD),
    )

    n_p, n_s = batch * seq, dec_batch * dec_seq
    nb_p = n_p // TM
    row_p = lambda i: 0
    row_s = lambda i: 1 + (i * TM) // DEC_SEQ
    row_all = lambda i: jnp.where(i < nb_p, 0, 1 + ((i - nb_p) * TM) // DEC_SEQ)

    parts = [x_prompt.reshape(n_p, D_MODEL), x_sample.reshape(n_s, D_MODEL)]
    saved = []
    for l in range(DEPTH):
        gf = g_final if l == DEPTH - 1 else None
        x_all = _ffn(parts, mod3, l, 0, row_all, g_ff1, w_ff1_in, w_ff1_out)
        xp, st, w2 = _mixer_ffn(x_all, 0, n_p, mod3, l, row_p, seq, layers[l], None, tables, g_ff2,
                                (w_ff2_in, w_ff2_out), gf)
        xs, _, _ = _mixer_ffn(x_all, nb_p, n_s, mod3, l, row_s, dec_seq, layers[l], ctx, tables, g_ff2, w2, gf)
        saved.append(st)
        parts = [xp, xs]
    y_prompt, y_sample = parts
    (ckv_new, kr_new), (k_new, v_new) = saved

    return (y_prompt.reshape(batch, seq, D_MODEL),
            y_sample.reshape(dec_batch, dec_seq, D_MODEL),
            ckv_new.reshape(batch, 1, seq, KV_LORA),
            kr_new.reshape(batch, 1, seq, QK_ROPE),
            k_new.reshape(batch, 1, seq, GQA_KV_HEADS, GQA_HEAD_DIM),
            v_new.reshape(batch, 1, seq, GQA_KV_HEADS, GQA_HEAD_DIM))
```
